```python
import math
import jax, jax.numpy as jnp
from jax import lax
import numpy as np

D_MODEL = 1024
BATCH = 16
SEQ = 4096
DEPTH = 4

D_FF = 2816
MLA_HEADS = 8
MLA_Q_RANK = 384
MLA_KV_RANK = 256
MLA_NOPE_DIM = 64
MLA_ROPE_DIM = 32
MLA_V_DIM = 64
ROPE_THETA = 10000.0
DIL_PAIRS = ((128, 1), (512, 4), (2048, 16))
DIL_HEADS_PER_GROUP = 4
DIL_HEADS = DIL_HEADS_PER_GROUP * len(DIL_PAIRS)
DIL_HEAD_DIM = 128
SSM_WIDTH = 512
SSM_GROUP_SIZE = 16
SSM_GROUPS = SSM_WIDTH // SSM_GROUP_SIZE
SSM_STATE = 64
SSM_CHUNK = 128
N_BRANCH = 3
BRANCH_WIDTH = 512
Q_BLOCK = 128
N_IN = MLA_Q_RANK + MLA_KV_RANK + MLA_ROPE_DIM + 3 * DIL_HEADS * DIL_HEAD_DIM + SSM_WIDTH + N_BRANCH * D_MODEL
SPLIT_POINTS = (
    MLA_Q_RANK,
    MLA_Q_RANK + MLA_KV_RANK,
    MLA_Q_RANK + MLA_KV_RANK + MLA_ROPE_DIM,
    MLA_Q_RANK + MLA_KV_RANK + MLA_ROPE_DIM + 3 * DIL_HEADS * DIL_HEAD_DIM,
    MLA_Q_RANK + MLA_KV_RANK + MLA_ROPE_DIM + 3 * DIL_HEADS * DIL_HEAD_DIM + SSM_WIDTH,
)
DEEPNORM_ALPHA = (2 * DEPTH) ** 0.25
DEEPNORM_BETA = (8 * DEPTH) ** -0.25
MACARON_WEIGHT = 0.5
LN_EPS = 1e-5
RMS_EPS = 1e-6

kernel_name = 'hybrid_mla_dilated_s5_macaron_deepnorm'


def layer_norm(x, g, b):
    xf = x.astype(jnp.float32)
    mu = jnp.mean(xf, axis=-1, keepdims=True)
    var = jnp.mean(jnp.square(xf - mu), axis=-1, keepdims=True)
    return ((xf - mu) * lax.rsqrt(var + LN_EPS)).astype(x.dtype) * g + b


def rms_norm(x, g):
    xf = x.astype(jnp.float32)
    return (xf * lax.rsqrt(jnp.mean(xf * xf, axis=-1, keepdims=True) + RMS_EPS)).astype(x.dtype) * g


def modulate(h, shift, scale):
    return h * (1.0 + scale[:, None, :]) + shift[:, None, :]


def post_norm(h, out, res_w, gate, g, b):
    return layer_norm(DEEPNORM_ALPHA * h + res_w * gate[:, None, :] * out, g, b)


def swiglu(h, w1, w3, w2):
    return (jax.nn.silu(h @ w1) * (h @ w3)) @ w2


def alibi_slopes(n):
    return jnp.exp2(-8.0 * (jnp.arange(n, dtype=jnp.float32) + 1.0) / n)


def rope(t, positions):
    half = MLA_ROPE_DIM // 2
    inv_freq = jnp.power(ROPE_THETA, -jnp.arange(half, dtype=jnp.float32) / half)
    ang = positions.astype(jnp.float32)[..., None] * inv_freq
    ang = ang.reshape(ang.shape[:2] + (1,) * (t.ndim - 3) + (half,))
    cos, sin = jnp.cos(ang), jnp.sin(ang)
    tf = t.astype(jnp.float32)
    t1, t2 = tf[..., :half], tf[..., half:]
    return jnp.concatenate([t1 * cos - t2 * sin, t2 * cos + t1 * sin], axis=-1).astype(t.dtype)


def causal_block_attention(q, k, v, scale):
    B, S, H, _ = q.shape
    nb = S // Q_BLOCK
    kpos = jnp.arange(S)

    def one_block(j):
        qb = lax.dynamic_slice_in_dim(q, j * Q_BLOCK, Q_BLOCK, axis=1)
        s = jnp.einsum('bqhd,bkhd->bhqk', qb, k, preferred_element_type=jnp.float32) * scale
        qpos = j * Q_BLOCK + jnp.arange(Q_BLOCK)
        s = jnp.where(kpos[None, :] <= qpos[:, None], s, -jnp.inf)
        p = jax.nn.softmax(s, axis=-1).astype(v.dtype)
        return jnp.einsum('bhqk,bkhd->bqhd', p, v)

    out = lax.map(one_block, jnp.arange(nb))
    return out.transpose(1, 0, 2, 3, 4).reshape(B, S, H, v.shape[-1])


def banded_causal_attention(q, k, v, band, slopes, stride, scale):
    N, L, H, dh = q.shape
    nb = -(-L // Q_BLOCK)
    pad = nb * Q_BLOCK - L
    q = jnp.pad(q, ((0, 0), (0, pad), (0, 0), (0, 0)))
    k = jnp.pad(k, ((0, 0), (Q_BLOCK, pad), (0, 0), (0, 0)))
    v = jnp.pad(v, ((0, 0), (Q_BLOCK, pad), (0, 0), (0, 0)))
    qi = jnp.arange(Q_BLOCK)
    ki = jnp.arange(2 * Q_BLOCK)
    rel = qi[:, None] - ki[None, :] + Q_BLOCK
    band_ok = (rel >= 0) & (rel <= band)
    bias = -(slopes.astype(jnp.float32) * stride)[:, None, None] * rel.astype(jnp.float32)

    def one_block(j):
        qb = lax.dynamic_slice_in_dim(q, j * Q_BLOCK, Q_BLOCK, axis=1)
        kb = lax.dynamic_slice_in_dim(k, j * Q_BLOCK, 2 * Q_BLOCK, axis=1)
        vb = lax.dynamic_slice_in_dim(v, j * Q_BLOCK, 2 * Q_BLOCK, axis=1)
        s = jnp.einsum('nqhd,nkhd->nhqk', qb, kb, preferred_element_type=jnp.float32) * scale + bias
        key_ok = band_ok & ((j * Q_BLOCK - Q_BLOCK + ki) >= 0)[None, :]
        s = jnp.where(key_ok, s, -jnp.inf)
        lse = jax.nn.logsumexp(s, axis=-1)
        p = jnp.exp(s - lse[..., None]).astype(v.dtype)
        o = jnp.einsum('nhqk,nkhd->nqhd', p, vb)
        return o, lse.transpose(0, 2, 1)

    o, lse = lax.map(one_block, jnp.arange(nb))
    o = o.transpose(1, 0, 2, 3, 4).reshape(N, nb * Q_BLOCK, H, dh)[:, :L]
    lse = lse.transpose(1, 0, 2, 3).reshape(N, nb * Q_BLOCK, H)[:, :L]
    return o, lse


def to_strided(t, dil):
    B, S, H, d = t.shape
    return t.reshape(B, S // dil, dil, H, d).transpose(0, 2, 1, 3, 4).reshape(B * dil, S // dil, H, d)


def from_strided(t, B, dil):
    L = t.shape[1]
    rest = t.shape[2:]
    t = t.reshape((B, dil, L) + rest)
    t = jnp.moveaxis(t, 1, 2)
    return t.reshape((B, L * dil) + rest)


def dilated_attention(q, k, v):
    B = q.shape[0]
    slopes = alibi_slopes(DIL_HEADS)
    outs, lses = [], []
    for g, (window, dil) in enumerate(DIL_PAIRS):
        lo, hi = g * DIL_HEADS_PER_GROUP, (g + 1) * DIL_HEADS_PER_GROUP
        o, lse = banded_causal_attention(
            to_strided(q[:, :, lo:hi], dil), to_strided(k[:, :, lo:hi], dil), to_strided(v[:, :, lo:hi], dil),
            window // dil, slopes[lo:hi], dil, DIL_HEAD_DIM ** -0.5)
        outs.append(from_strided(o, B, dil))
        lses.append(from_strided(lse, B, dil))
    w = jax.nn.softmax(jnp.stack(lses, axis=0), axis=0)
    o = jnp.stack(outs, axis=0)
    return jnp.einsum('gbsh,gbshd->bshd', w.astype(o.dtype), o)


def s5_ssm(u, lam_re, lam_im, log_dt, b_re, b_im, c_re, c_im, d_skip):
    B, S, _ = u.shape
    f32 = jnp.float32
    uf = u.astype(f32)
    lam = lax.complex(lam_re.astype(f32), lam_im.astype(f32))
    dt = jnp.exp(log_dt.astype(f32))[:, None]
    lam_bar = jnp.exp(lam * dt)
    b_bar = ((lam_bar - 1.0) / lam)[..., None] * lax.complex(b_re.astype(f32), b_im.astype(f32))
    c_mat = lax.complex(c_re.astype(f32), c_im.astype(f32))
    nc = S // SSM_CHUNK
    u_chunks = uf.reshape(B, nc, SSM_CHUNK, SSM_GROUPS, SSM_GROUP_SIZE).transpose(1, 2, 0, 3, 4)
    a = jnp.broadcast_to(lam_bar, (SSM_CHUNK, 1, SSM_GROUPS, SSM_STATE))

    def combine(e1, e2):
        a1, b1 = e1
        a2, b2 = e2
        return a2 * a1, a2 * b1 + b2

    def chunk_step(state, u_c):
        bu = jnp.einsum('gph,tbgh->tbgp', b_bar, u_c.astype(jnp.complex64))
        a_cum, xs = lax.associative_scan(combine, (a, bu), axis=0)
        xs = xs + a_cum * state[None]
        y = jnp.einsum('gkp,tbgp->tbgk', c_mat, xs).real
        return xs[-1], y

    state0 = jnp.zeros((B, SSM_GROUPS, SSM_STATE), jnp.complex64)
    _, y = lax.scan(chunk_step, state0, u_chunks)
    y = y.transpose(2, 0, 1, 3, 4).reshape(B, S, SSM_WIDTH)
    return y + d_skip.astype(f32) * uf


def token_mixer(h, positions, w_in, b_in, q_norm, kv_norm, w_qb, w_kvb,
                lam_re, lam_im, log_dt, b_re, b_im, c_re, c_im, d_skip, w_glu, b_glu, w_br, w_out):
    B, S, _ = h.shape
    proj = h @ w_in + b_in
    q_a, kv_a, k_rope, dil_qkv, u, gate_logits = jnp.split(proj, SPLIT_POINTS, axis=-1)

    q = (rms_norm(q_a, q_norm) @ w_qb).reshape(B, S, MLA_HEADS, MLA_NOPE_DIM + MLA_ROPE_DIM)
    q = jnp.concatenate([q[..., :MLA_NOPE_DIM], rope(q[..., MLA_NOPE_DIM:], positions)], axis=-1)
    kv = (rms_norm(kv_a, kv_norm) @ w_kvb).reshape(B, S, MLA_HEADS, MLA_NOPE_DIM + MLA_V_DIM)
    k_pe = jnp.broadcast_to(rope(k_rope, positions)[:, :, None, :], (B, S, MLA_HEADS, MLA_ROPE_DIM))
    k = jnp.concatenate([kv[..., :MLA_NOPE_DIM], k_pe], axis=-1)
    y_mla = causal_block_attention(q, k, kv[..., MLA_NOPE_DIM:], (MLA_NOPE_DIM + MLA_ROPE_DIM) ** -0.5)
    y_mla = y_mla.reshape(B, S, BRANCH_WIDTH)

    qkv = dil_qkv.reshape(B, S, 3, DIL_HEADS, DIL_HEAD_DIM)
    y_dil = dilated_attention(qkv[:, :, 0], qkv[:, :, 1], qkv[:, :, 2]).reshape(B, S, BRANCH_WIDTH)

    y_ssm = jax.nn.gelu(s5_ssm(u, lam_re, lam_im, log_dt, b_re, b_im, c_re, c_im, d_skip)).astype(h.dtype)
    y_ssm = y_ssm * jax.nn.sigmoid(y_ssm @ w_glu + b_glu)

    gates = jax.nn.sigmoid(gate_logits).reshape(B, S, N_BRANCH, D_MODEL)
    merged = (gates[:, :, 0] * (y_mla @ w_br[0])
              + gates[:, :, 1] * (y_dil @ w_br[1])
              + gates[:, :, 2] * (y_ssm @ w_br[2]))
    return merged @ w_out


def setup_inputs(seed: int = 0) -> dict:
    key = jax.random.key(seed)
    ks = iter(jax.random.split(key, 40))
    f32 = jnp.float32
    L, D = DEPTH, D_MODEL

    def nrm(shape, scale):
        return jax.random.normal(next(ks), shape, f32) * scale

    x = nrm((BATCH, SEQ, D), 1.0)
    c = nrm((BATCH, D), 1.0)
    offset = jax.random.randint(next(ks), (BATCH, 1), 0, 1024, dtype=jnp.int32)
    positions = offset + jnp.arange(SEQ, dtype=jnp.int32)[None, :]
    w_ada = nrm((L, D, 9 * D), D ** -0.5)
    b_ada = nrm((L, 9 * D), 0.02)
    ln_g = 1.0 + nrm((L, 3, D), 0.02)
    ln_b = nrm((L, 3, D), 0.02)
    ffn_w1 = nrm((L, 2, D, D_FF), D ** -0.5)
    ffn_w3 = nrm((L, 2, D, D_FF), D ** -0.5)
    ffn_w2 = nrm((L, 2, D_FF, D), D_FF ** -0.5 * DEEPNORM_BETA)
    w_in = nrm((L, D, N_IN), D ** -0.5)
    b_in = nrm((L, N_IN), 0.02)
    mla_q_norm = 1.0 + nrm((L, MLA_Q_RANK), 0.02)
    mla_kv_norm = 1.0 + nrm((L, MLA_KV_RANK), 0.02)
    mla_w_qb = nrm((L, MLA_Q_RANK, MLA_HEADS * (MLA_NOPE_DIM + MLA_ROPE_DIM)), MLA_Q_RANK ** -0.5)
    mla_w_kvb = nrm((L, MLA_KV_RANK, MLA_HEADS * (MLA_NOPE_DIM + MLA_V_DIM)), MLA_KV_RANK ** -0.5)
    n_idx = jnp.arange(SSM_STATE, dtype=f32)
    ssm_lambda_re = -0.5 + nrm((L, SSM_GROUPS, SSM_STATE), 0.01)
    ssm_lambda_im = math.pi * n_idx + nrm((L, SSM_GROUPS, SSM_STATE), 0.01)
    ssm_log_dt = jax.random.uniform(next(ks), (L, SSM_GROUPS), f32, math.log(1e-3), math.log(1e-1))
    ssm_b_re = nrm((L, SSM_GROUPS, SSM_STATE, SSM_GROUP_SIZE), (2 * SSM_GROUP_SIZE) ** -0.5)
    ssm_b_im = nrm((L, SSM_GROUPS, SSM_STATE, SSM_GROUP_SIZE), (2 * SSM_GROUP_SIZE) ** -0.5)
    ssm_c_re = nrm((L, SSM_GROUPS, SSM_GROUP_SIZE, SSM_STATE), (2 * SSM_STATE) ** -0.5)
    ssm_c_im = nrm((L, SSM_GROUPS, SSM_GROUP_SIZE, SSM_STATE), (2 * SSM_STATE) ** -0.5)
    ssm_d = nrm((L, SSM_WIDTH), 1.0)
    ssm_w_glu = nrm((L, SSM_WIDTH, SSM_WIDTH), SSM_WIDTH ** -0.5)
    ssm_b_glu = nrm((L, SSM_WIDTH), 0.02)
    w_br = nrm((L, N_BRANCH, BRANCH_WIDTH, D), BRANCH_WIDTH ** -0.5)
    w_out = nrm((L, D, D), D ** -0.5 * DEEPNORM_BETA)
    return {
        'x': x, 'c': c, 'positions': positions,
        'w_ada': w_ada, 'b_ada': b_ada, 'ln_g': ln_g, 'ln_b': ln_b,
        'ffn_w1': ffn_w1, 'ffn_w3': ffn_w3, 'ffn_w2': ffn_w2,
        'w_in': w_in, 'b_in': b_in,
        'mla_q_norm': mla_q_norm, 'mla_kv_norm': mla_kv_norm, 'mla_w_qb': mla_w_qb, 'mla_w_kvb': mla_w_kvb,
        'ssm_lambda_re': ssm_lambda_re, 'ssm_lambda_im': ssm_lambda_im, 'ssm_log_dt': ssm_log_dt,
        'ssm_b_re': ssm_b_re, 'ssm_b_im': ssm_b_im, 'ssm_c_re': ssm_c_re, 'ssm_c_im': ssm_c_im,
        'ssm_d': ssm_d, 'ssm_w_glu': ssm_w_glu, 'ssm_b_glu': ssm_b_glu,
        'w_br': w_br, 'w_out': w_out,
    }


def reference(x, c, positions, w_ada, b_ada, ln_g, ln_b, ffn_w1, ffn_w3, ffn_w2, w_in, b_in,
              mla_q_norm, mla_kv_norm, mla_w_qb, mla_w_kvb,
              ssm_lambda_re, ssm_lambda_im, ssm_log_dt, ssm_b_re, ssm_b_im, ssm_c_re, ssm_c_im,
              ssm_d, ssm_w_glu, ssm_b_glu, w_br, w_out):
    B, S, D = x.shape
    h = x
    cond = jax.nn.silu(c)
    for l in range(DEPTH):
        ada = (cond @ w_ada[l] + b_ada[l]).reshape(B, 3, 3, D)
        f = swiglu(modulate(h, ada[:, 0, 0], ada[:, 0, 1]), ffn_w1[l, 0], ffn_w3[l, 0], ffn_w2[l, 0])
        h = post_norm(h, f, MACARON_WEIGHT, ada[:, 0, 2], ln_g[l, 0], ln_b[l, 0])
        m = token_mixer(modulate(h, ada[:, 1, 0], ada[:, 1, 1]), positions, w_in[l], b_in[l],
                        mla_q_norm[l], mla_kv_norm[l], mla_w_qb[l], mla_w_kvb[l],
                        ssm_lambda_re[l], ssm_lambda_im[l], ssm_log_dt[l], ssm_b_re[l], ssm_b_im[l],
                        ssm_c_re[l], ssm_c_im[l], ssm_d[l], ssm_w_glu[l], ssm_b_glu[l], w_br[l], w_out[l])
        h = post_norm(h, m, 1.0, ada[:, 1, 2], ln_g[l, 1], ln_b[l, 1])
        f = swiglu(modulate(h, ada[:, 2, 0], ada[:, 2, 1]), ffn_w1[l, 1], ffn_w3[l, 1], ffn_w2[l, 1])
        h = post_norm(h, f, MACARON_WEIGHT, ada[:, 2, 2], ln_g[l, 2], ln_b[l, 2])
    return h
```

```python
import functools
import math

import numpy as np
import jax
import jax.numpy as jnp
from jax import lax
from jax.experimental import pallas as pl
from jax.experimental.pallas import tpu as pltpu

D_MODEL = 1024
DEPTH = 4
D_FF = 2816
MLA_HEADS = 8
MLA_Q_RANK = 384
MLA_KV_RANK = 256
MLA_NOPE_DIM = 64
MLA_ROPE_DIM = 32
MLA_V_DIM = 64
ROPE_THETA = 10000.0
DIL_PAIRS = ((128, 1), (512, 4), (2048, 16))
DIL_HPG = 4
DIL_HEADS = DIL_HPG * len(DIL_PAIRS)
DIL_HEAD_DIM = 128
DIL_BAND = 128
SSM_WIDTH = 512
SSM_GROUP_SIZE = 16
SSM_GROUPS = SSM_WIDTH // SSM_GROUP_SIZE
SSM_STATE = 64
N_BRANCH = 3
BRANCH_WIDTH = 512
DEEPNORM_ALPHA = (2 * DEPTH) ** 0.25
MACARON_WEIGHT = 0.5
LN_EPS = 1e-5
RMS_EPS = 1e-6

LANES = 128
V7X_VMEM_LIMIT_BYTES = 56 * 1024 * 1024

FF_CHUNK = 256
N_FF_CHUNKS = D_FF // FF_CHUNK
MLA_HEAD_PAD = 128
MLA_QK_WIDTH = MLA_HEADS * MLA_HEAD_PAD
MLA_V_WIDTH = MLA_HEADS * MLA_V_DIM
MLA_SCALE = (MLA_NOPE_DIM + MLA_ROPE_DIM) ** -0.5
DIL_SCALE = DIL_HEAD_DIM ** -0.5
DIL_GROUP_WIDTH = DIL_HPG * DIL_HEAD_DIM
MLA_A_WIDTH = 768
COL_Q1 = 1024
COL_U = COL_Q1 + 3 * DIL_GROUP_WIDTH
COL_GATES = COL_U + SSM_WIDTH
N_MAIN = COL_GATES + N_BRANCH * D_MODEL
PROJ_CHUNK = 512
SSM_BLOCKS = 4
SSM_BLOCK_CH = SSM_WIDTH // SSM_BLOCKS
SSM_BLOCK_STATES = SSM_GROUPS * SSM_STATE // SSM_BLOCKS
SSM_SLABS = SSM_BLOCK_STATES // LANES
SSM_TT = 128
SSM_PITCH = 136

_F32 = jnp.float32
_BF16 = jnp.bfloat16
_NEG = -1e30


def _cparams(n_axes):
    return pltpu.CompilerParams(
        dimension_semantics=("arbitrary",) * n_axes,
        vmem_limit_bytes=V7X_VMEM_LIMIT_BYTES,
    )


def _resident(block_shape, index_map):
    return pl.BlockSpec(block_shape, index_map, pipeline_mode=pl.Buffered(1))


def _layer_norm(y, g, b):
    mu = jnp.mean(y, axis=-1, keepdims=True)
    yc = y - mu
    var = jnp.mean(yc * yc, axis=-1, keepdims=True)
    return yc * lax.rsqrt(var + LN_EPS) * g + b


def _dot(a, b):
    return jnp.dot(a, b, preferred_element_type=_F32)


def _dot_nt(a, b):
    return lax.dot_general(a, b, (((1,), (1,)), ((), ())), preferred_element_type=_F32)


def _ada_kernel(c_ref, w_ref, b_ref, o_ref):
    c = c_ref[...]
    cond = (c * jax.nn.sigmoid(c)).astype(_BF16)
    o_ref[0] = _dot(cond, w_ref[0].astype(_BF16)) + b_ref[0]


def _ada_call(c, w_ada, b_ada):
    depth, d, n = w_ada.shape
    bsz = c.shape[0]
    tn = 1536
    return pl.pallas_call(
        _ada_kernel,
        grid=(depth, n // tn),
        in_specs=[
            pl.BlockSpec((bsz, d), lambda l, j: (0, 0)),
            pl.BlockSpec((1, d, tn), lambda l, j: (l, 0, j)),
            pl.BlockSpec((1, 1, tn), lambda l, j: (l, 0, j)),
        ],
        out_specs=pl.BlockSpec((1, bsz, tn), lambda l, j: (l, 0, j)),
        out_shape=jax.ShapeDtypeStruct((depth, bsz, n), _F32),
        compiler_params=_cparams(2),
        name="ada",
    )(c, w_ada, b_ada.reshape(depth, 1, n))


def _rope_freq_lanes():
    half = MLA_ROPE_DIM // 2
    inv_freq = np.power(np.float32(ROPE_THETA), -np.arange(half, dtype=np.float32) / np.float32(half))
    f = np.zeros((1, MLA_HEAD_PAD), np.float32)
    f[0, MLA_NOPE_DIM:MLA_NOPE_DIM + half] = inv_freq
    f[0, MLA_NOPE_DIM + half:MLA_NOPE_DIM + 2 * half] = inv_freq
    return f


def _rope_kernel(pos_ref, f_ref, cos_ref, sin_ref):
    ang = pos_ref[0].astype(_F32) * f_ref[...]
    cos_ref[0] = jnp.cos(ang)
    sin_ref[0] = jnp.sin(ang)


def _rope_call(positions):
    bsz, s = positions.shape
    tm = min(s, 512)
    out = jax.ShapeDtypeStruct((bsz, s, MLA_HEAD_PAD), _F32)
    return pl.pallas_call(
        _rope_kernel,
        grid=(bsz, s // tm),
        in_specs=[
            pl.BlockSpec((1, tm, 1), lambda b, i: (b, i, 0)),
            pl.BlockSpec((1, MLA_HEAD_PAD), lambda b, i: (0, 0)),
        ],
        out_specs=[pl.BlockSpec((1, tm, MLA_HEAD_PAD), lambda b, i: (b, i, 0))] * 2,
        out_shape=[out, out],
        compiler_params=_cparams(2),
        name="rope_tables",
    )(positions.reshape(bsz, s, 1), jnp.asarray(_rope_freq_lanes()))


def _ffn_kernel(sub, h_ref, ada_ref, w1_ref, w3_ref, w2_ref, g_ref, b_ref, o_ref):
    h = h_ref[0]
    shift = ada_ref[0, 3 * sub + 0:3 * sub + 1, :]
    scale = ada_ref[0, 3 * sub + 1:3 * sub + 2, :]
    gate = ada_ref[0, 3 * sub + 2:3 * sub + 3, :]
    xm = (h * (1.0 + scale) + shift).astype(_BF16)
    acc = jnp.zeros(h.shape, _F32)
    for j in range(N_FF_CHUNKS):
        a = _dot(xm, w1_ref[j])
        b = _dot(xm, w3_ref[j])
        g = (a * jax.nn.sigmoid(a) * b).astype(_BF16)
        acc = acc + _dot(g, w2_ref[j])
    y = DEEPNORM_ALPHA * h + MACARON_WEIGHT * gate * acc
    o_ref[0] = _layer_norm(y, g_ref[...], b_ref[...])


def _ffn_call(h, ada, sub, w1, w3, w2, ln_g, ln_b):
    bsz, s, d = h.shape
    tm = min(s, 512)
    return pl.pallas_call(
        functools.partial(_ffn_kernel, sub),
        grid=(bsz, s // tm),
        in_specs=[
            pl.BlockSpec((1, tm, d), lambda b, i: (b, i, 0)),
            pl.BlockSpec((1, 9, d), lambda b, i: (b, 0, 0)),
            _resident((N_FF_CHUNKS, d, FF_CHUNK), lambda b, i: (0, 0, 0)),
            _resident((N_FF_CHUNKS, d, FF_CHUNK), lambda b, i: (0, 0, 0)),
            _resident((N_FF_CHUNKS, FF_CHUNK, d), lambda b, i: (0, 0, 0)),
            pl.BlockSpec((1, d), lambda b, i: (0, 0)),
            pl.BlockSpec((1, d), lambda b, i: (0, 0)),
        ],
        out_specs=pl.BlockSpec((1, tm, d), lambda b, i: (b, i, 0)),
        out_shape=jax.ShapeDtypeStruct((bsz, s, d), _F32),
        compiler_params=_cparams(2),
        name="ffn_sublayer",
    )(h, ada, w1, w3, w2, ln_g, ln_b)


def _proj_kernel(n_out, x_ref, ada_ref, w_ref, b_ref, o_ref):
    x = x_ref[0]
    shift = ada_ref[0, 3:4, :]
    scale = ada_ref[0, 4:5, :]
    xm = (x * (1.0 + scale) + shift).astype(_BF16)
    for n0 in range(0, n_out, PROJ_CHUNK):
        n1 = min(n0 + PROJ_CHUNK, n_out)
        o_ref[0, 0, :, n0:n1] = (_dot(xm, w_ref[:, n0:n1]) + b_ref[:, n0:n1]).astype(_BF16)


def _proj_call(h, ada, w, bias, dil):
    bsz, s, d = h.shape
    n_out = w.shape[1]
    length = s // dil
    tm = min(length, 512)
    hv = h.reshape(bsz, length, dil * d)
    return pl.pallas_call(
        functools.partial(_proj_kernel, n_out),
        grid=(bsz, dil, length // tm),
        in_specs=[
            pl.BlockSpec((1, tm, d), lambda b, r, i: (b, i, r)),
            pl.BlockSpec((1, 9, d), lambda b, r, i: (b, 0, 0)),
            _resident((d, n_out), lambda b, r, i: (0, 0)),
            pl.BlockSpec((1, n_out), lambda b, r, i: (0, 0)),
        ],
        out_specs=pl.BlockSpec((1, 1, tm, n_out), lambda b, r, i: (b, r, i, 0)),
        out_shape=jax.ShapeDtypeStruct((bsz, dil, length, n_out), _BF16),
        compiler_params=_cparams(3),
        name=f"in_proj_dil{dil}",
    )(hv, ada, w, bias)


def _mla_prep_kernel(a_ref, cos_ref, sin_ref, qg_ref, kvg_ref, wq_ref, wqr_ref, wk_ref, wv_ref,
                     pk_ref, pkr_ref, q_ref, k_ref, v_ref):
    a = a_ref[0, 0]
    qa = a[:, :MLA_Q_RANK].astype(_F32)
    kva = a[:, MLA_Q_RANK:MLA_Q_RANK + MLA_KV_RANK].astype(_F32)
    kr = a[:, MLA_Q_RANK + MLA_KV_RANK:]
    qn = (qa * lax.rsqrt(jnp.mean(qa * qa, axis=-1, keepdims=True) + RMS_EPS) * qg_ref[...]).astype(_BF16)
    kvn = (kva * lax.rsqrt(jnp.mean(kva * kva, axis=-1, keepdims=True) + RMS_EPS) * kvg_ref[...]).astype(_BF16)
    cos = cos_ref[0]
    sin = sin_ref[0]
    q = _dot(qn, wq_ref[...])
    qrot = _dot(qn, wqr_ref[...])
    k = _dot(kvn, wk_ref[...]) + _dot(kr, pk_ref[...])
    krot = _dot(kr, pkr_ref[...])
    for hd in range(MLA_HEADS):
        sl = slice(hd * MLA_HEAD_PAD, (hd + 1) * MLA_HEAD_PAD)
        q_ref[0, :, sl] = ((q[:, sl] * cos + qrot[:, sl] * sin) * MLA_SCALE).astype(_BF16)
        k_ref[0, :, sl] = (k[:, sl] * cos + krot[:, sl] * sin).astype(_BF16)
    v_ref[0] = _dot(kvn, wv_ref[...]).astype(_BF16)


def _mla_prep_call(proj_main, cos, sin, qg, kvg, wq, wqr, wk, wv, pk, pkr):
    bsz, _, s, _ = proj_main.shape
    tm = min(s, 512)
    const = lambda b, i: (0, 0)
    return pl.pallas_call(
        _mla_prep_kernel,
        grid=(bsz, s // tm),
        in_specs=[
            pl.BlockSpec((1, 1, tm, MLA_A_WIDTH), lambda b, i: (b, 0, i, 0)),
            pl.BlockSpec((1, tm, MLA_HEAD_PAD), lambda b, i: (b, i, 0)),
            pl.BlockSpec((1, tm, MLA_HEAD_PAD), lambda b, i: (b, i, 0)),
            pl.BlockSpec((1, MLA_Q_RANK), const),
            pl.BlockSpec((1, MLA_KV_RANK), const),
            pl.BlockSpec((MLA_Q_RANK, MLA_QK_WIDTH), const),
            pl.BlockSpec((MLA_Q_RANK, MLA_QK_WIDTH), const),
            pl.BlockSpec((MLA_KV_RANK, MLA_QK_WIDTH), const),
            pl.BlockSpec((MLA_KV_RANK, MLA_V_WIDTH), const),
            pl.BlockSpec((LANES, MLA_QK_WIDTH), const),
            pl.BlockSpec((LANES, MLA_QK_WIDTH), const),
        ],
        out_specs=[
            pl.BlockSpec((1, tm, MLA_QK_WIDTH), lambda b, i: (b, i, 0)),
            pl.BlockSpec((1, tm, MLA_QK_WIDTH), lambda b, i: (b, i, 0)),
            pl.BlockSpec((1, tm, MLA_V_WIDTH), lambda b, i: (b, i, 0)),
        ],
        out_shape=[
            jax.ShapeDtypeStruct((bsz, s, MLA_QK_WIDTH), _BF16),
            jax.ShapeDtypeStruct((bsz, s, MLA_QK_WIDTH), _BF16),
            jax.ShapeDtypeStruct((bsz, s, MLA_V_WIDTH), _BF16),
        ],
        compiler_params=_cparams(2),
        name="mla_prep",
    )(proj_main, cos, sin, qg, kvg, wq, wqr, wk, wv, pk, pkr)


def _mla_attn_kernel(tq, q_ref, k_ref, v_ref, o_ref):
    qi = pl.program_id(2)
    row = lax.broadcasted_iota(jnp.int32, (tq, tq), 0)
    col = lax.broadcasted_iota(jnp.int32, (tq, tq), 1)
    causal = col <= row
    outs = []
    for e in range(2):
        lanes = slice(e * MLA_HEAD_PAD, (e + 1) * MLA_HEAD_PAD)
        q = q_ref[0, :, lanes]

        def tile(j, carry, masked):
            m, l, acc = carry
            start = pl.multiple_of(j * tq, tq)
            k = k_ref[0, pl.ds(start, tq), lanes]
            v = v_ref[0, pl.ds(start, tq), :]
            s = _dot_nt(q, k)
            if masked:
                s = jnp.where(causal, s, _NEG)
            m_new = jnp.maximum(m, jnp.max(s, axis=-1, keepdims=True))
            alpha = jnp.exp(m - m_new)
            p = jnp.exp(s - m_new)
            l = alpha * l + jnp.sum(p, axis=-1, keepdims=True)
            acc = alpha * acc + _dot(p.astype(_BF16), v)
            return m_new, l, acc

        init = (jnp.full((tq, 1), _NEG, _F32), jnp.zeros((tq, 1), _F32),
                jnp.zeros((tq, 2 * MLA_V_DIM), _F32))
        carry = lax.fori_loop(0, qi, lambda j, c: tile(j, c, False), init)
        _, l, acc = tile(qi, carry, True)
        outs.append(acc / l)
    lane = lax.broadcasted_iota(jnp.int32, (tq, 2 * MLA_V_DIM), 1)
    o_ref[0] = jnp.where(lane < MLA_V_DIM, outs[0], outs[1]).astype(_BF16)


def _mla_attn_call(q, k, v):
    bsz, s, _ = q.shape
    tq = min(s, 512)
    pair = 2 * MLA_HEAD_PAD
    return pl.pallas_call(
        functools.partial(_mla_attn_kernel, tq),
        grid=(bsz, MLA_HEADS // 2, s // tq),
        in_specs=[
            pl.BlockSpec((1, tq, pair), lambda b, h, i: (b, i, h)),
            pl.BlockSpec((1, s, pair), lambda b, h, i: (b, 0, h)),
            pl.BlockSpec((1, s, 2 * MLA_V_DIM), lambda b, h, i: (b, 0, h)),
        ],
        out_specs=pl.BlockSpec((1, tq, 2 * MLA_V_DIM), lambda b, h, i: (b, i, h)),
        out_shape=jax.ShapeDtypeStruct((bsz, s, MLA_V_WIDTH), _BF16),
        compiler_params=_cparams(3),
        name="mla_attention",
    )(q, k, v)


def _alibi_slope(head):
    return float(np.exp2(np.float32(-8.0) * (np.float32(head) + np.float32(1.0)) / np.float32(DIL_HEADS)))


def _dil_attn_kernel(tq, group, dil, q_ref, k_ref, v_ref, o_ref, lse_ref):
    qi = pl.program_id(2)
    q0 = pl.multiple_of(qi * tq, tq)
    p0 = pl.multiple_of(jnp.maximum(q0 - DIL_BAND, 0), DIL_BAND)
    row_p = lax.broadcasted_iota(jnp.int32, (tq, DIL_BAND), 0)
    col_p = lax.broadcasted_iota(jnp.int32, (tq, DIL_BAND), 1)
    rel_p = DIL_BAND + row_p - col_p
    ok_p = (rel_p >= 0) & (rel_p <= DIL_BAND) & (qi > 0)
    row_c = lax.broadcasted_iota(jnp.int32, (tq, tq), 0)
    col_c = lax.broadcasted_iota(jnp.int32, (tq, tq), 1)
    rel_c = row_c - col_c
    ok_c = (rel_c >= 0) & (rel_c <= DIL_BAND)
    relf_p = rel_p.astype(_F32)
    relf_c = rel_c.astype(_F32)
    lane = lax.broadcasted_iota(jnp.int32, (tq, LANES), 1)
    lse_all = jnp.zeros((tq, LANES), _F32)
    for i in range(DIL_HPG):
        lanes = slice(i * DIL_HEAD_DIM, (i + 1) * DIL_HEAD_DIM)
        slope = _alibi_slope(group * DIL_HPG + i) * dil
        q = q_ref[0, 0, :, lanes]
        kp = k_ref[0, 0, pl.ds(p0, DIL_BAND), lanes]
        kc = k_ref[0, 0, pl.ds(q0, tq), lanes]
        vp = v_ref[0, 0, pl.ds(p0, DIL_BAND), lanes]
        vc = v_ref[0, 0, pl.ds(q0, tq), lanes]
        s_p = jnp.where(ok_p, _dot_nt(q, kp) * DIL_SCALE - slope * relf_p, _NEG)
        s_c = jnp.where(ok_c, _dot_nt(q, kc) * DIL_SCALE - slope * relf_c, _NEG)
        m = jnp.maximum(jnp.max(s_p, axis=-1, keepdims=True), jnp.max(s_c, axis=-1, keepdims=True))
        e_p = jnp.exp(s_p - m)
        e_c = jnp.exp(s_c - m)
        l = jnp.sum(e_p, axis=-1, keepdims=True) + jnp.sum(e_c, axis=-1, keepdims=True)
        o = _dot(e_p.astype(_BF16), vp) + _dot(e_c.astype(_BF16), vc)
        o_ref[0, :, lanes] = (o / l).astype(_BF16)
        lse_all = jnp.where(lane == i, m + jnp.log(l), lse_all)
    lse_ref[0] = lse_all


def _dil_attn_call(arr, col_block0, group, dil, tq):
    bsz, _, length, _ = arr.shape
    w = DIL_GROUP_WIDTH
    tq = min(tq, length)
    o, lse = pl.pallas_call(
        functools.partial(_dil_attn_kernel, tq, group, dil),
        grid=(bsz, dil, length // tq),
        in_specs=[
            pl.BlockSpec((1, 1, tq, w), lambda b, r, i: (b, r, i, col_block0)),
            pl.BlockSpec((1, 1, length, w), lambda b, r, i: (b, r, 0, col_block0 + 1)),
            pl.BlockSpec((1, 1, length, w), lambda b, r, i: (b, r, 0, col_block0 + 2)),
        ],
        out_specs=[
            pl.BlockSpec((1, tq, w), lambda b, r, i: (b, i, r)),
            pl.BlockSpec((1, tq, LANES), lambda b, r, i: (b, i, r)),
        ],
        out_shape=[
            jax.ShapeDtypeStruct((bsz, length, dil * w), _BF16),
            jax.ShapeDtypeStruct((bsz, length, dil * LANES), _F32),
        ],
        compiler_params=_cparams(3),
        name=f"dil_attention_g{group}",
    )(arr, arr, arr)
    return o.reshape(bsz, length * dil, w), lse.reshape(bsz, length * dil, LANES)


def _ssm_kernel(nb, u_ref, bw_ref, cw_ref, are_ref, aim_ref, d_ref, o_ref, bx_ref, st_ref):
    ti = pl.program_id(1)

    @pl.when(ti == 0)
    def _():
        st_ref[...] = jnp.zeros(st_ref.shape, _F32)

    for b in range(nb):
        bu = _dot(u_ref[b, 0], bw_ref[0])
        for sidx in range(2 * SSM_SLABS):
            bx_ref[sidx, b * SSM_PITCH:b * SSM_PITCH + SSM_TT, :] = bu[:, sidx * LANES:(sidx + 1) * LANES]

    a_re = [jnp.broadcast_to(are_ref[0, :, k * LANES:(k + 1) * LANES], (nb, LANES)) for k in range(SSM_SLABS)]
    a_im = [jnp.broadcast_to(aim_ref[0, :, k * LANES:(k + 1) * LANES], (nb, LANES)) for k in range(SSM_SLABS)]

    def step(t, carry):
        xr, xi = carry
        nr, ni = [], []
        for k in range(SSM_SLABS):
            rows = pl.ds(t, nb, stride=SSM_PITCH)
            br = bx_ref[k, rows, :]
            bi = bx_ref[SSM_SLABS + k, rows, :]
            r = a_re[k] * xr[k] - a_im[k] * xi[k] + br
            i = a_re[k] * xi[k] + a_im[k] * xr[k] + bi
            bx_ref[k, rows, :] = r
            bx_ref[SSM_SLABS + k, rows, :] = i
            nr.append(r)
            ni.append(i)
        return tuple(nr), tuple(ni)

    x0 = (tuple(st_ref[k] for k in range(SSM_SLABS)),
          tuple(st_ref[SSM_SLABS + k] for k in range(SSM_SLABS)))
    xr, xi = lax.fori_loop(0, SSM_TT, step, x0)
    for k in range(SSM_SLABS):
        st_ref[k] = xr[k]
        st_ref[SSM_SLABS + k] = xi[k]

    dskip = d_ref[0]
    for b in range(nb):
        xs = jnp.concatenate(
            [bx_ref[sidx, b * SSM_PITCH:b * SSM_PITCH + SSM_TT, :] for sidx in range(2 * SSM_SLABS)],
            axis=1).astype(_BF16)
        y = _dot(xs, cw_ref[0]) + dskip * u_ref[b, 0].astype(_F32)
        o_ref[b] = jax.nn.gelu(y).astype(_BF16)


def _ssm_call(proj_main, bw, cw, a_re, a_im, d_skip):
    bsz, _, s, _ = proj_main.shape
    u_block0 = COL_U // SSM_BLOCK_CH
    nstate = 2 * SSM_BLOCK_STATES
    return pl.pallas_call(
        functools.partial(_ssm_kernel, bsz),
        grid=(SSM_BLOCKS, s // SSM_TT),
        in_specs=[
            pl.BlockSpec((bsz, 1, SSM_TT, SSM_BLOCK_CH), lambda m, t: (0, 0, t, u_block0 + m)),
            pl.BlockSpec((1, SSM_BLOCK_CH, nstate), lambda m, t: (m, 0, 0)),
            pl.BlockSpec((1, nstate, SSM_BLOCK_CH), lambda m, t: (m, 0, 0)),
            pl.BlockSpec((1, 1, SSM_BLOCK_STATES), lambda m, t: (m, 0, 0)),
            pl.BlockSpec((1, 1, SSM_BLOCK_STATES), lambda m, t: (m, 0, 0)),
            pl.BlockSpec((1, 1, SSM_BLOCK_CH), lambda m, t: (m, 0, 0)),
        ],
        out_specs=pl.BlockSpec((bsz, SSM_TT, SSM_BLOCK_CH), lambda m, t: (0, t, m)),
        out_shape=jax.ShapeDtypeStruct((bsz, s, SSM_WIDTH), _BF16),
        scratch_shapes=[
            pltpu.VMEM((2 * SSM_SLABS, bsz * SSM_PITCH, LANES), _F32),
            pltpu.VMEM((2 * SSM_SLABS, bsz, LANES), _F32),
        ],
        compiler_params=_cparams(2),
        name="s5_scan",
    )(proj_main, bw, cw, a_re, a_im, d_skip)


def _ssm_weights(lam_re, lam_im, log_dt, b_re, b_im, c_re, c_im, d_skip):
    lam = lax.complex(lam_re.astype(_F32), lam_im.astype(_F32))
    dt = jnp.exp(log_dt.astype(_F32))[:, None]
    lam_bar = jnp.exp(lam * dt)
    b_bar = ((lam_bar - 1.0) / lam)[..., None] * lax.complex(b_re.astype(_F32), b_im.astype(_F32))
    gpb = SSM_GROUPS // SSM_BLOCKS
    eye = jnp.eye(gpb, dtype=_F32)

    def in_weights(part):
        w = part.reshape(SSM_BLOCKS, gpb, SSM_STATE, SSM_GROUP_SIZE)
        w = jnp.einsum("mgph,gk->mghkp", w, eye)
        return w.reshape(SSM_BLOCKS, SSM_BLOCK_CH, SSM_BLOCK_STATES)

    def out_weights(part):
        w = part.reshape(SSM_BLOCKS, gpb, SSM_GROUP_SIZE, SSM_STATE)
        w = jnp.einsum("mgkp,gj->mgpjk", w, eye)
        return w.reshape(SSM_BLOCKS, SSM_BLOCK_STATES, SSM_BLOCK_CH)

    bw = jnp.concatenate([in_weights(jnp.real(b_bar)), in_weights(jnp.imag(b_bar))], axis=2).astype(_BF16)
    cw = jnp.concatenate([out_weights(c_re.astype(_F32)), -out_weights(c_im.astype(_F32))], axis=1).astype(_BF16)
    a_re = jnp.real(lam_bar).reshape(SSM_BLOCKS, 1, SSM_BLOCK_STATES)
    a_im = jnp.imag(lam_bar).reshape(SSM_BLOCKS, 1, SSM_BLOCK_STATES)
    return bw, cw, a_re, a_im, d_skip.astype(_F32).reshape(SSM_BLOCKS, 1, SSM_BLOCK_CH)


def _merge_kernel(h_ref, ada_ref, ymla_ref, o0_ref, o1_ref, o2_ref, l0_ref, l1_ref, l2_ref, ys_ref,
                  gl_ref, wglu_ref, bglu_ref, wbr_ref, wout_ref, g_ref, b_ref, out_ref):
    h = h_ref[0]
    gate = ada_ref[0, 5:6, :]
    l0, l1, l2 = l0_ref[0], l1_ref[0], l2_ref[0]
    m = jnp.maximum(jnp.maximum(l0, l1), l2)
    e0, e1, e2 = jnp.exp(l0 - m), jnp.exp(l1 - m), jnp.exp(l2 - m)
    inv = 1.0 / (e0 + e1 + e2)
    w0, w1, w2 = e0 * inv, e1 * inv, e2 * inv
    heads = []
    for i in range(DIL_HPG):
        lanes = slice(i * DIL_HEAD_DIM, (i + 1) * DIL_HEAD_DIM)
        heads.append(w0[:, i:i + 1] * o0_ref[0, :, lanes].astype(_F32)
                     + w1[:, i:i + 1] * o1_ref[0, :, lanes].astype(_F32)
                     + w2[:, i:i + 1] * o2_ref[0, :, lanes].astype(_F32))
    y_dil = jnp.concatenate(heads, axis=1).astype(_BF16)
    ys = ys_ref[0]
    glu = jax.nn.sigmoid(_dot(ys, wglu_ref[...]) + bglu_ref[...])
    y_ssm = (ys.astype(_F32) * glu).astype(_BF16)
    d = h.shape[-1]
    merged = (jax.nn.sigmoid(gl_ref[0, 0, :, 0:d].astype(_F32)) * _dot(ymla_ref[0], wbr_ref[0])
              + jax.nn.sigmoid(gl_ref[0, 0, :, d:2 * d].astype(_F32)) * _dot(y_dil, wbr_ref[1])
              + jax.nn.sigmoid(gl_ref[0, 0, :, 2 * d:3 * d].astype(_F32)) * _dot(y_ssm, wbr_ref[2]))
    out = _dot(merged.astype(_BF16), wout_ref[...])
    y = DEEPNORM_ALPHA * h + gate * out
    out_ref[0] = _layer_norm(y, g_ref[...], b_ref[...])


def _merge_call(h, ada, y_mla, dil_o, dil_lse, y_s, proj_main, wglu, bglu, wbr, wout, ln_g, ln_b):
    bsz, s, d = h.shape
    tm = min(s, 256)
    tok = lambda b, i: (b, i, 0)
    const2 = lambda b, i: (0, 0)
    bw = BRANCH_WIDTH
    return pl.pallas_call(
        _merge_kernel,
        grid=(bsz, s // tm),
        in_specs=[
            pl.BlockSpec((1, tm, d), tok),
            pl.BlockSpec((1, 9, d), lambda b, i: (b, 0, 0)),
            pl.BlockSpec((1, tm, bw), tok),
            pl.BlockSpec((1, tm, bw), tok),
            pl.BlockSpec((1, tm, bw), tok),
            pl.BlockSpec((1, tm, bw), tok),
            pl.BlockSpec((1, tm, LANES), tok),
            pl.BlockSpec((1, tm, LANES), tok),
            pl.BlockSpec((1, tm, LANES), tok),
            pl.BlockSpec((1, tm, bw), tok),
            pl.BlockSpec((1, 1, tm, N_BRANCH * d), lambda b, i: (b, 0, i, COL_GATES // (N_BRANCH * d))),
            pl.BlockSpec((bw, bw), const2),
            pl.BlockSpec((1, bw), const2),
            pl.BlockSpec((N_BRANCH, bw, d), lambda b, i: (0, 0, 0)),
            pl.BlockSpec((d, d), const2),
            pl.BlockSpec((1, d), const2),
            pl.BlockSpec((1, d), const2),
        ],
        out_specs=pl.BlockSpec((1, tm, d), tok),
        out_shape=jax.ShapeDtypeStruct((bsz, s, d), _F32),
        compiler_params=_cparams(2),
        name="merge_sublayer",
    )(h, ada, y_mla, dil_o[0], dil_o[1], dil_o[2], dil_lse[0], dil_lse[1], dil_lse[2], y_s,
      proj_main, wglu, bglu, wbr, wout, ln_g, ln_b)


def _in_proj_weights(w_in, b_in):
    d = w_in.shape[0]
    o_kr = MLA_Q_RANK + MLA_KV_RANK
    o_dil = o_kr + MLA_ROPE_DIM
    o_u = o_dil + 3 * DIL_HEADS * DIL_HEAD_DIM
    o_g = o_u + SSM_WIDTH

    def dil_cols(arr, part, group):
        start = o_dil + part * DIL_HEADS * DIL_HEAD_DIM + group * DIL_GROUP_WIDTH
        return arr[..., start:start + DIL_GROUP_WIDTH]

    def main(arr):
        pad = jnp.zeros(arr.shape[:-1] + (COL_Q1 - o_dil,), arr.dtype)
        return jnp.concatenate([arr[..., :o_dil], pad, dil_cols(arr, 0, 0), dil_cols(arr, 1, 0),
                                dil_cols(arr, 2, 0), arr[..., o_u:o_g], arr[..., o_g:]], axis=-1)

    def group(arr, g):
        return jnp.concatenate([dil_cols(arr, 0, g), dil_cols(arr, 1, g), dil_cols(arr, 2, g)], axis=-1)

    b2 = b_in.astype(_F32).reshape(1, -1)
    return ((main(w_in).astype(_BF16), main(b2)),
            (group(w_in, 1).astype(_BF16), group(b2, 1)),
            (group(w_in, 2).astype(_BF16), group(b2, 2)))


def _mla_weights(w_qb, w_kvb):
    qd = MLA_NOPE_DIM + MLA_ROPE_DIM
    half = MLA_ROPE_DIM // 2
    wq = w_qb.reshape(MLA_Q_RANK, MLA_HEADS, qd)
    zeros = jnp.zeros((MLA_Q_RANK, MLA_HEADS, MLA_HEAD_PAD - qd), w_qb.dtype)
    zn = jnp.zeros((MLA_Q_RANK, MLA_HEADS, MLA_NOPE_DIM), w_qb.dtype)
    wq_pad = jnp.concatenate([wq, zeros], axis=-1).reshape(MLA_Q_RANK, MLA_QK_WIDTH)
    t1 = wq[..., MLA_NOPE_DIM:MLA_NOPE_DIM + half]
    t2 = wq[..., MLA_NOPE_DIM + half:]
    wq_rot = jnp.concatenate([zn, -t2, t1, zeros], axis=-1).reshape(MLA_Q_RANK, MLA_QK_WIDTH)
    wkv = w_kvb.reshape(MLA_KV_RANK, MLA_HEADS, MLA_NOPE_DIM + MLA_V_DIM)
    zk = jnp.zeros((MLA_KV_RANK, MLA_HEADS, MLA_HEAD_PAD - MLA_NOPE_DIM), w_kvb.dtype)
    wk = jnp.concatenate([wkv[..., :MLA_NOPE_DIM], zk], axis=-1).reshape(MLA_KV_RANK, MLA_QK_WIDTH)
    wv = wkv[..., MLA_NOPE_DIM:].reshape(MLA_KV_RANK, MLA_V_WIDTH)
    return wq_pad.astype(_BF16), wq_rot.astype(_BF16), wk.astype(_BF16), wv.astype(_BF16)


def _rope_key_placement():
    half = MLA_ROPE_DIM // 2
    pk = np.zeros((LANES, MLA_QK_WIDTH), np.float32)
    pkr = np.zeros((LANES, MLA_QK_WIDTH), np.float32)
    for hd in range(MLA_HEADS):
        base = hd * MLA_HEAD_PAD + MLA_NOPE_DIM
        for i in range(MLA_ROPE_DIM):
            pk[i, base + i] = 1.0
        for i in range(half):
            pkr[half + i, base + i] = -1.0
            pkr[i, base + half + i] = 1.0
    return jnp.asarray(pk, _BF16), jnp.asarray(pkr, _BF16)


def _ffn_weights(w1, w3, w2):
    d = w1.shape[0]
    w1c = w1.reshape(d, N_FF_CHUNKS, FF_CHUNK).transpose(1, 0, 2).astype(_BF16)
    w3c = w3.reshape(d, N_FF_CHUNKS, FF_CHUNK).transpose(1, 0, 2).astype(_BF16)
    w2c = w2.reshape(N_FF_CHUNKS, FF_CHUNK, d).astype(_BF16)
    return w1c, w3c, w2c


def kernel(x, c, positions, w_ada, b_ada, ln_g, ln_b, ffn_w1, ffn_w3, ffn_w2, w_in, b_in, mla_q_norm, mla_kv_norm, mla_w_qb, mla_w_kvb, ssm_lambda_re, ssm_lambda_im, ssm_log_dt, ssm_b_re, ssm_b_im, ssm_c_re, ssm_c_im, ssm_d, ssm_w_glu, ssm_b_glu, w_br, w_out):
    bsz, s, d = x.shape
    assert d == D_MODEL and s % (DIL_PAIRS[2][1] * DIL_BAND) == 0, x.shape
    ada_all = _ada_call(c, w_ada, b_ada).reshape(DEPTH, bsz, 9, d)
    cos, sin = _rope_call(positions)
    pk, pkr = _rope_key_placement()
    h = x
    for l in range(DEPTH):
        ada = ada_all[l]
        h = _ffn_call(h, ada, 0, *_ffn_weights(ffn_w1[l, 0], ffn_w3[l, 0], ffn_w2[l, 0]),
                      ln_g[l, 0].reshape(1, d), ln_b[l, 0].reshape(1, d))
        (w_main, b_main), (w_g1, b_g1), (w_g2, b_g2) = _in_proj_weights(w_in[l], b_in[l])
        proj_main = _proj_call(h, ada, w_main, b_main, 1)
        proj_g1 = _proj_call(h, ada, w_g1, b_g1, DIL_PAIRS[1][1])
        proj_g2 = _proj_call(h, ada, w_g2, b_g2, DIL_PAIRS[2][1])
        wq, wqr, wk, wv = _mla_weights(mla_w_qb[l], mla_w_kvb[l])
        q, k, v = _mla_prep_call(proj_main, cos, sin, mla_q_norm[l].reshape(1, -1),
                                 mla_kv_norm[l].reshape(1, -1), wq, wqr, wk, wv, pk, pkr)
        y_mla = _mla_attn_call(q, k, v)
        o0, lse0 = _dil_attn_call(proj_main, COL_Q1 // DIL_GROUP_WIDTH, 0, DIL_PAIRS[0][1], 256)
        o1, lse1 = _dil_attn_call(proj_g1, 0, 1, DIL_PAIRS[1][1], 256)
        o2, lse2 = _dil_attn_call(proj_g2, 0, 2, DIL_PAIRS[2][1], 128)
        y_s = _ssm_call(proj_main, *_ssm_weights(ssm_lambda_re[l], ssm_lambda_im[l], ssm_log_dt[l],
                                                 ssm_b_re[l], ssm_b_im[l], ssm_c_re[l], ssm_c_im[l], ssm_d[l]))
        h = _merge_call(h, ada, y_mla, (o0, o1, o2), (lse0, lse1, lse2), y_s, proj_main,
                        ssm_w_glu[l].astype(_BF16), ssm_b_glu[l].reshape(1, -1), w_br[l].astype(_BF16),
                        w_out[l].astype(_BF16), ln_g[l, 1].reshape(1, d), ln_b[l, 1].reshape(1, d))
        h = _ffn_call(h, ada, 2, *_ffn_weights(ffn_w1[l, 1], ffn_w3[l, 1], ffn_w2[l, 1]),
                      ln_g[l, 2].reshape(1, d), ln_b[l, 2].reshape(1, d))
    return h
```

```python
import functools
import math

import numpy as np
import jax
import jax.numpy as jnp
from jax import lax
from jax.experimental import pallas as pl
from jax.experimental.pallas import tpu as pltpu

D_MODEL = 1024
DEPTH = 4
D_FF = 2816
MLA_HEADS = 8
MLA_Q_RANK = 384
MLA_KV_RANK = 256
MLA_NOPE_DIM = 64
MLA_ROPE_DIM = 32
MLA_V_DIM = 64
ROPE_THETA = 10000.0
DIL_PAIRS = ((128, 1), (512, 4), (2048, 16))
DIL_HPG = 4
DIL_HEADS = DIL_HPG * len(DIL_PAIRS)
DIL_HEAD_DIM = 128
DIL_BAND = 128
SSM_WIDTH = 512
SSM_GROUP_SIZE = 16
SSM_GROUPS = SSM_WIDTH // SSM_GROUP_SIZE
SSM_STATE = 64
N_BRANCH = 3
BRANCH_WIDTH = 512
DEEPNORM_ALPHA = (2 * DEPTH) ** 0.25
MACARON_WEIGHT = 0.5
LN_EPS = 1e-5
RMS_EPS = 1e-6

LANES = 128
V7X_VMEM_LIMIT_BYTES = 56 * 1024 * 1024

FF_CHUNK = 256
N_FF_CHUNKS = D_FF // FF_CHUNK
MLA_HEAD_PAD = 128
MLA_QK_WIDTH = MLA_HEADS * MLA_HEAD_PAD
MLA_V_WIDTH = MLA_HEADS * MLA_V_DIM
MLA_SCALE = (MLA_NOPE_DIM + MLA_ROPE_DIM) ** -0.5
DIL_SCALE = DIL_HEAD_DIM ** -0.5
DIL_GROUP_WIDTH = DIL_HPG * DIL_HEAD_DIM
MLA_A_WIDTH = 768
COL_Q1 = 1024
COL_U = COL_Q1 + 3 * DIL_GROUP_WIDTH
COL_GATES = COL_U + SSM_WIDTH
N_MAIN = COL_GATES + N_BRANCH * D_MODEL
PROJ_CHUNK = 512
PERM_TILE = 256
LOG2E = math.log2(math.e)
SSM_BLOCKS = 4
SSM_BLOCK_CH = SSM_WIDTH // SSM_BLOCKS
SSM_BLOCK_STATES = SSM_GROUPS * SSM_STATE // SSM_BLOCKS
SSM_SLABS = SSM_BLOCK_STATES // LANES
SSM_TT = 128
SSM_PITCH = 136

_F32 = jnp.float32
_BF16 = jnp.bfloat16
_NEG = -1e30


def _cparams(n_axes):
    return pltpu.CompilerParams(
        dimension_semantics=("arbitrary",) * n_axes,
        vmem_limit_bytes=V7X_VMEM_LIMIT_BYTES,
    )


def _resident(block_shape, index_map):
    return pl.BlockSpec(block_shape, index_map, pipeline_mode=pl.Buffered(1))


def _layer_norm(y, g, b):
    mu = jnp.mean(y, axis=-1, keepdims=True)
    yc = y - mu
    var = jnp.mean(yc * yc, axis=-1, keepdims=True)
    return yc * lax.rsqrt(var + LN_EPS) * g + b


def _dot(a, b):
    return jnp.dot(a, b, preferred_element_type=_F32)


def _dot_nt(a, b):
    return lax.dot_general(a, b, (((1,), (1,)), ((), ())), preferred_element_type=_F32)


def _ada_kernel(c_ref, w_ref, b_ref, o_ref):
    c = c_ref[...]
    cond = (c * jax.nn.sigmoid(c)).astype(_BF16)
    o_ref[0] = _dot(cond, w_ref[0].astype(_BF16)) + b_ref[0]


def _ada_call(c, w_ada, b_ada):
    depth, d, n = w_ada.shape
    bsz = c.shape[0]
    tn = 1536
    return pl.pallas_call(
        _ada_kernel,
        grid=(depth, n // tn),
        in_specs=[
            pl.BlockSpec((bsz, d), lambda l, j: (0, 0)),
            pl.BlockSpec((1, d, tn), lambda l, j: (l, 0, j)),
            pl.BlockSpec((1, 1, tn), lambda l, j: (l, 0, j)),
        ],
        out_specs=pl.BlockSpec((1, bsz, tn), lambda l, j: (l, 0, j)),
        out_shape=jax.ShapeDtypeStruct((depth, bsz, n), _F32),
        compiler_params=_cparams(2),
        name="ada",
    )(c, w_ada, b_ada.reshape(depth, 1, n))


def _rope_freq_lanes():
    half = MLA_ROPE_DIM // 2
    inv_freq = np.power(np.float32(ROPE_THETA), -np.arange(half, dtype=np.float32) / np.float32(half))
    f = np.zeros((1, MLA_HEAD_PAD), np.float32)
    f[0, MLA_NOPE_DIM:MLA_NOPE_DIM + half] = inv_freq
    f[0, MLA_NOPE_DIM + half:MLA_NOPE_DIM + 2 * half] = inv_freq
    return f


def _rope_kernel(pos_ref, f_ref, cos_ref, sin_ref):
    ang = pos_ref[0].astype(_F32) * f_ref[...]
    cos_ref[0] = jnp.cos(ang)
    sin_ref[0] = jnp.sin(ang)


def _rope_call(positions):
    bsz, s = positions.shape
    tm = min(s, 512)
    out = jax.ShapeDtypeStruct((bsz, s, MLA_HEAD_PAD), _F32)
    return pl.pallas_call(
        _rope_kernel,
        grid=(bsz, s // tm),
        in_specs=[
            pl.BlockSpec((1, tm, 1), lambda b, i: (b, i, 0)),
            pl.BlockSpec((1, MLA_HEAD_PAD), lambda b, i: (0, 0)),
        ],
        out_specs=[pl.BlockSpec((1, tm, MLA_HEAD_PAD), lambda b, i: (b, i, 0))] * 2,
        out_shape=[out, out],
        compiler_params=_cparams(2),
        name="rope_tables",
    )(positions.reshape(bsz, s, 1), jnp.asarray(_rope_freq_lanes()))


def _ffn_kernel(sub, h_ref, ada_ref, w1_ref, w3_ref, w2_ref, g_ref, b_ref, o_ref):
    h = h_ref[0]
    shift = ada_ref[0, 3 * sub + 0:3 * sub + 1, :]
    scale = ada_ref[0, 3 * sub + 1:3 * sub + 2, :]
    gate = ada_ref[0, 3 * sub + 2:3 * sub + 3, :]
    xm = (h * (1.0 + scale) + shift).astype(_BF16)
    acc = jnp.zeros(h.shape, _F32)
    for j in range(N_FF_CHUNKS):
        a = _dot(xm, w1_ref[j])
        b = _dot(xm, w3_ref[j])
        g = (a * jax.nn.sigmoid(a) * b).astype(_BF16)
        acc = acc + _dot(g, w2_ref[j])
    y = DEEPNORM_ALPHA * h + MACARON_WEIGHT * gate * acc
    o_ref[0] = _layer_norm(y, g_ref[...], b_ref[...])


def _ffn_call(h, ada, sub, w1, w3, w2, ln_g, ln_b):
    bsz, s, d = h.shape
    tm = min(s, 512)
    return pl.pallas_call(
        functools.partial(_ffn_kernel, sub),
        grid=(bsz, s // tm),
        in_specs=[
            pl.BlockSpec((1, tm, d), lambda b, i: (b, i, 0)),
            pl.BlockSpec((1, 9, d), lambda b, i: (b, 0, 0)),
            _resident((N_FF_CHUNKS, d, FF_CHUNK), lambda b, i: (0, 0, 0)),
            _resident((N_FF_CHUNKS, d, FF_CHUNK), lambda b, i: (0, 0, 0)),
            _resident((N_FF_CHUNKS, FF_CHUNK, d), lambda b, i: (0, 0, 0)),
            pl.BlockSpec((1, d), lambda b, i: (0, 0)),
            pl.BlockSpec((1, d), lambda b, i: (0, 0)),
        ],
        out_specs=pl.BlockSpec((1, tm, d), lambda b, i: (b, i, 0)),
        out_shape=jax.ShapeDtypeStruct((bsz, s, d), _F32),
        compiler_params=_cparams(2),
        name="ffn_sublayer",
    )(h, ada, w1, w3, w2, ln_g, ln_b)


def _perm_matrix(dil):
    n = PERM_TILE // dil
    p = np.zeros((PERM_TILE, PERM_TILE), np.float32)
    for r in range(dil):
        for i in range(n):
            p[r * n + i, i * dil + r] = 1.0
    return p


def _proj_kernel(tm, dils, x_ref, ada_ref, wm_ref, bm_ref, *rest):
    ngrp = len(dils)
    grp_in = rest[:3 * ngrp]
    om_ref = rest[3 * ngrp]
    grp_out = rest[3 * ngrp + 1:]
    x = x_ref[0]
    shift = ada_ref[0, 3:4, :]
    scale = ada_ref[0, 4:5, :]
    xm = (x * (1.0 + scale) + shift).astype(_BF16)
    n_main = wm_ref.shape[1]
    for n0 in range(0, n_main, PROJ_CHUNK):
        n1 = min(n0 + PROJ_CHUNK, n_main)
        om_ref[0, :, n0:n1] = (_dot(xm, wm_ref[:, n0:n1]) + bm_ref[:, n0:n1]).astype(_BF16)
    for g, dil in enumerate(dils):
        w_ref, b_ref, p_ref = grp_in[3 * g:3 * g + 3]
        o_ref = grp_out[g]
        n = PERM_TILE // dil
        n_out = w_ref.shape[1]
        for t in range(tm // PERM_TILE):
            xp = _dot(p_ref[...], xm[t * PERM_TILE:(t + 1) * PERM_TILE]).astype(_BF16)
            for n0 in range(0, n_out, PROJ_CHUNK):
                n1 = min(n0 + PROJ_CHUNK, n_out)
                res = (_dot(xp, w_ref[:, n0:n1]) + b_ref[:, n0:n1]).astype(_BF16)
                for r in range(dil):
                    o_ref[0, r, t * n:(t + 1) * n, n0:n1] = res[r * n:(r + 1) * n]


def _proj_call(h, ada, w_main, b_main, groups):
    bsz, s, d = h.shape
    tm = min(s, 512)
    dils = tuple(g[0] for g in groups)
    const = lambda b, i: (0, 0)
    in_specs = [
        pl.BlockSpec((1, tm, d), lambda b, i: (b, i, 0)),
        pl.BlockSpec((1, 9, d), lambda b, i: (b, 0, 0)),
        _resident(w_main.shape, const),
        pl.BlockSpec(b_main.shape, const),
    ]
    args = [h, ada, w_main, b_main]
    out_specs = [pl.BlockSpec((1, tm, w_main.shape[1]), lambda b, i: (b, i, 0))]
    out_shape = [jax.ShapeDtypeStruct((bsz, s, w_main.shape[1]), _BF16)]
    for dil, w, bias in groups:
        in_specs += [_resident(w.shape, const), pl.BlockSpec(bias.shape, const),
                     pl.BlockSpec((PERM_TILE, PERM_TILE), const)]
        args += [w, bias, jnp.asarray(_perm_matrix(dil), _BF16)]
        out_specs.append(pl.BlockSpec((1, dil, tm // dil, w.shape[1]), lambda b, i: (b, 0, i, 0)))
        out_shape.append(jax.ShapeDtypeStruct((bsz, dil, s // dil, w.shape[1]), _BF16))
    return pl.pallas_call(
        functools.partial(_proj_kernel, tm, dils),
        grid=(bsz, s // tm),
        in_specs=in_specs,
        out_specs=out_specs,
        out_shape=out_shape,
        compiler_params=_cparams(2),
        name="in_proj",
    )(*args)


def _mla_prep_kernel(a_ref, cos_ref, sin_ref, qg_ref, kvg_ref, wq_ref, wqr_ref, wk_ref, wv_ref,
                     pk_ref, pkr_ref, ones_ref, q_ref, k_ref, v_ref):
    a = a_ref[0]
    qa = a[:, :MLA_Q_RANK].astype(_F32)
    kva = a[:, MLA_Q_RANK:MLA_Q_RANK + MLA_KV_RANK].astype(_F32)
    kr = a[:, MLA_Q_RANK + MLA_KV_RANK:]
    qn = (qa * lax.rsqrt(jnp.mean(qa * qa, axis=-1, keepdims=True) + RMS_EPS) * qg_ref[...]).astype(_BF16)
    kvn = (kva * lax.rsqrt(jnp.mean(kva * kva, axis=-1, keepdims=True) + RMS_EPS) * kvg_ref[...]).astype(_BF16)
    cos = cos_ref[0]
    sin = sin_ref[0]
    q = _dot(qn, wq_ref[...])
    qrot = _dot(qn, wqr_ref[...])
    k = _dot(kvn, wk_ref[...]) + _dot(kr, pk_ref[...])
    krot = _dot(kr, pkr_ref[...])
    for hd in range(MLA_HEADS):
        sl = slice(hd * MLA_HEAD_PAD, (hd + 1) * MLA_HEAD_PAD)
        q_ref[0, :, sl] = ((q[:, sl] * cos + qrot[:, sl] * sin) * (MLA_SCALE * LOG2E)).astype(_BF16)
        k_ref[0, :, sl] = (k[:, sl] * cos + krot[:, sl] * sin).astype(_BF16)
    v_ref[0] = (_dot(kvn, wv_ref[...]) + ones_ref[...]).astype(_BF16)


def _mla_prep_call(proj_main, cos, sin, qg, kvg, wq, wqr, wk, wv, pk, pkr, ones_row):
    bsz, s, _ = proj_main.shape
    tm = min(s, 512)
    const = lambda b, i: (0, 0)
    return pl.pallas_call(
        _mla_prep_kernel,
        grid=(bsz, s // tm),
        in_specs=[
            pl.BlockSpec((1, tm, MLA_A_WIDTH), lambda b, i: (b, i, 0)),
            pl.BlockSpec((1, tm, MLA_HEAD_PAD), lambda b, i: (b, i, 0)),
            pl.BlockSpec((1, tm, MLA_HEAD_PAD), lambda b, i: (b, i, 0)),
            pl.BlockSpec((1, MLA_Q_RANK), const),
            pl.BlockSpec((1, MLA_KV_RANK), const),
            pl.BlockSpec((MLA_Q_RANK, MLA_QK_WIDTH), const),
            pl.BlockSpec((MLA_Q_RANK, MLA_QK_WIDTH), const),
            pl.BlockSpec((MLA_KV_RANK, MLA_QK_WIDTH), const),
            pl.BlockSpec((MLA_KV_RANK, MLA_QK_WIDTH), const),
            pl.BlockSpec((LANES, MLA_QK_WIDTH), const),
            pl.BlockSpec((LANES, MLA_QK_WIDTH), const),
            pl.BlockSpec((1, MLA_QK_WIDTH), const),
        ],
        out_specs=[pl.BlockSpec((1, tm, MLA_QK_WIDTH), lambda b, i: (b, i, 0))] * 3,
        out_shape=[jax.ShapeDtypeStruct((bsz, s, MLA_QK_WIDTH), _BF16)] * 3,
        compiler_params=_cparams(2),
        name="mla_prep",
    )(proj_main, cos, sin, qg, kvg, wq, wqr, wk, wv, pk, pkr, ones_row)


def _mla_attn_kernel(tq, q_ref, k_ref, v_ref, o_ref):
    qi = pl.program_id(2)
    row = lax.broadcasted_iota(jnp.int32, (tq, tq), 0)
    col = lax.broadcasted_iota(jnp.int32, (tq, tq), 1)
    causal = col <= row
    lanes = [slice(e * MLA_HEAD_PAD, (e + 1) * MLA_HEAD_PAD) for e in range(2)]
    qs = [q_ref[0, :, lanes[e]] for e in range(2)]

    def tile(j, carry, masked):
        start = pl.multiple_of(j * tq, tq)
        new = []
        for e in range(2):
            m, acc = carry[e]
            k = k_ref[0, pl.ds(start, tq), lanes[e]]
            v = v_ref[0, pl.ds(start, tq), lanes[e]]
            s = _dot_nt(qs[e], k)
            if masked:
                s = jnp.where(causal, s, _NEG)
            m_new = jnp.maximum(m, jnp.max(s, axis=-1, keepdims=True))
            p = jnp.exp2(s - m_new)
            acc = jnp.exp2(m - m_new) * acc + _dot(p.astype(_BF16), v)
            new.append((m_new, acc))
        return tuple(new)

    init = tuple((jnp.full((tq, 1), _NEG, _F32), jnp.zeros((tq, MLA_HEAD_PAD), _F32)) for _ in range(2))
    carry = lax.fori_loop(0, qi, lambda j, c: tile(j, c, False), init)
    (_, acc0), (_, acc1) = tile(qi, carry, True)
    out0 = acc0 * (1.0 / acc0[:, MLA_V_DIM:MLA_V_DIM + 1])
    out1 = acc1 * (1.0 / acc1[:, 0:1])
    lane = lax.broadcasted_iota(jnp.int32, (tq, MLA_HEAD_PAD), 1)
    o_ref[0] = jnp.where(lane < MLA_V_DIM, out0, out1).astype(_BF16)


def _mla_attn_call(q, k, v):
    bsz, s, _ = q.shape
    tq = min(s, 512)
    pair = 2 * MLA_HEAD_PAD
    return pl.pallas_call(
        functools.partial(_mla_attn_kernel, tq),
        grid=(bsz, MLA_HEADS // 2, s // tq),
        in_specs=[
            pl.BlockSpec((1, tq, pair), lambda b, h, i: (b, i, h)),
            pl.BlockSpec((1, s, pair), lambda b, h, i: (b, 0, h)),
            pl.BlockSpec((1, s, pair), lambda b, h, i: (b, 0, h)),
        ],
        out_specs=pl.BlockSpec((1, tq, 2 * MLA_V_DIM), lambda b, h, i: (b, i, h)),
        out_shape=jax.ShapeDtypeStruct((bsz, s, MLA_V_WIDTH), _BF16),
        compiler_params=_cparams(3),
        name="mla_attention",
    )(q, k, v)


def _alibi_slope(head):
    return float(np.exp2(np.float32(-8.0) * (np.float32(head) + np.float32(1.0)) / np.float32(DIL_HEADS)))


def _dil_attn_kernel(tq, group, dil, q_ref, k_ref, v_ref, o_ref, lse_ref):
    qi = pl.program_id(2)
    q0 = pl.multiple_of(qi * tq, tq)
    p0 = pl.multiple_of(jnp.maximum(q0 - DIL_BAND, 0), DIL_BAND)
    row_p = lax.broadcasted_iota(jnp.int32, (tq, DIL_BAND), 0)
    col_p = lax.broadcasted_iota(jnp.int32, (tq, DIL_BAND), 1)
    rel_p = DIL_BAND + row_p - col_p
    ok_p = (rel_p >= 0) & (rel_p <= DIL_BAND) & (qi > 0)
    row_c = lax.broadcasted_iota(jnp.int32, (tq, tq), 0)
    col_c = lax.broadcasted_iota(jnp.int32, (tq, tq), 1)
    rel_c = row_c - col_c
    ok_c = (rel_c >= 0) & (rel_c <= DIL_BAND)
    relf_p = rel_p.astype(_F32)
    relf_c = rel_c.astype(_F32)
    lane = lax.broadcasted_iota(jnp.int32, (tq, LANES), 1)
    lse_all = jnp.zeros((tq, LANES), _F32)
    for i in range(DIL_HPG):
        lanes = slice(i * DIL_HEAD_DIM, (i + 1) * DIL_HEAD_DIM)
        slope = _alibi_slope(group * DIL_HPG + i) * dil
        q = q_ref[0, 0, :, lanes]
        kp = k_ref[0, 0, pl.ds(p0, DIL_BAND), lanes]
        kc = k_ref[0, 0, pl.ds(q0, tq), lanes]
        vp = v_ref[0, 0, pl.ds(p0, DIL_BAND), lanes]
        vc = v_ref[0, 0, pl.ds(q0, tq), lanes]
        s_p = jnp.where(ok_p, _dot_nt(q, kp) * DIL_SCALE - slope * relf_p, _NEG)
        s_c = jnp.where(ok_c, _dot_nt(q, kc) * DIL_SCALE - slope * relf_c, _NEG)
        m = jnp.maximum(jnp.max(s_p, axis=-1, keepdims=True), jnp.max(s_c, axis=-1, keepdims=True))
        e_p = jnp.exp(s_p - m)
        e_c = jnp.exp(s_c - m)
        l = jnp.sum(e_p, axis=-1, keepdims=True) + jnp.sum(e_c, axis=-1, keepdims=True)
        o = _dot(e_p.astype(_BF16), vp) + _dot(e_c.astype(_BF16), vc)
        o_ref[0, 0, :, lanes] = (o / l).astype(_BF16)
        lse_all = jnp.where(lane == i, m + jnp.log(l), lse_all)
    lse_ref[0, 0] = lse_all


def _dil_attn_call(arr, col_block0, group, dil, tq):
    bsz, _, length, _ = arr.shape
    w = DIL_GROUP_WIDTH
    tq = min(tq, length)
    return pl.pallas_call(
        functools.partial(_dil_attn_kernel, tq, group, dil),
        grid=(bsz, dil, length // tq),
        in_specs=[
            pl.BlockSpec((1, 1, tq, w), lambda b, r, i: (b, r, i, col_block0)),
            pl.BlockSpec((1, 1, length, w), lambda b, r, i: (b, r, 0, col_block0 + 1)),
            pl.BlockSpec((1, 1, length, w), lambda b, r, i: (b, r, 0, col_block0 + 2)),
        ],
        out_specs=[
            pl.BlockSpec((1, 1, tq, w), lambda b, r, i: (b, r, i, 0)),
            pl.BlockSpec((1, 1, tq, LANES), lambda b, r, i: (b, r, i, 0)),
        ],
        out_shape=[
            jax.ShapeDtypeStruct((bsz, dil, length, w), _BF16),
            jax.ShapeDtypeStruct((bsz, dil, length, LANES), _F32),
        ],
        compiler_params=_cparams(3),
        name=f"dil_attention_g{group}",
    )(arr, arr, arr)


def _ssm_kernel(nb, u_ref, bw_ref, cw_ref, are_ref, aim_ref, d_ref, o_ref, bx_ref, st_ref):
    ti = pl.program_id(1)

    @pl.when(ti == 0)
    def _():
        st_ref[...] = jnp.zeros(st_ref.shape, _F32)

    for b in range(nb):
        bu = _dot(u_ref[b, 0], bw_ref[0])
        for sidx in range(2 * SSM_SLABS):
            bx_ref[sidx, b * SSM_PITCH:b * SSM_PITCH + SSM_TT, :] = bu[:, sidx * LANES:(sidx + 1) * LANES]

    a_re = [jnp.broadcast_to(are_ref[0, :, k * LANES:(k + 1) * LANES], (nb, LANES)) for k in range(SSM_SLABS)]
    a_im = [jnp.broadcast_to(aim_ref[0, :, k * LANES:(k + 1) * LANES], (nb, LANES)) for k in range(SSM_SLABS)]

    def step(t, carry):
        xr, xi = carry
        nr, ni = [], []
        for k in range(SSM_SLABS):
            rows = pl.ds(t, nb, stride=SSM_PITCH)
            br = bx_ref[k, rows, :]
            bi = bx_ref[SSM_SLABS + k, rows, :]
            r = a_re[k] * xr[k] - a_im[k] * xi[k] + br
            i = a_re[k] * xi[k] + a_im[k] * xr[k] + bi
            bx_ref[k, rows, :] = r
            bx_ref[SSM_SLABS + k, rows, :] = i
            nr.append(r)
            ni.append(i)
        return tuple(nr), tuple(ni)

    x0 = (tuple(st_ref[k] for k in range(SSM_SLABS)),
          tuple(st_ref[SSM_SLABS + k] for k in range(SSM_SLABS)))
    xr, xi = lax.fori_loop(0, SSM_TT, step, x0)
    for k in range(SSM_SLABS):
        st_ref[k] = xr[k]
        st_ref[SSM_SLABS + k] = xi[k]

    dskip = d_ref[0]
    for b in range(nb):
        xs = jnp.concatenate(
            [bx_ref[sidx, b * SSM_PITCH:b * SSM_PITCH + SSM_TT, :] for sidx in range(2 * SSM_SLABS)],
            axis=1).astype(_BF16)
        y = _dot(xs, cw_ref[0]) + dskip * u_ref[b, 0].astype(_F32)
        o_ref[b] = jax.nn.gelu(y).astype(_BF16)


def _ssm_call(proj_main, bw, cw, a_re, a_im, d_skip):
    bsz, _, s, _ = proj_main.shape
    u_block0 = COL_U // SSM_BLOCK_CH
    nstate = 2 * SSM_BLOCK_STATES
    return pl.pallas_call(
        functools.partial(_ssm_kernel, bsz),
        grid=(SSM_BLOCKS, s // SSM_TT),
        in_specs=[
            pl.BlockSpec((bsz, 1, SSM_TT, SSM_BLOCK_CH), lambda m, t: (0, 0, t, u_block0 + m)),
            pl.BlockSpec((1, SSM_BLOCK_CH, nstate), lambda m, t: (m, 0, 0)),
            pl.BlockSpec((1, nstate, SSM_BLOCK_CH), lambda m, t: (m, 0, 0)),
            pl.BlockSpec((1, 1, SSM_BLOCK_STATES), lambda m, t: (m, 0, 0)),
            pl.BlockSpec((1, 1, SSM_BLOCK_STATES), lambda m, t: (m, 0, 0)),
            pl.BlockSpec((1, 1, SSM_BLOCK_CH), lambda m, t: (m, 0, 0)),
        ],
        out_specs=pl.BlockSpec((bsz, SSM_TT, SSM_BLOCK_CH), lambda m, t: (0, t, m)),
        out_shape=jax.ShapeDtypeStruct((bsz, s, SSM_WIDTH), _BF16),
        scratch_shapes=[
            pltpu.VMEM((2 * SSM_SLABS, bsz * SSM_PITCH, LANES), _F32),
            pltpu.VMEM((2 * SSM_SLABS, bsz, LANES), _F32),
        ],
        compiler_params=_cparams(2),
        name="s5_scan",
    )(proj_main, bw, cw, a_re, a_im, d_skip)


def _ssm_weights(lam_re, lam_im, log_dt, b_re, b_im, c_re, c_im, d_skip):
    lam = lax.complex(lam_re.astype(_F32), lam_im.astype(_F32))
    dt = jnp.exp(log_dt.astype(_F32))[:, None]
    lam_bar = jnp.exp(lam * dt)
    b_bar = ((lam_bar - 1.0) / lam)[..., None] * lax.complex(b_re.astype(_F32), b_im.astype(_F32))
    gpb = SSM_GROUPS // SSM_BLOCKS
    eye = jnp.eye(gpb, dtype=_F32)

    def in_weights(part):
        w = part.reshape(SSM_BLOCKS, gpb, SSM_STATE, SSM_GROUP_SIZE)
        w = jnp.einsum("mgph,gk->mghkp", w, eye)
        return w.reshape(SSM_BLOCKS, SSM_BLOCK_CH, SSM_BLOCK_STATES)

    def out_weights(part):
        w = part.reshape(SSM_BLOCKS, gpb, SSM_GROUP_SIZE, SSM_STATE)
        w = jnp.einsum("mgkp,gj->mgpjk", w, eye)
        return w.reshape(SSM_BLOCKS, SSM_BLOCK_STATES, SSM_BLOCK_CH)

    bw = jnp.concatenate([in_weights(jnp.real(b_bar)), in_weights(jnp.imag(b_bar))], axis=2).astype(_BF16)
    cw = jnp.concatenate([out_weights(c_re.astype(_F32)), -out_weights(c_im.astype(_F32))], axis=1).astype(_BF16)
    a_re = jnp.real(lam_bar).reshape(SSM_BLOCKS, 1, SSM_BLOCK_STATES)
    a_im = jnp.imag(lam_bar).reshape(SSM_BLOCKS, 1, SSM_BLOCK_STATES)
    return bw, cw, a_re, a_im, d_skip.astype(_F32).reshape(SSM_BLOCKS, 1, SSM_BLOCK_CH)


def _to_token_order(ref, pt_ref):
    _, dil, n, width = ref.shape
    x = ref[0].reshape(dil * n, width)
    if x.dtype == _BF16:
        return _dot(pt_ref[...], x)
    hi = x.astype(_BF16)
    r1 = x - hi.astype(_F32)
    mid = r1.astype(_BF16)
    lo = (r1 - mid.astype(_F32)).astype(_BF16)
    res = _dot(pt_ref[...], jnp.concatenate([hi, mid, lo], axis=1))
    return (res[:, :width] + res[:, width:2 * width]) + res[:, 2 * width:]


def _merge_kernel(h_ref, ada_ref, ymla_ref, o0_ref, o1_ref, o2_ref, l0_ref, l1_ref, l2_ref, pt1_ref, pt2_ref,
                  ys_ref, gl_ref, wglu_ref, bglu_ref, wbr_ref, wout_ref, g_ref, b_ref, out_ref):
    h = h_ref[0]
    gate = ada_ref[0, 5:6, :]
    o0 = o0_ref[0, 0].astype(_F32)
    o1 = _to_token_order(o1_ref, pt1_ref)
    o2 = _to_token_order(o2_ref, pt2_ref)
    l0 = l0_ref[0, 0]
    l1 = _to_token_order(l1_ref, pt1_ref)
    l2 = _to_token_order(l2_ref, pt2_ref)
    m = jnp.maximum(jnp.maximum(l0, l1), l2)
    e0, e1, e2 = jnp.exp(l0 - m), jnp.exp(l1 - m), jnp.exp(l2 - m)
    inv = 1.0 / (e0 + e1 + e2)
    w0, w1, w2 = e0 * inv, e1 * inv, e2 * inv
    heads = []
    for i in range(DIL_HPG):
        lanes = slice(i * DIL_HEAD_DIM, (i + 1) * DIL_HEAD_DIM)
        heads.append(w0[:, i:i + 1] * o0[:, lanes] + w1[:, i:i + 1] * o1[:, lanes] + w2[:, i:i + 1] * o2[:, lanes])
    y_dil = jnp.concatenate(heads, axis=1).astype(_BF16)
    ys = ys_ref[0]
    glu = jax.nn.sigmoid(_dot(ys, wglu_ref[...]) + bglu_ref[...])
    y_ssm = (ys.astype(_F32) * glu).astype(_BF16)
    d = h.shape[-1]
    merged = (jax.nn.sigmoid(gl_ref[0, :, 0:d].astype(_F32)) * _dot(ymla_ref[0], wbr_ref[0])
              + jax.nn.sigmoid(gl_ref[0, :, d:2 * d].astype(_F32)) * _dot(y_dil, wbr_ref[1])
              + jax.nn.sigmoid(gl_ref[0, :, 2 * d:3 * d].astype(_F32)) * _dot(y_ssm, wbr_ref[2]))
    out = _dot(merged.astype(_BF16), wout_ref[...])
    y = DEEPNORM_ALPHA * h + gate * out
    out_ref[0] = _layer_norm(y, g_ref[...], b_ref[...])


def _merge_call(h, ada, y_mla, dil_o, dil_lse, y_s, proj_main, wglu, bglu, wbr, wout, ln_g, ln_b):
    bsz, s, d = h.shape
    tm = PERM_TILE
    tok = lambda b, i: (b, i, 0)
    grp = lambda b, i: (b, 0, i, 0)
    const2 = lambda b, i: (0, 0)
    bw = BRANCH_WIDTH
    d1, d2 = DIL_PAIRS[1][1], DIL_PAIRS[2][1]
    pt1 = jnp.asarray(_perm_matrix(d1).T, _BF16)
    pt2 = jnp.asarray(_perm_matrix(d2).T, _BF16)
    return pl.pallas_call(
        _merge_kernel,
        grid=(bsz, s // tm),
        in_specs=[
            pl.BlockSpec((1, tm, d), tok),
            pl.BlockSpec((1, 9, d), lambda b, i: (b, 0, 0)),
            pl.BlockSpec((1, tm, bw), tok),
            pl.BlockSpec((1, 1, tm, bw), grp),
            pl.BlockSpec((1, d1, tm // d1, bw), grp),
            pl.BlockSpec((1, d2, tm // d2, bw), grp),
            pl.BlockSpec((1, 1, tm, LANES), grp),
            pl.BlockSpec((1, d1, tm // d1, LANES), grp),
            pl.BlockSpec((1, d2, tm // d2, LANES), grp),
            pl.BlockSpec((PERM_TILE, PERM_TILE), const2),
            pl.BlockSpec((PERM_TILE, PERM_TILE), const2),
            pl.BlockSpec((1, tm, bw), tok),
            pl.BlockSpec((1, tm, N_BRANCH * d), lambda b, i: (b, i, COL_GATES // (N_BRANCH * d))),
            pl.BlockSpec((bw, bw), const2),
            pl.BlockSpec((1, bw), const2),
            pl.BlockSpec((N_BRANCH, bw, d), lambda b, i: (0, 0, 0)),
            pl.BlockSpec((d, d), const2),
            pl.BlockSpec((1, d), const2),
            pl.BlockSpec((1, d), const2),
        ],
        out_specs=pl.BlockSpec((1, tm, d), tok),
        out_shape=jax.ShapeDtypeStruct((bsz, s, d), _F32),
        compiler_params=_cparams(2),
        name="merge_sublayer",
    )(h, ada, y_mla, dil_o[0], dil_o[1], dil_o[2], dil_lse[0], dil_lse[1], dil_lse[2], pt1, pt2, y_s,
      proj_main, wglu, bglu, wbr, wout, ln_g, ln_b)


def _in_proj_weights(w_in, b_in):
    d = w_in.shape[0]
    o_kr = MLA_Q_RANK + MLA_KV_RANK
    o_dil = o_kr + MLA_ROPE_DIM
    o_u = o_dil + 3 * DIL_HEADS * DIL_HEAD_DIM
    o_g = o_u + SSM_WIDTH

    def dil_cols(arr, part, group):
        start = o_dil + part * DIL_HEADS * DIL_HEAD_DIM + group * DIL_GROUP_WIDTH
        return arr[..., start:start + DIL_GROUP_WIDTH]

    def main(arr):
        pad = jnp.zeros(arr.shape[:-1] + (COL_Q1 - o_dil,), arr.dtype)
        return jnp.concatenate([arr[..., :o_dil], pad, dil_cols(arr, 0, 0), dil_cols(arr, 1, 0),
                                dil_cols(arr, 2, 0), arr[..., o_u:o_g], arr[..., o_g:]], axis=-1)

    def group(arr, g):
        return jnp.concatenate([dil_cols(arr, 0, g), dil_cols(arr, 1, g), dil_cols(arr, 2, g)], axis=-1)

    b2 = b_in.astype(_F32).reshape(1, -1)
    return ((main(w_in).astype(_BF16), main(b2)),
            (group(w_in, 1).astype(_BF16), group(b2, 1)),
            (group(w_in, 2).astype(_BF16), group(b2, 2)))


def _mla_weights(w_qb, w_kvb):
    qd = MLA_NOPE_DIM + MLA_ROPE_DIM
    half = MLA_ROPE_DIM // 2
    wq = w_qb.reshape(MLA_Q_RANK, MLA_HEADS, qd)
    zeros = jnp.zeros((MLA_Q_RANK, MLA_HEADS, MLA_HEAD_PAD - qd), w_qb.dtype)
    zn = jnp.zeros((MLA_Q_RANK, MLA_HEADS, MLA_NOPE_DIM), w_qb.dtype)
    wq_pad = jnp.concatenate([wq, zeros], axis=-1).reshape(MLA_Q_RANK, MLA_QK_WIDTH)
    t1 = wq[..., MLA_NOPE_DIM:MLA_NOPE_DIM + half]
    t2 = wq[..., MLA_NOPE_DIM + half:]
    wq_rot = jnp.concatenate([zn, -t2, t1, zeros], axis=-1).reshape(MLA_Q_RANK, MLA_QK_WIDTH)
    wkv = w_kvb.reshape(MLA_KV_RANK, MLA_HEADS, MLA_NOPE_DIM + MLA_V_DIM)
    zk = jnp.zeros((MLA_KV_RANK, MLA_HEADS, MLA_HEAD_PAD - MLA_NOPE_DIM), w_kvb.dtype)
    wk = jnp.concatenate([wkv[..., :MLA_NOPE_DIM], zk], axis=-1).reshape(MLA_KV_RANK, MLA_QK_WIDTH)
    wv = wkv[..., MLA_NOPE_DIM:].reshape(MLA_KV_RANK, MLA_HEADS // 2, 2, MLA_V_DIM)
    zv = jnp.zeros_like(wv[:, :, 0])
    wv_pad = jnp.concatenate([wv[:, :, 0], zv, zv, wv[:, :, 1]], axis=-1).reshape(MLA_KV_RANK, MLA_QK_WIDTH)
    return wq_pad.astype(_BF16), wq_rot.astype(_BF16), wk.astype(_BF16), wv_pad.astype(_BF16)


def _mla_ones_row():
    ones = np.zeros((1, MLA_QK_WIDTH), np.float32)
    for hd in range(MLA_HEADS):
        ones[0, hd * MLA_HEAD_PAD + (MLA_V_DIM if hd % 2 == 0 else 0)] = 1.0
    return jnp.asarray(ones)


def _rope_key_placement():
    half = MLA_ROPE_DIM // 2
    pk = np.zeros((LANES, MLA_QK_WIDTH), np.float32)
    pkr = np.zeros((LANES, MLA_QK_WIDTH), np.float32)
    for hd in range(MLA_HEADS):
        base = hd * MLA_HEAD_PAD + MLA_NOPE_DIM
        for i in range(MLA_ROPE_DIM):
            pk[i, base + i] = 1.0
        for i in range(half):
            pkr[half + i, base + i] = -1.0
            pkr[i, base + half + i] = 1.0
    return jnp.asarray(pk, _BF16), jnp.asarray(pkr, _BF16)


def _ffn_weights(w1, w3, w2):
    d = w1.shape[0]
    w1c = w1.reshape(d, N_FF_CHUNKS, FF_CHUNK).transpose(1, 0, 2).astype(_BF16)
    w3c = w3.reshape(d, N_FF_CHUNKS, FF_CHUNK).transpose(1, 0, 2).astype(_BF16)
    w2c = w2.reshape(N_FF_CHUNKS, FF_CHUNK, d).astype(_BF16)
    return w1c, w3c, w2c


def kernel(x, c, positions, w_ada, b_ada, ln_g, ln_b, ffn_w1, ffn_w3, ffn_w2, w_in, b_in, mla_q_norm, mla_kv_norm, mla_w_qb, mla_w_kvb, ssm_lambda_re, ssm_lambda_im, ssm_log_dt, ssm_b_re, ssm_b_im, ssm_c_re, ssm_c_im, ssm_d, ssm_w_glu, ssm_b_glu, w_br, w_out):
    bsz, s, d = x.shape
    assert d == D_MODEL and s % (DIL_PAIRS[2][1] * DIL_BAND) == 0, x.shape
    ada_all = _ada_call(c, w_ada, b_ada).reshape(DEPTH, bsz, 9, d)
    cos, sin = _rope_call(positions)
    pk, pkr = _rope_key_placement()
    ones_row = _mla_ones_row()
    h = x
    for l in range(DEPTH):
        ada = ada_all[l]
        h = _ffn_call(h, ada, 0, *_ffn_weights(ffn_w1[l, 0], ffn_w3[l, 0], ffn_w2[l, 0]),
                      ln_g[l, 0].reshape(1, d), ln_b[l, 0].reshape(1, d))
        (w_main, b_main), (w_g1, b_g1), (w_g2, b_g2) = _in_proj_weights(w_in[l], b_in[l])
        proj_main, proj_g1, proj_g2 = _proj_call(
            h, ada, w_main, b_main, [(DIL_PAIRS[1][1], w_g1, b_g1), (DIL_PAIRS[2][1], w_g2, b_g2)])
        proj_main4 = proj_main.reshape(bsz, 1, s, N_MAIN)
        wq, wqr, wk, wv = _mla_weights(mla_w_qb[l], mla_w_kvb[l])
        q, k, v = _mla_prep_call(proj_main, cos, sin, mla_q_norm[l].reshape(1, -1),
                                 mla_kv_norm[l].reshape(1, -1), wq, wqr, wk, wv, pk, pkr, ones_row)
        y_mla = _mla_attn_call(q, k, v)
        o0, lse0 = _dil_attn_call(proj_main4, COL_Q1 // DIL_GROUP_WIDTH, 0, DIL_PAIRS[0][1], 256)
        o1, lse1 = _dil_attn_call(proj_g1, 0, 1, DIL_PAIRS[1][1], 256)
        o2, lse2 = _dil_attn_call(proj_g2, 0, 2, DIL_PAIRS[2][1], 128)
        y_s = _ssm_call(proj_main4, *_ssm_weights(ssm_lambda_re[l], ssm_lambda_im[l], ssm_log_dt[l],
                                                 ssm_b_re[l], ssm_b_im[l], ssm_c_re[l], ssm_c_im[l], ssm_d[l]))
        h = _merge_call(h, ada, y_mla, (o0, o1, o2), (lse0, lse1, lse2), y_s, proj_main,
                        ssm_w_glu[l].astype(_BF16), ssm_b_glu[l].reshape(1, -1), w_br[l].astype(_BF16),
                        w_out[l].astype(_BF16), ln_g[l, 1].reshape(1, d), ln_b[l, 1].reshape(1, d))
        h = _ffn_call(h, ada, 2, *_ffn_weights(ffn_w1[l, 1], ffn_w3[l, 1], ffn_w2[l, 1]),
                      ln_g[l, 2].reshape(1, d), ln_b[l, 2].reshape(1, d))
    return h
```

```python
import functools
import math

import numpy as np
import jax
import jax.numpy as jnp
from jax import lax
from jax.experimental import pallas as pl
from jax.experimental.pallas import tpu as pltpu

D_MODEL = 1024
DEPTH = 4
D_FF = 2816
MLA_HEADS = 8
MLA_Q_RANK = 384
MLA_KV_RANK = 256
MLA_NOPE_DIM = 64
MLA_ROPE_DIM = 32
MLA_V_DIM = 64
ROPE_THETA = 10000.0
DIL_PAIRS = ((128, 1), (512, 4), (2048, 16))
DIL_HPG = 4
DIL_HEADS = DIL_HPG * len(DIL_PAIRS)
DIL_HEAD_DIM = 128
DIL_BAND = 128
SSM_WIDTH = 512
SSM_GROUP_SIZE = 16
SSM_GROUPS = SSM_WIDTH // SSM_GROUP_SIZE
SSM_STATE = 64
N_BRANCH = 3
BRANCH_WIDTH = 512
DEEPNORM_ALPHA = (2 * DEPTH) ** 0.25
MACARON_WEIGHT = 0.5
LN_EPS = 1e-5
RMS_EPS = 1e-6

LANES = 128
V7X_VMEM_LIMIT_BYTES = 56 * 1024 * 1024

FF_CHUNK = 256
N_FF_CHUNKS = D_FF // FF_CHUNK
MLA_HEAD_PAD = 128
MLA_QK_WIDTH = MLA_HEADS * MLA_HEAD_PAD
MLA_V_WIDTH = MLA_HEADS * MLA_V_DIM
MLA_SCALE = (MLA_NOPE_DIM + MLA_ROPE_DIM) ** -0.5
DIL_SCALE = DIL_HEAD_DIM ** -0.5
DIL_GROUP_WIDTH = DIL_HPG * DIL_HEAD_DIM
MLA_A_WIDTH = 768
COL_Q1 = 1024
COL_U = COL_Q1 + 3 * DIL_GROUP_WIDTH
COL_GATES = COL_U + SSM_WIDTH
N_MAIN = COL_GATES + N_BRANCH * D_MODEL
PROJ_CHUNK = 512
PERM_TILE = 256
LOG2E = math.log2(math.e)
MLA_ROW_CHUNK = 64
SSM_BLOCKS = 4
SSM_BLOCK_CH = SSM_WIDTH // SSM_BLOCKS
SSM_BLOCK_STATES = SSM_GROUPS * SSM_STATE // SSM_BLOCKS
SSM_SLABS = SSM_BLOCK_STATES // LANES
SSM_TT = 128
SSM_PITCH = 136

_F32 = jnp.float32
_BF16 = jnp.bfloat16
_NEG = -1e30


def _cparams(n_axes):
    return pltpu.CompilerParams(
        dimension_semantics=("arbitrary",) * n_axes,
        vmem_limit_bytes=V7X_VMEM_LIMIT_BYTES,
    )


def _resident(block_shape, index_map):
    return pl.BlockSpec(block_shape, index_map, pipeline_mode=pl.Buffered(1))


def _layer_norm(y, g, b):
    mu = jnp.mean(y, axis=-1, keepdims=True)
    yc = y - mu
    var = jnp.mean(yc * yc, axis=-1, keepdims=True)
    return yc * lax.rsqrt(var + LN_EPS) * g + b


def _dot(a, b):
    return jnp.dot(a, b, preferred_element_type=_F32)


def _dot_nt(a, b):
    return lax.dot_general(a, b, (((1,), (1,)), ((), ())), preferred_element_type=_F32)


def _ada_kernel(c_ref, w_ref, b_ref, o_ref):
    c = c_ref[...]
    cond = (c * jax.nn.sigmoid(c)).astype(_BF16)
    o_ref[0] = _dot(cond, w_ref[0].astype(_BF16)) + b_ref[0]


def _ada_call(c, w_ada, b_ada):
    depth, d, n = w_ada.shape
    bsz = c.shape[0]
    tn = 1536
    return pl.pallas_call(
        _ada_kernel,
        grid=(depth, n // tn),
        in_specs=[
            pl.BlockSpec((bsz, d), lambda l, j: (0, 0)),
            pl.BlockSpec((1, d, tn), lambda l, j: (l, 0, j)),
            pl.BlockSpec((1, 1, tn), lambda l, j: (l, 0, j)),
        ],
        out_specs=pl.BlockSpec((1, bsz, tn), lambda l, j: (l, 0, j)),
        out_shape=jax.ShapeDtypeStruct((depth, bsz, n), _F32),
        compiler_params=_cparams(2),
        name="ada",
    )(c, w_ada, b_ada.reshape(depth, 1, n))


def _rope_freq_lanes():
    half = MLA_ROPE_DIM // 2
    inv_freq = np.power(np.float32(ROPE_THETA), -np.arange(half, dtype=np.float32) / np.float32(half))
    f = np.zeros((1, MLA_HEAD_PAD), np.float32)
    f[0, MLA_NOPE_DIM:MLA_NOPE_DIM + half] = inv_freq
    f[0, MLA_NOPE_DIM + half:MLA_NOPE_DIM + 2 * half] = inv_freq
    return f


def _rope_kernel(pos_ref, f_ref, cos_ref, sin_ref):
    ang = pos_ref[0].astype(_F32) * f_ref[...]
    cos_ref[0] = jnp.cos(ang)
    sin_ref[0] = jnp.sin(ang)


def _rope_call(positions):
    bsz, s = positions.shape
    tm = min(s, 512)
    out = jax.ShapeDtypeStruct((bsz, s, MLA_HEAD_PAD), _F32)
    return pl.pallas_call(
        _rope_kernel,
        grid=(bsz, s // tm),
        in_specs=[
            pl.BlockSpec((1, tm, 1), lambda b, i: (b, i, 0)),
            pl.BlockSpec((1, MLA_HEAD_PAD), lambda b, i: (0, 0)),
        ],
        out_specs=[pl.BlockSpec((1, tm, MLA_HEAD_PAD), lambda b, i: (b, i, 0))] * 2,
        out_shape=[out, out],
        compiler_params=_cparams(2),
        name="rope_tables",
    )(positions.reshape(bsz, s, 1), jnp.asarray(_rope_freq_lanes()))


def _ffn_kernel(sub, h_ref, ada_ref, w1_ref, w3_ref, w2_ref, g_ref, b_ref, o_ref):
    h = h_ref[0]
    shift = ada_ref[0, 3 * sub + 0:3 * sub + 1, :]
    scale = ada_ref[0, 3 * sub + 1:3 * sub + 2, :]
    gate = ada_ref[0, 3 * sub + 2:3 * sub + 3, :]
    xm = (h * (1.0 + scale) + shift).astype(_BF16)
    acc = jnp.zeros(h.shape, _F32)
    for j in range(N_FF_CHUNKS):
        a = _dot(xm, w1_ref[j])
        b = _dot(xm, w3_ref[j])
        g = (a * jax.nn.sigmoid(a) * b).astype(_BF16)
        acc = acc + _dot(g, w2_ref[j])
    y = DEEPNORM_ALPHA * h + MACARON_WEIGHT * gate * acc
    o_ref[0] = _layer_norm(y, g_ref[...], b_ref[...])


def _ffn_call(h, ada, sub, w1, w3, w2, ln_g, ln_b):
    bsz, s, d = h.shape
    tm = min(s, 512)
    return pl.pallas_call(
        functools.partial(_ffn_kernel, sub),
        grid=(bsz, s // tm),
        in_specs=[
            pl.BlockSpec((1, tm, d), lambda b, i: (b, i, 0)),
            pl.BlockSpec((1, 9, d), lambda b, i: (b, 0, 0)),
            _resident((N_FF_CHUNKS, d, FF_CHUNK), lambda b, i: (0, 0, 0)),
            _resident((N_FF_CHUNKS, d, FF_CHUNK), lambda b, i: (0, 0, 0)),
            _resident((N_FF_CHUNKS, FF_CHUNK, d), lambda b, i: (0, 0, 0)),
            pl.BlockSpec((1, d), lambda b, i: (0, 0)),
            pl.BlockSpec((1, d), lambda b, i: (0, 0)),
        ],
        out_specs=pl.BlockSpec((1, tm, d), lambda b, i: (b, i, 0)),
        out_shape=jax.ShapeDtypeStruct((bsz, s, d), _F32),
        compiler_params=_cparams(2),
        name="ffn_sublayer",
    )(h, ada, w1, w3, w2, ln_g, ln_b)


def _perm_matrix(dil):
    n = PERM_TILE // dil
    p = np.zeros((PERM_TILE, PERM_TILE), np.float32)
    for r in range(dil):
        for i in range(n):
            p[r * n + i, i * dil + r] = 1.0
    return p


def _proj_kernel(tm, dils, x_ref, ada_ref, wm_ref, bm_ref, *rest):
    ngrp = len(dils)
    grp_in = rest[:3 * ngrp]
    om_ref = rest[3 * ngrp]
    grp_out = rest[3 * ngrp + 1:]
    x = x_ref[0]
    shift = ada_ref[0, 3:4, :]
    scale = ada_ref[0, 4:5, :]
    xm = (x * (1.0 + scale) + shift).astype(_BF16)
    n_main = wm_ref.shape[1]
    for n0 in range(0, n_main, PROJ_CHUNK):
        n1 = min(n0 + PROJ_CHUNK, n_main)
        om_ref[0, :, n0:n1] = (_dot(xm, wm_ref[:, n0:n1]) + bm_ref[:, n0:n1]).astype(_BF16)
    for g, dil in enumerate(dils):
        w_ref, b_ref, p_ref = grp_in[3 * g:3 * g + 3]
        o_ref = grp_out[g]
        n = PERM_TILE // dil
        n_out = w_ref.shape[1]
        for t in range(tm // PERM_TILE):
            xp = _dot(p_ref[...], xm[t * PERM_TILE:(t + 1) * PERM_TILE]).astype(_BF16)
            for n0 in range(0, n_out, PROJ_CHUNK):
                n1 = min(n0 + PROJ_CHUNK, n_out)
                res = (_dot(xp, w_ref[:, n0:n1]) + b_ref[:, n0:n1]).astype(_BF16)
                for r in range(dil):
                    o_ref[0, r, t * n:(t + 1) * n, n0:n1] = res[r * n:(r + 1) * n]


def _proj_call(h, ada, w_main, b_main, groups):
    bsz, s, d = h.shape
    tm = min(s, 512)
    dils = tuple(g[0] for g in groups)
    const = lambda b, i: (0, 0)
    in_specs = [
        pl.BlockSpec((1, tm, d), lambda b, i: (b, i, 0)),
        pl.BlockSpec((1, 9, d), lambda b, i: (b, 0, 0)),
        _resident(w_main.shape, const),
        pl.BlockSpec(b_main.shape, const),
    ]
    args = [h, ada, w_main, b_main]
    out_specs = [pl.BlockSpec((1, tm, w_main.shape[1]), lambda b, i: (b, i, 0))]
    out_shape = [jax.ShapeDtypeStruct((bsz, s, w_main.shape[1]), _BF16)]
    for dil, w, bias in groups:
        in_specs += [_resident(w.shape, const), pl.BlockSpec(bias.shape, const),
                     pl.BlockSpec((PERM_TILE, PERM_TILE), const)]
        args += [w, bias, jnp.asarray(_perm_matrix(dil), _BF16)]
        out_specs.append(pl.BlockSpec((1, dil, tm // dil, w.shape[1]), lambda b, i: (b, 0, i, 0)))
        out_shape.append(jax.ShapeDtypeStruct((bsz, dil, s // dil, w.shape[1]), _BF16))
    return pl.pallas_call(
        functools.partial(_proj_kernel, tm, dils),
        grid=(bsz, s // tm),
        in_specs=in_specs,
        out_specs=out_specs,
        out_shape=out_shape,
        compiler_params=_cparams(2),
        name="in_proj",
    )(*args)


def _mla_prep_kernel(a_ref, cos_ref, sin_ref, qg_ref, kvg_ref, wq_ref, wqr_ref, wk_ref, wv_ref,
                     pk_ref, pkr_ref, ones_ref, q_ref, k_ref, v_ref):
    a = a_ref[0]
    qa = a[:, :MLA_Q_RANK].astype(_F32)
    kva = a[:, MLA_Q_RANK:MLA_Q_RANK + MLA_KV_RANK].astype(_F32)
    kr = a[:, MLA_Q_RANK + MLA_KV_RANK:]
    qn = (qa * lax.rsqrt(jnp.mean(qa * qa, axis=-1, keepdims=True) + RMS_EPS) * qg_ref[...]).astype(_BF16)
    kvn = (kva * lax.rsqrt(jnp.mean(kva * kva, axis=-1, keepdims=True) + RMS_EPS) * kvg_ref[...]).astype(_BF16)
    cos = cos_ref[0]
    sin = sin_ref[0]
    q = _dot(qn, wq_ref[...])
    qrot = _dot(qn, wqr_ref[...])
    k = _dot(kvn, wk_ref[...]) + _dot(kr, pk_ref[...])
    krot = _dot(kr, pkr_ref[...])
    for hd in range(MLA_HEADS):
        sl = slice(hd * MLA_HEAD_PAD, (hd + 1) * MLA_HEAD_PAD)
        q_ref[0, :, sl] = ((q[:, sl] * cos + qrot[:, sl] * sin) * (MLA_SCALE * LOG2E)).astype(_BF16)
        k_ref[0, :, sl] = (k[:, sl] * cos + krot[:, sl] * sin).astype(_BF16)
    v_ref[0] = (_dot(kvn, wv_ref[...]) + ones_ref[...]).astype(_BF16)


def _mla_prep_call(proj_main, cos, sin, qg, kvg, wq, wqr, wk, wv, pk, pkr, ones_row):
    bsz, s, _ = proj_main.shape
    tm = min(s, 512)
    const = lambda b, i: (0, 0)
    return pl.pallas_call(
        _mla_prep_kernel,
        grid=(bsz, s // tm),
        in_specs=[
            pl.BlockSpec((1, tm, MLA_A_WIDTH), lambda b, i: (b, i, 0)),
            pl.BlockSpec((1, tm, MLA_HEAD_PAD), lambda b, i: (b, i, 0)),
            pl.BlockSpec((1, tm, MLA_HEAD_PAD), lambda b, i: (b, i, 0)),
            pl.BlockSpec((1, MLA_Q_RANK), const),
            pl.BlockSpec((1, MLA_KV_RANK), const),
            pl.BlockSpec((MLA_Q_RANK, MLA_QK_WIDTH), const),
            pl.BlockSpec((MLA_Q_RANK, MLA_QK_WIDTH), const),
            pl.BlockSpec((MLA_KV_RANK, MLA_QK_WIDTH), const),
            pl.BlockSpec((MLA_KV_RANK, MLA_QK_WIDTH), const),
            pl.BlockSpec((LANES, MLA_QK_WIDTH), const),
            pl.BlockSpec((LANES, MLA_QK_WIDTH), const),
            pl.BlockSpec((1, MLA_QK_WIDTH), const),
        ],
        out_specs=[pl.BlockSpec((1, tm, MLA_QK_WIDTH), lambda b, i: (b, i, 0))] * 3,
        out_shape=[jax.ShapeDtypeStruct((bsz, s, MLA_QK_WIDTH), _BF16)] * 3,
        compiler_params=_cparams(2),
        name="mla_prep",
    )(proj_main, cos, sin, qg, kvg, wq, wqr, wk, wv, pk, pkr, ones_row)


def _mla_attn_kernel(tq, q_ref, k_ref, v_ref, o_ref, s_ref, p_ref):
    qi = pl.program_id(2)
    row = lax.broadcasted_iota(jnp.int32, (tq, tq), 0)
    col = lax.broadcasted_iota(jnp.int32, (tq, tq), 1)
    causal = col <= row
    lanes = [slice(e * MLA_HEAD_PAD, (e + 1) * MLA_HEAD_PAD) for e in range(2)]
    qs = [q_ref[0, :, lanes[e]] for e in range(2)]

    def scores(j, slot):
        start = pl.multiple_of(j * tq, tq)
        for e in range(2):
            s_ref[slot, e] = _dot_nt(qs[e], k_ref[0, pl.ds(start, tq), lanes[e]])

    def softmax_pv(j, slot, carry, masked):
        start = pl.multiple_of(j * tq, tq)
        new = []
        for e in range(2):
            m, acc = carry[e]
            m_parts = []
            for c in range(tq // MLA_ROW_CHUNK):
                rows = slice(c * MLA_ROW_CHUNK, (c + 1) * MLA_ROW_CHUNK)
                s = s_ref[slot, e, rows, :]
                if masked:
                    s = jnp.where(causal[rows], s, _NEG)
                m_c = jnp.maximum(m[rows], jnp.max(s, axis=-1, keepdims=True))
                p_ref[e, rows, :] = jnp.exp2(s - m_c).astype(_BF16)
                m_parts.append(m_c)
            m_new = jnp.concatenate(m_parts, axis=0)
            v = v_ref[0, pl.ds(start, tq), lanes[e]]
            acc = jnp.exp2(m - m_new) * acc + _dot(p_ref[e], v)
            new.append((m_new, acc))
        return tuple(new)

    def pair(jj, carry):
        t = 2 * jj
        scores(t + 1, 1)
        carry = softmax_pv(t, 0, carry, False)
        scores(t + 2, 0)
        return softmax_pv(t + 1, 1, carry, False)

    def tail_odd(carry):
        scores(qi, 1)
        carry = softmax_pv(qi - 1, 0, carry, False)
        return softmax_pv(qi, 1, carry, True)

    def tail_even(carry):
        return softmax_pv(qi, 0, carry, True)

    init = tuple((jnp.full((tq, 1), _NEG, _F32), jnp.zeros((tq, MLA_HEAD_PAD), _F32)) for _ in range(2))
    scores(0, 0)
    carry = lax.fori_loop(0, qi // 2, pair, init)
    (_, acc0), (_, acc1) = lax.cond(qi % 2 == 1, tail_odd, tail_even, carry)
    out0 = acc0 * (1.0 / acc0[:, MLA_V_DIM:MLA_V_DIM + 1])
    out1 = acc1 * (1.0 / acc1[:, 0:1])
    lane = lax.broadcasted_iota(jnp.int32, (tq, MLA_HEAD_PAD), 1)
    o_ref[0] = jnp.where(lane < MLA_V_DIM, out0, out1).astype(_BF16)


def _mla_attn_call(q, k, v):
    bsz, s, _ = q.shape
    tq = min(s, 512)
    pair = 2 * MLA_HEAD_PAD
    return pl.pallas_call(
        functools.partial(_mla_attn_kernel, tq),
        grid=(bsz, MLA_HEADS // 2, s // tq),
        in_specs=[
            pl.BlockSpec((1, tq, pair), lambda b, h, i: (b, i, h)),
            pl.BlockSpec((1, s, pair), lambda b, h, i: (b, 0, h)),
            pl.BlockSpec((1, s, pair), lambda b, h, i: (b, 0, h)),
        ],
        out_specs=pl.BlockSpec((1, tq, 2 * MLA_V_DIM), lambda b, h, i: (b, i, h)),
        out_shape=jax.ShapeDtypeStruct((bsz, s, MLA_V_WIDTH), _BF16),
        scratch_shapes=[pltpu.VMEM((2, 2, tq, tq), _F32), pltpu.VMEM((2, tq, tq), _BF16)],
        compiler_params=_cparams(3),
        name="mla_attention",
    )(q, k, v)


def _alibi_slope(head):
    return float(np.exp2(np.float32(-8.0) * (np.float32(head) + np.float32(1.0)) / np.float32(DIL_HEADS)))


def _dil_attn_kernel(tq, kw, single_tile, nres, group, dil, q_ref, k_ref, v_ref, o_ref, lse_ref, bias_ref):
    qi = pl.program_id(2)
    first_step = (pl.program_id(0) == 0) & (pl.program_id(1) == 0) & (qi == 0)
    nb = DIL_BAND

    @pl.when(first_step)
    def _():
        row = lax.broadcasted_iota(jnp.int32, (nb, kw), 0)
        col = lax.broadcasted_iota(jnp.int32, (nb, kw), 1)
        for variant, rel in ((0, nb + row - col), (1, row - col)):
            ok = (rel >= 0) & (rel <= DIL_BAND)
            relf = rel.astype(_F32)
            for i in range(DIL_HPG):
                slope = _alibi_slope(group * DIL_HPG + i) * dil * LOG2E
                bias_ref[i, variant] = jnp.where(ok, -slope * relf, _NEG)

    q0 = pl.multiple_of(qi * tq, tq)
    lane = lax.broadcasted_iota(jnp.int32, (nb, LANES), 1)
    for rr in range(nres):
        for sb in range(tq // nb):
            rows = slice(sb * nb, (sb + 1) * nb)
            if sb == 0:
                start = pl.multiple_of(jnp.maximum(q0 - nb, 0), nb)
            else:
                start = pl.multiple_of(q0 + (sb - 1) * nb, nb)
            lse_blk = jnp.zeros((nb, LANES), _F32)
            for i in range(DIL_HPG):
                lanes = slice(i * DIL_HEAD_DIM, (i + 1) * DIL_HEAD_DIM)
                if sb > 0:
                    bias = bias_ref[i, 0]
                elif single_tile:
                    bias = bias_ref[i, 1]
                else:
                    bias = jnp.where(qi == 0, bias_ref[i, 1], bias_ref[i, 0])
                s = _dot_nt(q_ref[0, rr, rows, lanes], k_ref[0, rr, pl.ds(start, kw), lanes]) + bias
                m = jnp.max(s, axis=-1, keepdims=True)
                e = jnp.exp2(s - m)
                l = jnp.sum(e, axis=-1, keepdims=True)
                o = _dot(e.astype(_BF16), v_ref[0, rr, pl.ds(start, kw), lanes])
                o_ref[0, rr, rows, lanes] = (o * (1.0 / l)).astype(_BF16)
                lse_blk = jnp.where(lane == i, (m + jnp.log2(l)) * (1.0 / LOG2E), lse_blk)
            lse_ref[0, rr, rows, :] = lse_blk


def _dil_attn_call(arr, col_block0, group, dil, tq, nres):
    bsz, _, length, _ = arr.shape
    w = DIL_GROUP_WIDTH
    tq = min(tq, length)
    kw = min(2 * DIL_BAND, length)
    nres = min(nres, dil)
    return pl.pallas_call(
        functools.partial(_dil_attn_kernel, tq, kw, length == tq, nres, group, dil),
        grid=(bsz, dil // nres, length // tq),
        in_specs=[
            pl.BlockSpec((1, nres, tq, w), lambda b, r, i: (b, r, i, col_block0)),
            pl.BlockSpec((1, nres, length, w), lambda b, r, i: (b, r, 0, col_block0 + 1)),
            pl.BlockSpec((1, nres, length, w), lambda b, r, i: (b, r, 0, col_block0 + 2)),
        ],
        out_specs=[
            pl.BlockSpec((1, nres, tq, w), lambda b, r, i: (b, r, i, 0)),
            pl.BlockSpec((1, nres, tq, LANES), lambda b, r, i: (b, r, i, 0)),
        ],
        out_shape=[
            jax.ShapeDtypeStruct((bsz, dil, length, w), _BF16),
            jax.ShapeDtypeStruct((bsz, dil, length, LANES), _F32),
        ],
        scratch_shapes=[pltpu.VMEM((DIL_HPG, 2, DIL_BAND, kw), _F32)],
        compiler_params=_cparams(3),
        name=f"dil_attention_g{group}",
    )(arr, arr, arr)


def _ssm_kernel(nb, u_ref, bw_ref, cw_ref, are_ref, aim_ref, d_ref, o_ref, bx_ref, st_ref):
    ti = pl.program_id(1)

    @pl.when(ti == 0)
    def _():
        st_ref[...] = jnp.zeros(st_ref.shape, _F32)

    for b in range(nb):
        bu = _dot(u_ref[b, 0], bw_ref[0])
        for sidx in range(2 * SSM_SLABS):
            bx_ref[sidx, b * SSM_PITCH:b * SSM_PITCH + SSM_TT, :] = bu[:, sidx * LANES:(sidx + 1) * LANES]

    a_re = [jnp.broadcast_to(are_ref[0, :, k * LANES:(k + 1) * LANES], (nb, LANES)) for k in range(SSM_SLABS)]
    a_im = [jnp.broadcast_to(aim_ref[0, :, k * LANES:(k + 1) * LANES], (nb, LANES)) for k in range(SSM_SLABS)]

    def step(t, carry):
        xr, xi = carry
        nr, ni = [], []
        for k in range(SSM_SLABS):
            rows = pl.ds(t, nb, stride=SSM_PITCH)
            br = bx_ref[k, rows, :]
            bi = bx_ref[SSM_SLABS + k, rows, :]
            r = a_re[k] * xr[k] - a_im[k] * xi[k] + br
            i = a_re[k] * xi[k] + a_im[k] * xr[k] + bi
            bx_ref[k, rows, :] = r
            bx_ref[SSM_SLABS + k, rows, :] = i
            nr.append(r)
            ni.append(i)
        return tuple(nr), tuple(ni)

    x0 = (tuple(st_ref[k] for k in range(SSM_SLABS)),
          tuple(st_ref[SSM_SLABS + k] for k in range(SSM_SLABS)))
    xr, xi = lax.fori_loop(0, SSM_TT, step, x0, unroll=8)
    for k in range(SSM_SLABS):
        st_ref[k] = xr[k]
        st_ref[SSM_SLABS + k] = xi[k]

    dskip = d_ref[0]
    for b in range(nb):
        xs = jnp.concatenate(
            [bx_ref[sidx, b * SSM_PITCH:b * SSM_PITCH + SSM_TT, :] for sidx in range(2 * SSM_SLABS)],
            axis=1).astype(_BF16)
        y = _dot(xs, cw_ref[0]) + dskip * u_ref[b, 0].astype(_F32)
        o_ref[b] = jax.nn.gelu(y).astype(_BF16)


def _ssm_call(proj_main, bw, cw, a_re, a_im, d_skip):
    bsz, _, s, _ = proj_main.shape
    u_block0 = COL_U // SSM_BLOCK_CH
    nstate = 2 * SSM_BLOCK_STATES
    return pl.pallas_call(
        functools.partial(_ssm_kernel, bsz),
        grid=(SSM_BLOCKS, s // SSM_TT),
        in_specs=[
            pl.BlockSpec((bsz, 1, SSM_TT, SSM_BLOCK_CH), lambda m, t: (0, 0, t, u_block0 + m)),
            pl.BlockSpec((1, SSM_BLOCK_CH, nstate), lambda m, t: (m, 0, 0)),
            pl.BlockSpec((1, nstate, SSM_BLOCK_CH), lambda m, t: (m, 0, 0)),
            pl.BlockSpec((1, 1, SSM_BLOCK_STATES), lambda m, t: (m, 0, 0)),
            pl.BlockSpec((1, 1, SSM_BLOCK_STATES), lambda m, t: (m, 0, 0)),
            pl.BlockSpec((1, 1, SSM_BLOCK_CH), lambda m, t: (m, 0, 0)),
        ],
        out_specs=pl.BlockSpec((bsz, SSM_TT, SSM_BLOCK_CH), lambda m, t: (0, t, m)),
        out_shape=jax.ShapeDtypeStruct((bsz, s, SSM_WIDTH), _BF16),
        scratch_shapes=[
            pltpu.VMEM((2 * SSM_SLABS, bsz * SSM_PITCH, LANES), _F32),
            pltpu.VMEM((2 * SSM_SLABS, bsz, LANES), _F32),
        ],
        compiler_params=_cparams(2),
        name="s5_scan",
    )(proj_main, bw, cw, a_re, a_im, d_skip)


def _ssm_weights(lam_re, lam_im, log_dt, b_re, b_im, c_re, c_im, d_skip):
    lam = lax.complex(lam_re.astype(_F32), lam_im.astype(_F32))
    dt = jnp.exp(log_dt.astype(_F32))[:, None]
    lam_bar = jnp.exp(lam * dt)
    b_bar = ((lam_bar - 1.0) / lam)[..., None] * lax.complex(b_re.astype(_F32), b_im.astype(_F32))
    gpb = SSM_GROUPS // SSM_BLOCKS
    eye = jnp.eye(gpb, dtype=_F32)

    def in_weights(part):
        w = part.reshape(SSM_BLOCKS, gpb, SSM_STATE, SSM_GROUP_SIZE)
        w = jnp.einsum("mgph,gk->mghkp", w, eye)
        return w.reshape(SSM_BLOCKS, SSM_BLOCK_CH, SSM_BLOCK_STATES)

    def out_weights(part):
        w = part.reshape(SSM_BLOCKS, gpb, SSM_GROUP_SIZE, SSM_STATE)
        w = jnp.einsum("mgkp,gj->mgpjk", w, eye)
        return w.reshape(SSM_BLOCKS, SSM_BLOCK_STATES, SSM_BLOCK_CH)

    bw = jnp.concatenate([in_weights(jnp.real(b_bar)), in_weights(jnp.imag(b_bar))], axis=2).astype(_BF16)
    cw = jnp.concatenate([out_weights(c_re.astype(_F32)), -out_weights(c_im.astype(_F32))], axis=1).astype(_BF16)
    a_re = jnp.real(lam_bar).reshape(SSM_BLOCKS, 1, SSM_BLOCK_STATES)
    a_im = jnp.imag(lam_bar).reshape(SSM_BLOCKS, 1, SSM_BLOCK_STATES)
    return bw, cw, a_re, a_im, d_skip.astype(_F32).reshape(SSM_BLOCKS, 1, SSM_BLOCK_CH)


def _to_token_order(ref, pt_ref):
    _, dil, n, width = ref.shape
    x = ref[0].reshape(dil * n, width)
    if x.dtype == _BF16:
        return _dot(pt_ref[...], x)
    hi = x.astype(_BF16)
    r1 = x - hi.astype(_F32)
    mid = r1.astype(_BF16)
    lo = (r1 - mid.astype(_F32)).astype(_BF16)
    res = _dot(pt_ref[...], jnp.concatenate([hi, mid, lo], axis=1))
    return (res[:, :width] + res[:, width:2 * width]) + res[:, 2 * width:]


def _merge_kernel(h_ref, ada_ref, ymla_ref, o0_ref, o1_ref, o2_ref, l0_ref, l1_ref, l2_ref, pt1_ref, pt2_ref,
                  ys_ref, gl_ref, wglu_ref, bglu_ref, wbr_ref, wout_ref, g_ref, b_ref, out_ref):
    h = h_ref[0]
    gate = ada_ref[0, 5:6, :]
    o0 = o0_ref[0, 0].astype(_F32)
    o1 = _to_token_order(o1_ref, pt1_ref)
    o2 = _to_token_order(o2_ref, pt2_ref)
    l0 = l0_ref[0, 0]
    l1 = _to_token_order(l1_ref, pt1_ref)
    l2 = _to_token_order(l2_ref, pt2_ref)
    m = jnp.maximum(jnp.maximum(l0, l1), l2)
    e0, e1, e2 = jnp.exp(l0 - m), jnp.exp(l1 - m), jnp.exp(l2 - m)
    inv = 1.0 / (e0 + e1 + e2)
    w0, w1, w2 = e0 * inv, e1 * inv, e2 * inv
    heads = []
    for i in range(DIL_HPG):
        lanes = slice(i * DIL_HEAD_DIM, (i + 1) * DIL_HEAD_DIM)
        heads.append(w0[:, i:i + 1] * o0[:, lanes] + w1[:, i:i + 1] * o1[:, lanes] + w2[:, i:i + 1] * o2[:, lanes])
    y_dil = jnp.concatenate(heads, axis=1).astype(_BF16)
    ys = ys_ref[0]
    glu = jax.nn.sigmoid(_dot(ys, wglu_ref[...]) + bglu_ref[...])
    y_ssm = (ys.astype(_F32) * glu).astype(_BF16)
    d = h.shape[-1]
    merged = (jax.nn.sigmoid(gl_ref[0, :, 0:d].astype(_F32)) * _dot(ymla_ref[0], wbr_ref[0])
              + jax.nn.sigmoid(gl_ref[0, :, d:2 * d].astype(_F32)) * _dot(y_dil, wbr_ref[1])
              + jax.nn.sigmoid(gl_ref[0, :, 2 * d:3 * d].astype(_F32)) * _dot(y_ssm, wbr_ref[2]))
    out = _dot(merged.astype(_BF16), wout_ref[...])
    y = DEEPNORM_ALPHA * h + gate * out
    out_ref[0] = _layer_norm(y, g_ref[...], b_ref[...])


def _merge_call(h, ada, y_mla, dil_o, dil_lse, y_s, proj_main, wglu, bglu, wbr, wout, ln_g, ln_b):
    bsz, s, d = h.shape
    tm = PERM_TILE
    tok = lambda b, i: (b, i, 0)
    grp = lambda b, i: (b, 0, i, 0)
    const2 = lambda b, i: (0, 0)
    bw = BRANCH_WIDTH
    d1, d2 = DIL_PAIRS[1][1], DIL_PAIRS[2][1]
    pt1 = jnp.asarray(_perm_matrix(d1).T, _BF16)
    pt2 = jnp.asarray(_perm_matrix(d2).T, _BF16)
    return pl.pallas_call(
        _merge_kernel,
        grid=(bsz, s // tm),
        in_specs=[
            pl.BlockSpec((1, tm, d), tok),
            pl.BlockSpec((1, 9, d), lambda b, i: (b, 0, 0)),
            pl.BlockSpec((1, tm, bw), tok),
            pl.BlockSpec((1, 1, tm, bw), grp),
            pl.BlockSpec((1, d1, tm // d1, bw), grp),
            pl.BlockSpec((1, d2, tm // d2, bw), grp),
            pl.BlockSpec((1, 1, tm, LANES), grp),
            pl.BlockSpec((1, d1, tm // d1, LANES), grp),
            pl.BlockSpec((1, d2, tm // d2, LANES), grp),
            pl.BlockSpec((PERM_TILE, PERM_TILE), const2),
            pl.BlockSpec((PERM_TILE, PERM_TILE), const2),
            pl.BlockSpec((1, tm, bw), tok),
            pl.BlockSpec((1, tm, N_BRANCH * d), lambda b, i: (b, i, COL_GATES // (N_BRANCH * d))),
            pl.BlockSpec((bw, bw), const2),
            pl.BlockSpec((1, bw), const2),
            pl.BlockSpec((N_BRANCH, bw, d), lambda b, i: (0, 0, 0)),
            pl.BlockSpec((d, d), const2),
            pl.BlockSpec((1, d), const2),
            pl.BlockSpec((1, d), const2),
        ],
        out_specs=pl.BlockSpec((1, tm, d), tok),
        out_shape=jax.ShapeDtypeStruct((bsz, s, d), _F32),
        compiler_params=_cparams(2),
        name="merge_sublayer",
    )(h, ada, y_mla, dil_o[0], dil_o[1], dil_o[2], dil_lse[0], dil_lse[1], dil_lse[2], pt1, pt2, y_s,
      proj_main, wglu, bglu, wbr, wout, ln_g, ln_b)


def _in_proj_weights(w_in, b_in):
    d = w_in.shape[0]
    o_kr = MLA_Q_RANK + MLA_KV_RANK
    o_dil = o_kr + MLA_ROPE_DIM
    o_u = o_dil + 3 * DIL_HEADS * DIL_HEAD_DIM
    o_g = o_u + SSM_WIDTH

    def dil_cols(arr, part, group):
        start = o_dil + part * DIL_HEADS * DIL_HEAD_DIM + group * DIL_GROUP_WIDTH
        cols = arr[..., start:start + DIL_GROUP_WIDTH]
        return cols * (DIL_SCALE * LOG2E) if part == 0 else cols

    def main(arr):
        pad = jnp.zeros(arr.shape[:-1] + (COL_Q1 - o_dil,), arr.dtype)
        return jnp.concatenate([arr[..., :o_dil], pad, dil_cols(arr, 0, 0), dil_cols(arr, 1, 0),
                                dil_cols(arr, 2, 0), arr[..., o_u:o_g], arr[..., o_g:]], axis=-1)

    def group(arr, g):
        return jnp.concatenate([dil_cols(arr, 0, g), dil_cols(arr, 1, g), dil_cols(arr, 2, g)], axis=-1)

    b2 = b_in.astype(_F32).reshape(1, -1)
    return ((main(w_in).astype(_BF16), main(b2)),
            (group(w_in, 1).astype(_BF16), group(b2, 1)),
            (group(w_in, 2).astype(_BF16), group(b2, 2)))


def _mla_weights(w_qb, w_kvb):
    qd = MLA_NOPE_DIM + MLA_ROPE_DIM
    half = MLA_ROPE_DIM // 2
    wq = w_qb.reshape(MLA_Q_RANK, MLA_HEADS, qd)
    zeros = jnp.zeros((MLA_Q_RANK, MLA_HEADS, MLA_HEAD_PAD - qd), w_qb.dtype)
    zn = jnp.zeros((MLA_Q_RANK, MLA_HEADS, MLA_NOPE_DIM), w_qb.dtype)
    wq_pad = jnp.concatenate([wq, zeros], axis=-1).reshape(MLA_Q_RANK, MLA_QK_WIDTH)
    t1 = wq[..., MLA_NOPE_DIM:MLA_NOPE_DIM + half]
    t2 = wq[..., MLA_NOPE_DIM + half:]
    wq_rot = jnp.concatenate([zn, -t2, t1, zeros], axis=-1).reshape(MLA_Q_RANK, MLA_QK_WIDTH)
    wkv = w_kvb.reshape(MLA_KV_RANK, MLA_HEADS, MLA_NOPE_DIM + MLA_V_DIM)
    zk = jnp.zeros((MLA_KV_RANK, MLA_HEADS, MLA_HEAD_PAD - MLA_NOPE_DIM), w_kvb.dtype)
    wk = jnp.concatenate([wkv[..., :MLA_NOPE_DIM], zk], axis=-1).reshape(MLA_KV_RANK, MLA_QK_WIDTH)
    wv = wkv[..., MLA_NOPE_DIM:].reshape(MLA_KV_RANK, MLA_HEADS // 2, 2, MLA_V_DIM)
    zv = jnp.zeros_like(wv[:, :, 0])
    wv_pad = jnp.concatenate([wv[:, :, 0], zv, zv, wv[:, :, 1]], axis=-1).reshape(MLA_KV_RANK, MLA_QK_WIDTH)
    return wq_pad.astype(_BF16), wq_rot.astype(_BF16), wk.astype(_BF16), wv_pad.astype(_BF16)


def _mla_ones_row():
    ones = np.zeros((1, MLA_QK_WIDTH), np.float32)
    for hd in range(MLA_HEADS):
        ones[0, hd * MLA_HEAD_PAD + (MLA_V_DIM if hd % 2 == 0 else 0)] = 1.0
    return jnp.asarray(ones)


def _rope_key_placement():
    half = MLA_ROPE_DIM // 2
    pk = np.zeros((LANES, MLA_QK_WIDTH), np.float32)
    pkr = np.zeros((LANES, MLA_QK_WIDTH), np.float32)
    for hd in range(MLA_HEADS):
        base = hd * MLA_HEAD_PAD + MLA_NOPE_DIM
        for i in range(MLA_ROPE_DIM):
            pk[i, base + i] = 1.0
        for i in range(half):
            pkr[half + i, base + i] = -1.0
            pkr[i, base + half + i] = 1.0
    return jnp.asarray(pk, _BF16), jnp.asarray(pkr, _BF16)


def _ffn_weights(w1, w3, w2):
    d = w1.shape[0]
    w1c = w1.reshape(d, N_FF_CHUNKS, FF_CHUNK).transpose(1, 0, 2).astype(_BF16)
    w3c = w3.reshape(d, N_FF_CHUNKS, FF_CHUNK).transpose(1, 0, 2).astype(_BF16)
    w2c = w2.reshape(N_FF_CHUNKS, FF_CHUNK, d).astype(_BF16)
    return w1c, w3c, w2c


def kernel(x, c, positions, w_ada, b_ada, ln_g, ln_b, ffn_w1, ffn_w3, ffn_w2, w_in, b_in, mla_q_norm, mla_kv_norm, mla_w_qb, mla_w_kvb, ssm_lambda_re, ssm_lambda_im, ssm_log_dt, ssm_b_re, ssm_b_im, ssm_c_re, ssm_c_im, ssm_d, ssm_w_glu, ssm_b_glu, w_br, w_out):
    bsz, s, d = x.shape
    assert d == D_MODEL and s % (DIL_PAIRS[2][1] * DIL_BAND) == 0, x.shape
    ada_all = _ada_call(c, w_ada, b_ada).reshape(DEPTH, bsz, 9, d)
    cos, sin = _rope_call(positions)
    pk, pkr = _rope_key_placement()
    ones_row = _mla_ones_row()
    h = x
    for l in range(DEPTH):
        ada = ada_all[l]
        h = _ffn_call(h, ada, 0, *_ffn_weights(ffn_w1[l, 0], ffn_w3[l, 0], ffn_w2[l, 0]),
                      ln_g[l, 0].reshape(1, d), ln_b[l, 0].reshape(1, d))
        (w_main, b_main), (w_g1, b_g1), (w_g2, b_g2) = _in_proj_weights(w_in[l], b_in[l])
        proj_main, proj_g1, proj_g2 = _proj_call(
            h, ada, w_main, b_main, [(DIL_PAIRS[1][1], w_g1, b_g1), (DIL_PAIRS[2][1], w_g2, b_g2)])
        proj_main4 = proj_main.reshape(bsz, 1, s, N_MAIN)
        wq, wqr, wk, wv = _mla_weights(mla_w_qb[l], mla_w_kvb[l])
        q, k, v = _mla_prep_call(proj_main, cos, sin, mla_q_norm[l].reshape(1, -1),
                                 mla_kv_norm[l].reshape(1, -1), wq, wqr, wk, wv, pk, pkr, ones_row)
        y_mla = _mla_attn_call(q, k, v)
        o0, lse0 = _dil_attn_call(proj_main4, COL_Q1 // DIL_GROUP_WIDTH, 0, DIL_PAIRS[0][1], 512, 1)
        o1, lse1 = _dil_attn_call(proj_g1, 0, 1, DIL_PAIRS[1][1], 512, 1)
        o2, lse2 = _dil_attn_call(proj_g2, 0, 2, DIL_PAIRS[2][1], 256, 4)
        y_s = _ssm_call(proj_main4, *_ssm_weights(ssm_lambda_re[l], ssm_lambda_im[l], ssm_log_dt[l],
                                                 ssm_b_re[l], ssm_b_im[l], ssm_c_re[l], ssm_c_im[l], ssm_d[l]))
        h = _merge_call(h, ada, y_mla, (o0, o1, o2), (lse0, lse1, lse2), y_s, proj_main,
                        ssm_w_glu[l].astype(_BF16), ssm_b_glu[l].reshape(1, -1), w_br[l].astype(_BF16),
                        w_out[l].astype(_BF16), ln_g[l, 1].reshape(1, d), ln_b[l, 1].reshape(1, d))
        h = _ffn_call(h, ada, 2, *_ffn_weights(ffn_w1[l, 1], ffn_w3[l, 1], ffn_w2[l, 1]),
                      ln_g[l, 2].reshape(1, d), ln_b[l, 2].reshape(1, d))
    return h
```

```python
import functools
import math

import numpy as np
import jax
import jax.numpy as jnp
from jax import lax
from jax.experimental import pallas as pl
from jax.experimental.pallas import tpu as pltpu

D_MODEL = 1024
DEPTH = 4
D_FF = 2816
MLA_HEADS = 8
MLA_Q_RANK = 384
MLA_KV_RANK = 256
MLA_NOPE_DIM = 64
MLA_ROPE_DIM = 32
MLA_V_DIM = 64
ROPE_THETA = 10000.0
DIL_PAIRS = ((128, 1), (512, 4), (2048, 16))
DIL_HPG = 4
DIL_HEADS = DIL_HPG * len(DIL_PAIRS)
DIL_HEAD_DIM = 128
DIL_BAND = 128
SSM_WIDTH = 512
SSM_GROUP_SIZE = 16
SSM_GROUPS = SSM_WIDTH // SSM_GROUP_SIZE
SSM_STATE = 64
N_BRANCH = 3
BRANCH_WIDTH = 512
DEEPNORM_ALPHA = (2 * DEPTH) ** 0.25
MACARON_WEIGHT = 0.5
LN_EPS = 1e-5
RMS_EPS = 1e-6

LANES = 128
V7X_VMEM_LIMIT_BYTES = 56 * 1024 * 1024

FF_CHUNK = 256
N_FF_CHUNKS = D_FF // FF_CHUNK
MLA_HEAD_PAD = 128
MLA_QK_WIDTH = MLA_HEADS * MLA_HEAD_PAD
MLA_V_WIDTH = MLA_HEADS * MLA_V_DIM
MLA_SCALE = (MLA_NOPE_DIM + MLA_ROPE_DIM) ** -0.5
DIL_SCALE = DIL_HEAD_DIM ** -0.5
DIL_GROUP_WIDTH = DIL_HPG * DIL_HEAD_DIM
MLA_A_WIDTH = 768
COL_Q1 = 1024
COL_U = COL_Q1 + 3 * DIL_GROUP_WIDTH
COL_GATES = COL_U + SSM_WIDTH
N_MAIN = COL_GATES + N_BRANCH * D_MODEL
PROJ_CHUNK = 512
PERM_TILE = 256
LOG2E = math.log2(math.e)
MLA_ROW_CHUNK = 64
MLA_HEADS_PER_STEP = 4
SSM_BLOCKS = 4
SSM_BLOCK_CH = SSM_WIDTH // SSM_BLOCKS
SSM_BLOCK_STATES = SSM_GROUPS * SSM_STATE // SSM_BLOCKS
SSM_SLABS = SSM_BLOCK_STATES // LANES
SSM_TT = 128
SSM_PITCH = 136

_F32 = jnp.float32
_BF16 = jnp.bfloat16
_NEG = -1e30


def _cparams(n_axes):
    return pltpu.CompilerParams(
        dimension_semantics=("arbitrary",) * n_axes,
        vmem_limit_bytes=V7X_VMEM_LIMIT_BYTES,
    )


def _resident(block_shape, index_map):
    return pl.BlockSpec(block_shape, index_map, pipeline_mode=pl.Buffered(1))


def _layer_norm(y, g, b):
    mu = jnp.mean(y, axis=-1, keepdims=True)
    yc = y - mu
    var = jnp.mean(yc * yc, axis=-1, keepdims=True)
    return yc * lax.rsqrt(var + LN_EPS) * g + b


def _dot(a, b):
    return jnp.dot(a, b, preferred_element_type=_F32)


def _dot_nt(a, b):
    return lax.dot_general(a, b, (((1,), (1,)), ((), ())), preferred_element_type=_F32)


def _ada_kernel(c_ref, w_ref, b_ref, o_ref):
    c = c_ref[...]
    cond = (c * jax.nn.sigmoid(c)).astype(_BF16)
    o_ref[0] = _dot(cond, w_ref[0].astype(_BF16)) + b_ref[0]


def _ada_call(c, w_ada, b_ada):
    depth, d, n = w_ada.shape
    bsz = c.shape[0]
    tn = 1536
    return pl.pallas_call(
        _ada_kernel,
        grid=(depth, n // tn),
        in_specs=[
            pl.BlockSpec((bsz, d), lambda l, j: (0, 0)),
            pl.BlockSpec((1, d, tn), lambda l, j: (l, 0, j)),
            pl.BlockSpec((1, 1, tn), lambda l, j: (l, 0, j)),
        ],
        out_specs=pl.BlockSpec((1, bsz, tn), lambda l, j: (l, 0, j)),
        out_shape=jax.ShapeDtypeStruct((depth, bsz, n), _F32),
        compiler_params=_cparams(2),
        name="ada",
    )(c, w_ada, b_ada.reshape(depth, 1, n))


def _rope_freq_lanes():
    half = MLA_ROPE_DIM // 2
    inv_freq = np.power(np.float32(ROPE_THETA), -np.arange(half, dtype=np.float32) / np.float32(half))
    f = np.zeros((1, MLA_HEAD_PAD), np.float32)
    f[0, MLA_NOPE_DIM:MLA_NOPE_DIM + half] = inv_freq
    f[0, MLA_NOPE_DIM + half:MLA_NOPE_DIM + 2 * half] = inv_freq
    return f


def _rope_kernel(pos_ref, f_ref, cos_ref, sin_ref):
    ang = pos_ref[0].astype(_F32) * f_ref[...]
    cos_ref[0] = jnp.cos(ang)
    sin_ref[0] = jnp.sin(ang)


def _rope_call(positions):
    bsz, s = positions.shape
    tm = min(s, 512)
    out = jax.ShapeDtypeStruct((bsz, s, MLA_HEAD_PAD), _F32)
    return pl.pallas_call(
        _rope_kernel,
        grid=(bsz, s // tm),
        in_specs=[
            pl.BlockSpec((1, tm, 1), lambda b, i: (b, i, 0)),
            pl.BlockSpec((1, MLA_HEAD_PAD), lambda b, i: (0, 0)),
        ],
        out_specs=[pl.BlockSpec((1, tm, MLA_HEAD_PAD), lambda b, i: (b, i, 0))] * 2,
        out_shape=[out, out],
        compiler_params=_cparams(2),
        name="rope_tables",
    )(positions.reshape(bsz, s, 1), jnp.asarray(_rope_freq_lanes()))


def _ffn_kernel(sub, h_ref, ada_ref, w1_ref, w3_ref, w2_ref, g_ref, b_ref, o_ref):
    h = h_ref[0]
    shift = ada_ref[0, 3 * sub + 0:3 * sub + 1, :]
    scale = ada_ref[0, 3 * sub + 1:3 * sub + 2, :]
    gate = ada_ref[0, 3 * sub + 2:3 * sub + 3, :]
    xm = (h * (1.0 + scale) + shift).astype(_BF16)
    acc = jnp.zeros(h.shape, _F32)
    for j in range(N_FF_CHUNKS):
        cols = slice(j * FF_CHUNK, (j + 1) * FF_CHUNK)
        a = _dot(xm, w1_ref[:, cols])
        b = _dot(xm, w3_ref[:, cols])
        g = (a * jax.nn.sigmoid(a) * b).astype(_BF16)
        acc = acc + _dot(g, w2_ref[cols, :])
    y = DEEPNORM_ALPHA * h + MACARON_WEIGHT * gate * acc
    o_ref[0] = _layer_norm(y, g_ref[...], b_ref[...])


def _ffn_call(h, ada, sub, w1, w3, w2, ln_g, ln_b):
    bsz, s, d = h.shape
    tm = min(s, 512)
    return pl.pallas_call(
        functools.partial(_ffn_kernel, sub),
        grid=(bsz, s // tm),
        in_specs=[
            pl.BlockSpec((1, tm, d), lambda b, i: (b, i, 0)),
            pl.BlockSpec((1, 9, d), lambda b, i: (b, 0, 0)),
            _resident((d, D_FF), lambda b, i: (0, 0)),
            _resident((d, D_FF), lambda b, i: (0, 0)),
            _resident((D_FF, d), lambda b, i: (0, 0)),
            pl.BlockSpec((1, d), lambda b, i: (0, 0)),
            pl.BlockSpec((1, d), lambda b, i: (0, 0)),
        ],
        out_specs=pl.BlockSpec((1, tm, d), lambda b, i: (b, i, 0)),
        out_shape=jax.ShapeDtypeStruct((bsz, s, d), _F32),
        compiler_params=_cparams(2),
        name="ffn_sublayer",
    )(h, ada, w1, w3, w2, ln_g, ln_b)


def _perm_matrix(dil):
    n = PERM_TILE // dil
    p = np.zeros((PERM_TILE, PERM_TILE), np.float32)
    for r in range(dil):
        for i in range(n):
            p[r * n + i, i * dil + r] = 1.0
    return p


def _proj_kernel(tm, dils, x_ref, ada_ref, wm_ref, bm_ref, *rest):
    ngrp = len(dils)
    grp_in = rest[:3 * ngrp]
    om_ref = rest[3 * ngrp]
    grp_out = rest[3 * ngrp + 1:]
    x = x_ref[0]
    shift = ada_ref[0, 3:4, :]
    scale = ada_ref[0, 4:5, :]
    xm = (x * (1.0 + scale) + shift).astype(_BF16)
    n_main = wm_ref.shape[1]
    for n0 in range(0, n_main, PROJ_CHUNK):
        n1 = min(n0 + PROJ_CHUNK, n_main)
        om_ref[0, :, n0:n1] = (_dot(xm, wm_ref[:, n0:n1]) + bm_ref[:, n0:n1]).astype(_BF16)
    for g, dil in enumerate(dils):
        w_ref, b_ref, p_ref = grp_in[3 * g:3 * g + 3]
        o_ref = grp_out[g]
        n = PERM_TILE // dil
        n_out = w_ref.shape[1]
        for t in range(tm // PERM_TILE):
            xp = _dot(p_ref[...], xm[t * PERM_TILE:(t + 1) * PERM_TILE]).astype(_BF16)
            for n0 in range(0, n_out, PROJ_CHUNK):
                n1 = min(n0 + PROJ_CHUNK, n_out)
                res = (_dot(xp, w_ref[:, n0:n1]) + b_ref[:, n0:n1]).astype(_BF16)
                for r in range(dil):
                    o_ref[0, r, t * n:(t + 1) * n, n0:n1] = res[r * n:(r + 1) * n]


def _proj_call(h, ada, w_main, b_main, groups):
    bsz, s, d = h.shape
    tm = min(s, 512)
    dils = tuple(g[0] for g in groups)
    const = lambda b, i: (0, 0)
    in_specs = [
        pl.BlockSpec((1, tm, d), lambda b, i: (b, i, 0)),
        pl.BlockSpec((1, 9, d), lambda b, i: (b, 0, 0)),
        _resident(w_main.shape, const),
        pl.BlockSpec(b_main.shape, const),
    ]
    args = [h, ada, w_main, b_main]
    out_specs = [pl.BlockSpec((1, tm, w_main.shape[1]), lambda b, i: (b, i, 0))]
    out_shape = [jax.ShapeDtypeStruct((bsz, s, w_main.shape[1]), _BF16)]
    for dil, w, bias in groups:
        in_specs += [_resident(w.shape, const), pl.BlockSpec(bias.shape, const),
                     pl.BlockSpec((PERM_TILE, PERM_TILE), const)]
        args += [w, bias, jnp.asarray(_perm_matrix(dil), _BF16)]
        out_specs.append(pl.BlockSpec((1, dil, tm // dil, w.shape[1]), lambda b, i: (b, 0, i, 0)))
        out_shape.append(jax.ShapeDtypeStruct((bsz, dil, s // dil, w.shape[1]), _BF16))
    return pl.pallas_call(
        functools.partial(_proj_kernel, tm, dils),
        grid=(bsz, s // tm),
        in_specs=in_specs,
        out_specs=out_specs,
        out_shape=out_shape,
        compiler_params=_cparams(2),
        name="in_proj",
    )(*args)


def _mla_prep_kernel(a_ref, cos_ref, sin_ref, qg_ref, kvg_ref, wq_ref, wqr_ref, wk_ref, wv_ref,
                     pk_ref, pkr_ref, ones_ref, q_ref, k_ref, v_ref):
    a = a_ref[0]
    qa = a[:, :MLA_Q_RANK].astype(_F32)
    kva = a[:, MLA_Q_RANK:MLA_Q_RANK + MLA_KV_RANK].astype(_F32)
    kr = a[:, MLA_Q_RANK + MLA_KV_RANK:]
    qn = (qa * lax.rsqrt(jnp.mean(qa * qa, axis=-1, keepdims=True) + RMS_EPS) * qg_ref[...]).astype(_BF16)
    kvn = (kva * lax.rsqrt(jnp.mean(kva * kva, axis=-1, keepdims=True) + RMS_EPS) * kvg_ref[...]).astype(_BF16)
    cos = cos_ref[0]
    sin = sin_ref[0]
    q = _dot(qn, wq_ref[...])
    qrot = _dot(qn, wqr_ref[...])
    k = _dot(kvn, wk_ref[...]) + _dot(kr, pk_ref[...])
    krot = _dot(kr, pkr_ref[...])
    for hd in range(MLA_HEADS):
        sl = slice(hd * MLA_HEAD_PAD, (hd + 1) * MLA_HEAD_PAD)
        q_ref[0, :, sl] = ((q[:, sl] * cos + qrot[:, sl] * sin) * (MLA_SCALE * LOG2E)).astype(_BF16)
        k_ref[0, :, sl] = (k[:, sl] * cos + krot[:, sl] * sin).astype(_BF16)
    v_ref[0] = (_dot(kvn, wv_ref[...]) + ones_ref[...]).astype(_BF16)


def _mla_prep_call(proj_main, cos, sin, qg, kvg, wq, wqr, wk, wv, pk, pkr, ones_row):
    bsz, s, _ = proj_main.shape
    tm = min(s, 512)
    const = lambda b, i: (0, 0)
    return pl.pallas_call(
        _mla_prep_kernel,
        grid=(bsz, s // tm),
        in_specs=[
            pl.BlockSpec((1, tm, MLA_A_WIDTH), lambda b, i: (b, i, 0)),
            pl.BlockSpec((1, tm, MLA_HEAD_PAD), lambda b, i: (b, i, 0)),
            pl.BlockSpec((1, tm, MLA_HEAD_PAD), lambda b, i: (b, i, 0)),
            pl.BlockSpec((1, MLA_Q_RANK), const),
            pl.BlockSpec((1, MLA_KV_RANK), const),
            pl.BlockSpec((MLA_Q_RANK, MLA_QK_WIDTH), const),
            pl.BlockSpec((MLA_Q_RANK, MLA_QK_WIDTH), const),
            pl.BlockSpec((MLA_KV_RANK, MLA_QK_WIDTH), const),
            pl.BlockSpec((MLA_KV_RANK, MLA_QK_WIDTH), const),
            pl.BlockSpec((LANES, MLA_QK_WIDTH), const),
            pl.BlockSpec((LANES, MLA_QK_WIDTH), const),
            pl.BlockSpec((1, MLA_QK_WIDTH), const),
        ],
        out_specs=[pl.BlockSpec((1, tm, MLA_QK_WIDTH), lambda b, i: (b, i, 0))] * 3,
        out_shape=[jax.ShapeDtypeStruct((bsz, s, MLA_QK_WIDTH), _BF16)] * 3,
        compiler_params=_cparams(2),
        name="mla_prep",
    )(proj_main, cos, sin, qg, kvg, wq, wqr, wk, wv, pk, pkr, ones_row)


def _mla_attn_kernel(tq, q_ref, k_ref, v_ref, o_ref, s_ref, p_ref):
    nh = MLA_HEADS_PER_STEP
    qi = pl.program_id(2)
    row = lax.broadcasted_iota(jnp.int32, (tq, tq), 0)
    col = lax.broadcasted_iota(jnp.int32, (tq, tq), 1)
    causal = col <= row
    lanes = [slice(e * MLA_HEAD_PAD, (e + 1) * MLA_HEAD_PAD) for e in range(nh)]
    qs = [q_ref[0, :, lanes[e]] for e in range(nh)]

    def scores(j, slot):
        start = pl.multiple_of(j * tq, tq)
        for e in range(nh):
            s_ref[slot, e] = _dot_nt(qs[e], k_ref[0, pl.ds(start, tq), lanes[e]])

    def softmax_pv(j, slot, carry, masked):
        start = pl.multiple_of(j * tq, tq)
        new = []
        for e in range(nh):
            m, acc = carry[e]
            m_parts = []
            for c in range(tq // MLA_ROW_CHUNK):
                rows = slice(c * MLA_ROW_CHUNK, (c + 1) * MLA_ROW_CHUNK)
                s = s_ref[slot, e, rows, :]
                if masked:
                    s = jnp.where(causal[rows], s, _NEG)
                m_c = jnp.maximum(m[rows], jnp.max(s, axis=-1, keepdims=True))
                p_ref[e, rows, :] = jnp.exp2(s - m_c).astype(_BF16)
                m_parts.append(m_c)
            m_new = jnp.concatenate(m_parts, axis=0)
            v = v_ref[0, pl.ds(start, tq), lanes[e]]
            acc = jnp.exp2(m - m_new) * acc + _dot(p_ref[e], v)
            new.append((m_new, acc))
        return tuple(new)

    def pair(jj, carry):
        t = 2 * jj
        scores(t + 1, 1)
        carry = softmax_pv(t, 0, carry, False)
        scores(t + 2, 0)
        return softmax_pv(t + 1, 1, carry, False)

    def tail_odd(carry):
        scores(qi, 1)
        carry = softmax_pv(qi - 1, 0, carry, False)
        return softmax_pv(qi, 1, carry, True)

    def tail_even(carry):
        return softmax_pv(qi, 0, carry, True)

    init = tuple((jnp.full((tq, 1), _NEG, _F32), jnp.zeros((tq, MLA_HEAD_PAD), _F32)) for _ in range(nh))
    scores(0, 0)
    carry = lax.fori_loop(0, qi // 2, pair, init)
    carry = lax.cond(qi % 2 == 1, tail_odd, tail_even, carry)
    lane = lax.broadcasted_iota(jnp.int32, (tq, MLA_HEAD_PAD), 1)
    for e2 in range(nh // 2):
        acc0, acc1 = carry[2 * e2][1], carry[2 * e2 + 1][1]
        out0 = acc0 * (1.0 / acc0[:, MLA_V_DIM:MLA_V_DIM + 1])
        out1 = acc1 * (1.0 / acc1[:, 0:1])
        o_ref[0, :, lanes[e2]] = jnp.where(lane < MLA_V_DIM, out0, out1).astype(_BF16)


def _mla_attn_call(q, k, v):
    bsz, s, _ = q.shape
    tq = min(s, 512)
    nh = MLA_HEADS_PER_STEP
    qk_w = nh * MLA_HEAD_PAD
    return pl.pallas_call(
        functools.partial(_mla_attn_kernel, tq),
        grid=(bsz, MLA_HEADS // nh, s // tq),
        in_specs=[
            pl.BlockSpec((1, tq, qk_w), lambda b, h, i: (b, i, h)),
            pl.BlockSpec((1, s, qk_w), lambda b, h, i: (b, 0, h)),
            pl.BlockSpec((1, s, qk_w), lambda b, h, i: (b, 0, h)),
        ],
        out_specs=pl.BlockSpec((1, tq, nh * MLA_V_DIM), lambda b, h, i: (b, i, h)),
        out_shape=jax.ShapeDtypeStruct((bsz, s, MLA_V_WIDTH), _BF16),
        scratch_shapes=[pltpu.VMEM((2, nh, tq, tq), _F32), pltpu.VMEM((nh, tq, tq), _BF16)],
        compiler_params=_cparams(3),
        name="mla_attention",
    )(q, k, v)


def _alibi_slope(head):
    return float(np.exp2(np.float32(-8.0) * (np.float32(head) + np.float32(1.0)) / np.float32(DIL_HEADS)))


def _dil_attn_kernel(tq, kw, single_tile, nres, group, dil, q_ref, k_ref, v_ref, o_ref, lse_ref, bias_ref):
    qi = pl.program_id(2)
    first_step = (pl.program_id(0) == 0) & (pl.program_id(1) == 0) & (qi == 0)
    nb = DIL_BAND

    @pl.when(first_step)
    def _():
        row = lax.broadcasted_iota(jnp.int32, (nb, kw), 0)
        col = lax.broadcasted_iota(jnp.int32, (nb, kw), 1)
        for variant, rel in ((0, nb + row - col), (1, row - col)):
            ok = (rel >= 0) & (rel <= DIL_BAND)
            relf = rel.astype(_F32)
            for i in range(DIL_HPG):
                slope = _alibi_slope(group * DIL_HPG + i) * dil * LOG2E
                bias_ref[i, variant] = jnp.where(ok, -slope * relf, _NEG)

    q0 = pl.multiple_of(qi * tq, tq)
    lane = lax.broadcasted_iota(jnp.int32, (nb, LANES), 1)
    for rr in range(nres):
        for sb in range(tq // nb):
            rows = slice(sb * nb, (sb + 1) * nb)
            if sb == 0:
                start = pl.multiple_of(jnp.maximum(q0 - nb, 0), nb)
            else:
                start = pl.multiple_of(q0 + (sb - 1) * nb, nb)
            lse_blk = jnp.zeros((nb, LANES), _F32)
            for i in range(DIL_HPG):
                lanes = slice(i * DIL_HEAD_DIM, (i + 1) * DIL_HEAD_DIM)
                if sb > 0:
                    bias = bias_ref[i, 0]
                elif single_tile:
                    bias = bias_ref[i, 1]
                else:
                    bias = jnp.where(qi == 0, bias_ref[i, 1], bias_ref[i, 0])
                s = _dot_nt(q_ref[0, rr, rows, lanes], k_ref[0, rr, pl.ds(start, kw), lanes]) + bias
                m = jnp.max(s, axis=-1, keepdims=True)
                e = jnp.exp2(s - m)
                l = jnp.sum(e, axis=-1, keepdims=True)
                o = _dot(e.astype(_BF16), v_ref[0, rr, pl.ds(start, kw), lanes])
                o_ref[0, rr, rows, lanes] = (o * (1.0 / l)).astype(_BF16)
                lse_blk = jnp.where(lane == i, (m + jnp.log2(l)) * (1.0 / LOG2E), lse_blk)
            lse_ref[0, rr, rows, :] = lse_blk


def _dil_attn_call(arr, col_block0, group, dil, tq, nres):
    bsz, _, length, _ = arr.shape
    w = DIL_GROUP_WIDTH
    tq = min(tq, length)
    kw = min(2 * DIL_BAND, length)
    nres = min(nres, dil)
    return pl.pallas_call(
        functools.partial(_dil_attn_kernel, tq, kw, length == tq, nres, group, dil),
        grid=(bsz, dil // nres, length // tq),
        in_specs=[
            pl.BlockSpec((1, nres, tq, w), lambda b, r, i: (b, r, i, col_block0)),
            pl.BlockSpec((1, nres, length, w), lambda b, r, i: (b, r, 0, col_block0 + 1)),
            pl.BlockSpec((1, nres, length, w), lambda b, r, i: (b, r, 0, col_block0 + 2)),
        ],
        out_specs=[
            pl.BlockSpec((1, nres, tq, w), lambda b, r, i: (b, r, i, 0)),
            pl.BlockSpec((1, nres, tq, LANES), lambda b, r, i: (b, r, i, 0)),
        ],
        out_shape=[
            jax.ShapeDtypeStruct((bsz, dil, length, w), _BF16),
            jax.ShapeDtypeStruct((bsz, dil, length, LANES), _F32),
        ],
        scratch_shapes=[pltpu.VMEM((DIL_HPG, 2, DIL_BAND, kw), _F32)],
        compiler_params=_cparams(3),
        name=f"dil_attention_g{group}",
    )(arr, arr, arr)


def _ssm_kernel(nb, u_ref, bw_ref, cw_ref, are_ref, aim_ref, d_ref, o_ref, bx_ref, st_ref):
    ti = pl.program_id(1)

    @pl.when(ti == 0)
    def _():
        st_ref[...] = jnp.zeros(st_ref.shape, _F32)

    for b in range(nb):
        bu = _dot(u_ref[b, 0], bw_ref[0])
        for sidx in range(2 * SSM_SLABS):
            bx_ref[sidx, b * SSM_PITCH:b * SSM_PITCH + SSM_TT, :] = bu[:, sidx * LANES:(sidx + 1) * LANES]

    a_re = [jnp.broadcast_to(are_ref[0, :, k * LANES:(k + 1) * LANES], (nb, LANES)) for k in range(SSM_SLABS)]
    a_im = [jnp.broadcast_to(aim_ref[0, :, k * LANES:(k + 1) * LANES], (nb, LANES)) for k in range(SSM_SLABS)]

    def step(t, carry):
        xr, xi = carry
        nr, ni = [], []
        for k in range(SSM_SLABS):
            rows = pl.ds(t, nb, stride=SSM_PITCH)
            br = bx_ref[k, rows, :]
            bi = bx_ref[SSM_SLABS + k, rows, :]
            r = a_re[k] * xr[k] - a_im[k] * xi[k] + br
            i = a_re[k] * xi[k] + a_im[k] * xr[k] + bi
            bx_ref[k, rows, :] = r
            bx_ref[SSM_SLABS + k, rows, :] = i
            nr.append(r)
            ni.append(i)
        return tuple(nr), tuple(ni)

    x0 = (tuple(st_ref[k] for k in range(SSM_SLABS)),
          tuple(st_ref[SSM_SLABS + k] for k in range(SSM_SLABS)))
    xr, xi = lax.fori_loop(0, SSM_TT, step, x0, unroll=8)
    for k in range(SSM_SLABS):
        st_ref[k] = xr[k]
        st_ref[SSM_SLABS + k] = xi[k]

    dskip = d_ref[0]
    for b in range(nb):
        xs = jnp.concatenate(
            [bx_ref[sidx, b * SSM_PITCH:b * SSM_PITCH + SSM_TT, :] for sidx in range(2 * SSM_SLABS)],
            axis=1).astype(_BF16)
        y = _dot(xs, cw_ref[0]) + dskip * u_ref[b, 0].astype(_F32)
        o_ref[b] = jax.nn.gelu(y).astype(_BF16)


def _ssm_call(proj_main, bw, cw, a_re, a_im, d_skip):
    bsz, _, s, _ = proj_main.shape
    u_block0 = COL_U // SSM_BLOCK_CH
    nstate = 2 * SSM_BLOCK_STATES
    return pl.pallas_call(
        functools.partial(_ssm_kernel, bsz),
        grid=(SSM_BLOCKS, s // SSM_TT),
        in_specs=[
            pl.BlockSpec((bsz, 1, SSM_TT, SSM_BLOCK_CH), lambda m, t: (0, 0, t, u_block0 + m)),
            pl.BlockSpec((1, SSM_BLOCK_CH, nstate), lambda m, t: (m, 0, 0)),
            pl.BlockSpec((1, nstate, SSM_BLOCK_CH), lambda m, t: (m, 0, 0)),
            pl.BlockSpec((1, 1, SSM_BLOCK_STATES), lambda m, t: (m, 0, 0)),
            pl.BlockSpec((1, 1, SSM_BLOCK_STATES), lambda m, t: (m, 0, 0)),
            pl.BlockSpec((1, 1, SSM_BLOCK_CH), lambda m, t: (m, 0, 0)),
        ],
        out_specs=pl.BlockSpec((bsz, SSM_TT, SSM_BLOCK_CH), lambda m, t: (0, t, m)),
        out_shape=jax.ShapeDtypeStruct((bsz, s, SSM_WIDTH), _BF16),
        scratch_shapes=[
            pltpu.VMEM((2 * SSM_SLABS, bsz * SSM_PITCH, LANES), _F32),
            pltpu.VMEM((2 * SSM_SLABS, bsz, LANES), _F32),
        ],
        compiler_params=_cparams(2),
        name="s5_scan",
    )(proj_main, bw, cw, a_re, a_im, d_skip)


def _ssm_weights(lam_re, lam_im, log_dt, b_re, b_im, c_re, c_im, d_skip):
    lam = lax.complex(lam_re.astype(_F32), lam_im.astype(_F32))
    dt = jnp.exp(log_dt.astype(_F32))[:, None]
    lam_bar = jnp.exp(lam * dt)
    b_bar = ((lam_bar - 1.0) / lam)[..., None] * lax.complex(b_re.astype(_F32), b_im.astype(_F32))
    gpb = SSM_GROUPS // SSM_BLOCKS
    eye = jnp.eye(gpb, dtype=_F32)

    def in_weights(part):
        w = part.reshape(SSM_BLOCKS, gpb, SSM_STATE, SSM_GROUP_SIZE)
        w = jnp.einsum("mgph,gk->mghkp", w, eye)
        return w.reshape(SSM_BLOCKS, SSM_BLOCK_CH, SSM_BLOCK_STATES)

    def out_weights(part):
        w = part.reshape(SSM_BLOCKS, gpb, SSM_GROUP_SIZE, SSM_STATE)
        w = jnp.einsum("mgkp,gj->mgpjk", w, eye)
        return w.reshape(SSM_BLOCKS, SSM_BLOCK_STATES, SSM_BLOCK_CH)

    bw = jnp.concatenate([in_weights(jnp.real(b_bar)), in_weights(jnp.imag(b_bar))], axis=2).astype(_BF16)
    cw = jnp.concatenate([out_weights(c_re.astype(_F32)), -out_weights(c_im.astype(_F32))], axis=1).astype(_BF16)
    a_re = jnp.real(lam_bar).reshape(SSM_BLOCKS, 1, SSM_BLOCK_STATES)
    a_im = jnp.imag(lam_bar).reshape(SSM_BLOCKS, 1, SSM_BLOCK_STATES)
    return bw, cw, a_re, a_im, d_skip.astype(_F32).reshape(SSM_BLOCKS, 1, SSM_BLOCK_CH)


def _to_token_order(ref, pt_ref):
    _, dil, n, width = ref.shape
    x = ref[0].reshape(dil * n, width)
    if x.dtype == _BF16:
        return _dot(pt_ref[...], x)
    hi = x.astype(_BF16)
    r1 = x - hi.astype(_F32)
    mid = r1.astype(_BF16)
    lo = (r1 - mid.astype(_F32)).astype(_BF16)
    res = _dot(pt_ref[...], jnp.concatenate([hi, mid, lo], axis=1))
    return (res[:, :width] + res[:, width:2 * width]) + res[:, 2 * width:]


def _merge_kernel(h_ref, ada_ref, ymla_ref, o0_ref, o1_ref, o2_ref, l0_ref, l1_ref, l2_ref, pt1_ref, pt2_ref,
                  ys_ref, gl_ref, wglu_ref, bglu_ref, wbr_ref, wout_ref, g_ref, b_ref, out_ref):
    h = h_ref[0]
    gate = ada_ref[0, 5:6, :]
    o0 = o0_ref[0, 0].astype(_F32)
    o1 = _to_token_order(o1_ref, pt1_ref)
    o2 = _to_token_order(o2_ref, pt2_ref)
    l0 = l0_ref[0, 0]
    l1 = _to_token_order(l1_ref, pt1_ref)
    l2 = _to_token_order(l2_ref, pt2_ref)
    m = jnp.maximum(jnp.maximum(l0, l1), l2)
    e0, e1, e2 = jnp.exp(l0 - m), jnp.exp(l1 - m), jnp.exp(l2 - m)
    inv = 1.0 / (e0 + e1 + e2)
    w0, w1, w2 = e0 * inv, e1 * inv, e2 * inv
    heads = []
    for i in range(DIL_HPG):
        lanes = slice(i * DIL_HEAD_DIM, (i + 1) * DIL_HEAD_DIM)
        heads.append(w0[:, i:i + 1] * o0[:, lanes] + w1[:, i:i + 1] * o1[:, lanes] + w2[:, i:i + 1] * o2[:, lanes])
    y_dil = jnp.concatenate(heads, axis=1).astype(_BF16)
    ys = ys_ref[0]
    glu = jax.nn.sigmoid(_dot(ys, wglu_ref[...]) + bglu_ref[...])
    y_ssm = (ys.astype(_F32) * glu).astype(_BF16)
    d = h.shape[-1]
    merged = (jax.nn.sigmoid(gl_ref[0, :, 0:d].astype(_F32)) * _dot(ymla_ref[0], wbr_ref[0])
              + jax.nn.sigmoid(gl_ref[0, :, d:2 * d].astype(_F32)) * _dot(y_dil, wbr_ref[1])
              + jax.nn.sigmoid(gl_ref[0, :, 2 * d:3 * d].astype(_F32)) * _dot(y_ssm, wbr_ref[2]))
    out = _dot(merged.astype(_BF16), wout_ref[...])
    y = DEEPNORM_ALPHA * h + gate * out
    out_ref[0] = _layer_norm(y, g_ref[...], b_ref[...])


def _merge_call(h, ada, y_mla, dil_o, dil_lse, y_s, proj_main, wglu, bglu, wbr, wout, ln_g, ln_b):
    bsz, s, d = h.shape
    tm = PERM_TILE
    tok = lambda b, i: (b, i, 0)
    grp = lambda b, i: (b, 0, i, 0)
    const2 = lambda b, i: (0, 0)
    bw = BRANCH_WIDTH
    d1, d2 = DIL_PAIRS[1][1], DIL_PAIRS[2][1]
    pt1 = jnp.asarray(_perm_matrix(d1).T, _BF16)
    pt2 = jnp.asarray(_perm_matrix(d2).T, _BF16)
    return pl.pallas_call(
        _merge_kernel,
        grid=(bsz, s // tm),
        in_specs=[
            pl.BlockSpec((1, tm, d), tok),
            pl.BlockSpec((1, 9, d), lambda b, i: (b, 0, 0)),
            pl.BlockSpec((1, tm, bw), tok),
            pl.BlockSpec((1, 1, tm, bw), grp),
            pl.BlockSpec((1, d1, tm // d1, bw), grp),
            pl.BlockSpec((1, d2, tm // d2, bw), grp),
            pl.BlockSpec((1, 1, tm, LANES), grp),
            pl.BlockSpec((1, d1, tm // d1, LANES), grp),
            pl.BlockSpec((1, d2, tm // d2, LANES), grp),
            pl.BlockSpec((PERM_TILE, PERM_TILE), const2),
            pl.BlockSpec((PERM_TILE, PERM_TILE), const2),
            pl.BlockSpec((1, tm, bw), tok),
            pl.BlockSpec((1, tm, N_BRANCH * d), lambda b, i: (b, i, COL_GATES // (N_BRANCH * d))),
            pl.BlockSpec((bw, bw), const2),
            pl.BlockSpec((1, bw), const2),
            pl.BlockSpec((N_BRANCH, bw, d), lambda b, i: (0, 0, 0)),
            pl.BlockSpec((d, d), const2),
            pl.BlockSpec((1, d), const2),
            pl.BlockSpec((1, d), const2),
        ],
        out_specs=pl.BlockSpec((1, tm, d), tok),
        out_shape=jax.ShapeDtypeStruct((bsz, s, d), _F32),
        compiler_params=_cparams(2),
        name="merge_sublayer",
    )(h, ada, y_mla, dil_o[0], dil_o[1], dil_o[2], dil_lse[0], dil_lse[1], dil_lse[2], pt1, pt2, y_s,
      proj_main, wglu, bglu, wbr, wout, ln_g, ln_b)


def _in_proj_weights(w_in, b_in):
    d = w_in.shape[0]
    o_kr = MLA_Q_RANK + MLA_KV_RANK
    o_dil = o_kr + MLA_ROPE_DIM
    o_u = o_dil + 3 * DIL_HEADS * DIL_HEAD_DIM
    o_g = o_u + SSM_WIDTH

    def dil_cols(arr, part, group):
        start = o_dil + part * DIL_HEADS * DIL_HEAD_DIM + group * DIL_GROUP_WIDTH
        cols = arr[..., start:start + DIL_GROUP_WIDTH]
        return cols * (DIL_SCALE * LOG2E) if part == 0 else cols

    def main(arr):
        pad = jnp.zeros(arr.shape[:-1] + (COL_Q1 - o_dil,), arr.dtype)
        return jnp.concatenate([arr[..., :o_dil], pad, dil_cols(arr, 0, 0), dil_cols(arr, 1, 0),
                                dil_cols(arr, 2, 0), arr[..., o_u:o_g], arr[..., o_g:]], axis=-1)

    def group(arr, g):
        return jnp.concatenate([dil_cols(arr, 0, g), dil_cols(arr, 1, g), dil_cols(arr, 2, g)], axis=-1)

    b2 = b_in.astype(_F32).reshape(1, -1)
    return ((main(w_in).astype(_BF16), main(b2)),
            (group(w_in, 1).astype(_BF16), group(b2, 1)),
            (group(w_in, 2).astype(_BF16), group(b2, 2)))


def _mla_weights(w_qb, w_kvb):
    qd = MLA_NOPE_DIM + MLA_ROPE_DIM
    half = MLA_ROPE_DIM // 2
    wq = w_qb.reshape(MLA_Q_RANK, MLA_HEADS, qd)
    zeros = jnp.zeros((MLA_Q_RANK, MLA_HEADS, MLA_HEAD_PAD - qd), w_qb.dtype)
    zn = jnp.zeros((MLA_Q_RANK, MLA_HEADS, MLA_NOPE_DIM), w_qb.dtype)
    wq_pad = jnp.concatenate([wq, zeros], axis=-1).reshape(MLA_Q_RANK, MLA_QK_WIDTH)
    t1 = wq[..., MLA_NOPE_DIM:MLA_NOPE_DIM + half]
    t2 = wq[..., MLA_NOPE_DIM + half:]
    wq_rot = jnp.concatenate([zn, -t2, t1, zeros], axis=-1).reshape(MLA_Q_RANK, MLA_QK_WIDTH)
    wkv = w_kvb.reshape(MLA_KV_RANK, MLA_HEADS, MLA_NOPE_DIM + MLA_V_DIM)
    zk = jnp.zeros((MLA_KV_RANK, MLA_HEADS, MLA_HEAD_PAD - MLA_NOPE_DIM), w_kvb.dtype)
    wk = jnp.concatenate([wkv[..., :MLA_NOPE_DIM], zk], axis=-1).reshape(MLA_KV_RANK, MLA_QK_WIDTH)
    wv = wkv[..., MLA_NOPE_DIM:].reshape(MLA_KV_RANK, MLA_HEADS // 2, 2, MLA_V_DIM)
    zv = jnp.zeros_like(wv[:, :, 0])
    wv_pad = jnp.concatenate([wv[:, :, 0], zv, zv, wv[:, :, 1]], axis=-1).reshape(MLA_KV_RANK, MLA_QK_WIDTH)
    return wq_pad.astype(_BF16), wq_rot.astype(_BF16), wk.astype(_BF16), wv_pad.astype(_BF16)


def _mla_ones_row():
    ones = np.zeros((1, MLA_QK_WIDTH), np.float32)
    for hd in range(MLA_HEADS):
        ones[0, hd * MLA_HEAD_PAD + (MLA_V_DIM if hd % 2 == 0 else 0)] = 1.0
    return jnp.asarray(ones)


def _rope_key_placement():
    half = MLA_ROPE_DIM // 2
    pk = np.zeros((LANES, MLA_QK_WIDTH), np.float32)
    pkr = np.zeros((LANES, MLA_QK_WIDTH), np.float32)
    for hd in range(MLA_HEADS):
        base = hd * MLA_HEAD_PAD + MLA_NOPE_DIM
        for i in range(MLA_ROPE_DIM):
            pk[i, base + i] = 1.0
        for i in range(half):
            pkr[half + i, base + i] = -1.0
            pkr[i, base + half + i] = 1.0
    return jnp.asarray(pk, _BF16), jnp.asarray(pkr, _BF16)


def _ffn_weights(w1, w3, w2):
    return w1.astype(_BF16), w3.astype(_BF16), w2.astype(_BF16)


def kernel(x, c, positions, w_ada, b_ada, ln_g, ln_b, ffn_w1, ffn_w3, ffn_w2, w_in, b_in, mla_q_norm, mla_kv_norm, mla_w_qb, mla_w_kvb, ssm_lambda_re, ssm_lambda_im, ssm_log_dt, ssm_b_re, ssm_b_im, ssm_c_re, ssm_c_im, ssm_d, ssm_w_glu, ssm_b_glu, w_br, w_out):
    bsz, s, d = x.shape
    assert d == D_MODEL and s % (DIL_PAIRS[2][1] * DIL_BAND) == 0, x.shape
    ada_all = _ada_call(c, w_ada, b_ada).reshape(DEPTH, bsz, 9, d)
    cos, sin = _rope_call(positions)
    pk, pkr = _rope_key_placement()
    ones_row = _mla_ones_row()
    h = x
    for l in range(DEPTH):
        ada = ada_all[l]
        h = _ffn_call(h, ada, 0, *_ffn_weights(ffn_w1[l, 0], ffn_w3[l, 0], ffn_w2[l, 0]),
                      ln_g[l, 0].reshape(1, d), ln_b[l, 0].reshape(1, d))
        (w_main, b_main), (w_g1, b_g1), (w_g2, b_g2) = _in_proj_weights(w_in[l], b_in[l])
        proj_main, proj_g1, proj_g2 = _proj_call(
            h, ada, w_main, b_main, [(DIL_PAIRS[1][1], w_g1, b_g1), (DIL_PAIRS[2][1], w_g2, b_g2)])
        proj_main4 = proj_main.reshape(bsz, 1, s, N_MAIN)
        wq, wqr, wk, wv = _mla_weights(mla_w_qb[l], mla_w_kvb[l])
        q, k, v = _mla_prep_call(proj_main, cos, sin, mla_q_norm[l].reshape(1, -1),
                                 mla_kv_norm[l].reshape(1, -1), wq, wqr, wk, wv, pk, pkr, ones_row)
        y_mla = _mla_attn_call(q, k, v)
        o0, lse0 = _dil_attn_call(proj_main4, COL_Q1 // DIL_GROUP_WIDTH, 0, DIL_PAIRS[0][1], 512, 1)
        o1, lse1 = _dil_attn_call(proj_g1, 0, 1, DIL_PAIRS[1][1], 512, 1)
        o2, lse2 = _dil_attn_call(proj_g2, 0, 2, DIL_PAIRS[2][1], 256, 4)
        y_s = _ssm_call(proj_main4, *_ssm_weights(ssm_lambda_re[l], ssm_lambda_im[l], ssm_log_dt[l],
                                                 ssm_b_re[l], ssm_b_im[l], ssm_c_re[l], ssm_c_im[l], ssm_d[l]))
        h = _merge_call(h, ada, y_mla, (o0, o1, o2), (lse0, lse1, lse2), y_s, proj_main,
                        ssm_w_glu[l].astype(_BF16), ssm_b_glu[l].reshape(1, -1), w_br[l].astype(_BF16),
                        w_out[l].astype(_BF16), ln_g[l, 1].reshape(1, d), ln_b[l, 1].reshape(1, d))
        h = _ffn_call(h, ada, 2, *_ffn_weights(ffn_w1[l, 1], ffn_w3[l, 1], ffn_w2[l, 1]),
                      ln_g[l, 2].reshape(1, d), ln_b[l, 2].reshape(1, d))
    return h
```

```python
import functools
import math

import numpy as np
import jax
import jax.numpy as jnp
from jax import lax
from jax.experimental import pallas as pl
from jax.experimental.pallas import tpu as pltpu

D_MODEL = 1024
DEPTH = 4
D_FF = 2816
MLA_HEADS = 8
MLA_Q_RANK = 384
MLA_KV_RANK = 256
MLA_NOPE_DIM = 64
MLA_ROPE_DIM = 32
MLA_V_DIM = 64
ROPE_THETA = 10000.0
DIL_PAIRS = ((128, 1), (512, 4), (2048, 16))
DIL_HPG = 4
DIL_HEADS = DIL_HPG * len(DIL_PAIRS)
DIL_HEAD_DIM = 128
DIL_BAND = 128
SSM_WIDTH = 512
SSM_GROUP_SIZE = 16
SSM_GROUPS = SSM_WIDTH // SSM_GROUP_SIZE
SSM_STATE = 64
N_BRANCH = 3
BRANCH_WIDTH = 512
DEEPNORM_ALPHA = (2 * DEPTH) ** 0.25
MACARON_WEIGHT = 0.5
LN_EPS = 1e-5
RMS_EPS = 1e-6

LANES = 128
V7X_VMEM_LIMIT_BYTES = 56 * 1024 * 1024

FF_CHUNK = 256
FFN_ROWS = 512
N_FF_CHUNKS = D_FF // FF_CHUNK
MLA_HEAD_PAD = 128
MLA_QK_WIDTH = MLA_HEADS * MLA_HEAD_PAD
MLA_V_WIDTH = MLA_HEADS * MLA_V_DIM
MLA_SCALE = (MLA_NOPE_DIM + MLA_ROPE_DIM) ** -0.5
DIL_SCALE = DIL_HEAD_DIM ** -0.5
DIL_GROUP_WIDTH = DIL_HPG * DIL_HEAD_DIM
MLA_A_WIDTH = 768
COL_Q1 = 1024
COL_U = COL_Q1 + 3 * DIL_GROUP_WIDTH
COL_GATES = COL_U + SSM_WIDTH
N_MAIN = COL_GATES + N_BRANCH * D_MODEL
PROJ_CHUNK = 512
PERM_TILE = 256
LOG2E = math.log2(math.e)
MLA_ROW_CHUNK = 64
MLA_HEADS_PER_STEP = 4
SSM_BLOCKS = 4
SSM_BLOCK_CH = SSM_WIDTH // SSM_BLOCKS
SSM_BLOCK_STATES = SSM_GROUPS * SSM_STATE // SSM_BLOCKS
SSM_SLABS = SSM_BLOCK_STATES // LANES
SSM_TT = 128
SSM_PITCH = 136

_F32 = jnp.float32
_BF16 = jnp.bfloat16
_NEG = -1e30


def _cparams(n_axes):
    return pltpu.CompilerParams(
        dimension_semantics=("arbitrary",) * n_axes,
        vmem_limit_bytes=V7X_VMEM_LIMIT_BYTES,
    )


def _resident(block_shape, index_map):
    return pl.BlockSpec(block_shape, index_map, pipeline_mode=pl.Buffered(1))


def _layer_norm(y, g, b):
    mu = jnp.mean(y, axis=-1, keepdims=True)
    yc = y - mu
    var = jnp.mean(yc * yc, axis=-1, keepdims=True)
    return yc * lax.rsqrt(var + LN_EPS) * g + b


def _dot(a, b):
    return jnp.dot(a, b, preferred_element_type=_F32)


def _dot_nt(a, b):
    return lax.dot_general(a, b, (((1,), (1,)), ((), ())), preferred_element_type=_F32)


def _ada_kernel(c_ref, w_ref, b_ref, o_ref):
    c = c_ref[...]
    cond = (c * jax.nn.sigmoid(c)).astype(_BF16)
    o_ref[0] = _dot(cond, w_ref[0].astype(_BF16)) + b_ref[0]


def _ada_call(c, w_ada, b_ada):
    depth, d, n = w_ada.shape
    bsz = c.shape[0]
    tn = 1536
    return pl.pallas_call(
        _ada_kernel,
        grid=(depth, n // tn),
        in_specs=[
            pl.BlockSpec((bsz, d), lambda l, j: (0, 0)),
            pl.BlockSpec((1, d, tn), lambda l, j: (l, 0, j)),
            pl.BlockSpec((1, 1, tn), lambda l, j: (l, 0, j)),
        ],
        out_specs=pl.BlockSpec((1, bsz, tn), lambda l, j: (l, 0, j)),
        out_shape=jax.ShapeDtypeStruct((depth, bsz, n), _F32),
        compiler_params=_cparams(2),
        name="ada",
    )(c, w_ada, b_ada.reshape(depth, 1, n))


def _rope_freq_lanes():
    half = MLA_ROPE_DIM // 2
    inv_freq = np.power(np.float32(ROPE_THETA), -np.arange(half, dtype=np.float32) / np.float32(half))
    f = np.zeros((1, MLA_HEAD_PAD), np.float32)
    f[0, MLA_NOPE_DIM:MLA_NOPE_DIM + half] = inv_freq
    f[0, MLA_NOPE_DIM + half:MLA_NOPE_DIM + 2 * half] = inv_freq
    return f


def _rope_kernel(pos_ref, f_ref, cos_ref, sin_ref):
    ang = pos_ref[0].astype(_F32) * f_ref[...]
    cos_ref[0] = jnp.cos(ang)
    sin_ref[0] = jnp.sin(ang)


def _rope_call(positions):
    bsz, s = positions.shape
    tm = min(s, 512)
    out = jax.ShapeDtypeStruct((bsz, s, MLA_HEAD_PAD), _F32)
    return pl.pallas_call(
        _rope_kernel,
        grid=(bsz, s // tm),
        in_specs=[
            pl.BlockSpec((1, tm, 1), lambda b, i: (b, i, 0)),
            pl.BlockSpec((1, MLA_HEAD_PAD), lambda b, i: (0, 0)),
        ],
        out_specs=[pl.BlockSpec((1, tm, MLA_HEAD_PAD), lambda b, i: (b, i, 0))] * 2,
        out_shape=[out, out],
        compiler_params=_cparams(2),
        name="rope_tables",
    )(positions.reshape(bsz, s, 1), jnp.asarray(_rope_freq_lanes()))


def _ffn_kernel(sub, h_ref, ada_ref, w1_ref, w3_ref, w2_ref, g_ref, b_ref, o_ref):
    shift = ada_ref[0, 3 * sub + 0:3 * sub + 1, :]
    scale = ada_ref[0, 3 * sub + 1:3 * sub + 2, :]
    gate = ada_ref[0, 3 * sub + 2:3 * sub + 3, :]
    for r0 in range(0, h_ref.shape[1], FFN_ROWS):
        rows = slice(r0, r0 + FFN_ROWS)
        h = h_ref[0, rows, :]
        xm = (h * (1.0 + scale) + shift).astype(_BF16)
        acc = jnp.zeros(h.shape, _F32)
        for j in range(N_FF_CHUNKS):
            cols = slice(j * FF_CHUNK, (j + 1) * FF_CHUNK)
            a = _dot(xm, w1_ref[:, cols])
            b = _dot(xm, w3_ref[:, cols])
            g = (a * jax.nn.sigmoid(a) * b).astype(_BF16)
            acc = acc + _dot(g, w2_ref[cols, :])
        y = DEEPNORM_ALPHA * h + MACARON_WEIGHT * gate * acc
        o_ref[0, rows, :] = _layer_norm(y, g_ref[...], b_ref[...])


def _ffn_call(h, ada, sub, w1, w3, w2, ln_g, ln_b):
    bsz, s, d = h.shape
    tm = min(s, 2 * FFN_ROWS)
    return pl.pallas_call(
        functools.partial(_ffn_kernel, sub),
        grid=(bsz, s // tm),
        in_specs=[
            pl.BlockSpec((1, tm, d), lambda b, i: (b, i, 0)),
            pl.BlockSpec((1, 9, d), lambda b, i: (b, 0, 0)),
            _resident((d, D_FF), lambda b, i: (0, 0)),
            _resident((d, D_FF), lambda b, i: (0, 0)),
            _resident((D_FF, d), lambda b, i: (0, 0)),
            pl.BlockSpec((1, d), lambda b, i: (0, 0)),
            pl.BlockSpec((1, d), lambda b, i: (0, 0)),
        ],
        out_specs=pl.BlockSpec((1, tm, d), lambda b, i: (b, i, 0)),
        out_shape=jax.ShapeDtypeStruct((bsz, s, d), _F32),
        compiler_params=_cparams(2),
        name="ffn_sublayer",
    )(h, ada, w1, w3, w2, ln_g, ln_b)


def _perm_matrix(dil):
    n = PERM_TILE // dil
    p = np.zeros((PERM_TILE, PERM_TILE), np.float32)
    for r in range(dil):
        for i in range(n):
            p[r * n + i, i * dil + r] = 1.0
    return p


def _proj_kernel(tm, dils, x_ref, ada_ref, wm_ref, bm_ref, *rest):
    ngrp = len(dils)
    grp_in = rest[:3 * ngrp]
    om_ref = rest[3 * ngrp]
    grp_out = rest[3 * ngrp + 1:]
    x = x_ref[0]
    shift = ada_ref[0, 3:4, :]
    scale = ada_ref[0, 4:5, :]
    xm = (x * (1.0 + scale) + shift).astype(_BF16)
    n_main = wm_ref.shape[1]
    for n0 in range(0, n_main, PROJ_CHUNK):
        n1 = min(n0 + PROJ_CHUNK, n_main)
        om_ref[0, :, n0:n1] = (_dot(xm, wm_ref[:, n0:n1]) + bm_ref[:, n0:n1]).astype(_BF16)
    for g, dil in enumerate(dils):
        w_ref, b_ref, p_ref = grp_in[3 * g:3 * g + 3]
        o_ref = grp_out[g]
        n = PERM_TILE // dil
        n_out = w_ref.shape[1]
        for t in range(tm // PERM_TILE):
            xp = _dot(p_ref[...], xm[t * PERM_TILE:(t + 1) * PERM_TILE]).astype(_BF16)
            for n0 in range(0, n_out, PROJ_CHUNK):
                n1 = min(n0 + PROJ_CHUNK, n_out)
                res = (_dot(xp, w_ref[:, n0:n1]) + b_ref[:, n0:n1]).astype(_BF16)
                for r in range(dil):
                    o_ref[0, r, t * n:(t + 1) * n, n0:n1] = res[r * n:(r + 1) * n]


def _proj_call(h, ada, w_main, b_main, groups):
    bsz, s, d = h.shape
    tm = min(s, 512)
    dils = tuple(g[0] for g in groups)
    const = lambda b, i: (0, 0)
    in_specs = [
        pl.BlockSpec((1, tm, d), lambda b, i: (b, i, 0)),
        pl.BlockSpec((1, 9, d), lambda b, i: (b, 0, 0)),
        _resident(w_main.shape, const),
        pl.BlockSpec(b_main.shape, const),
    ]
    args = [h, ada, w_main, b_main]
    out_specs = [pl.BlockSpec((1, tm, w_main.shape[1]), lambda b, i: (b, i, 0))]
    out_shape = [jax.ShapeDtypeStruct((bsz, s, w_main.shape[1]), _BF16)]
    for dil, w, bias in groups:
        in_specs += [_resident(w.shape, const), pl.BlockSpec(bias.shape, const),
                     pl.BlockSpec((PERM_TILE, PERM_TILE), const)]
        args += [w, bias, jnp.asarray(_perm_matrix(dil), _BF16)]
        out_specs.append(pl.BlockSpec((1, dil, tm // dil, w.shape[1]), lambda b, i: (b, 0, i, 0)))
        out_shape.append(jax.ShapeDtypeStruct((bsz, dil, s // dil, w.shape[1]), _BF16))
    return pl.pallas_call(
        functools.partial(_proj_kernel, tm, dils),
        grid=(bsz, s // tm),
        in_specs=in_specs,
        out_specs=out_specs,
        out_shape=out_shape,
        compiler_params=_cparams(2),
        name="in_proj",
    )(*args)


def _mla_prep_kernel(a_ref, cos_ref, sin_ref, qg_ref, kvg_ref, wq_ref, wqr_ref, wk_ref, wv_ref,
                     pk_ref, pkr_ref, ones_ref, q_ref, k_ref, v_ref):
    a = a_ref[0]
    qa = a[:, :MLA_Q_RANK].astype(_F32)
    kva = a[:, MLA_Q_RANK:MLA_Q_RANK + MLA_KV_RANK].astype(_F32)
    kr = a[:, MLA_Q_RANK + MLA_KV_RANK:]
    qn = (qa * lax.rsqrt(jnp.mean(qa * qa, axis=-1, keepdims=True) + RMS_EPS) * qg_ref[...]).astype(_BF16)
    kvn = (kva * lax.rsqrt(jnp.mean(kva * kva, axis=-1, keepdims=True) + RMS_EPS) * kvg_ref[...]).astype(_BF16)
    cos = cos_ref[0]
    sin = sin_ref[0]
    q = _dot(qn, wq_ref[...])
    qrot = _dot(qn, wqr_ref[...])
    k = _dot(kvn, wk_ref[...]) + _dot(kr, pk_ref[...])
    krot = _dot(kr, pkr_ref[...])
    for hd in range(MLA_HEADS):
        sl = slice(hd * MLA_HEAD_PAD, (hd + 1) * MLA_HEAD_PAD)
        q_ref[0, :, sl] = ((q[:, sl] * cos + qrot[:, sl] * sin) * (MLA_SCALE * LOG2E)).astype(_BF16)
        k_ref[0, :, sl] = (k[:, sl] * cos + krot[:, sl] * sin).astype(_BF16)
    v_ref[0] = (_dot(kvn, wv_ref[...]) + ones_ref[...]).astype(_BF16)


def _mla_prep_call(proj_main, cos, sin, qg, kvg, wq, wqr, wk, wv, pk, pkr, ones_row):
    bsz, s, _ = proj_main.shape
    tm = min(s, 512)
    const = lambda b, i: (0, 0)
    return pl.pallas_call(
        _mla_prep_kernel,
        grid=(bsz, s // tm),
        in_specs=[
            pl.BlockSpec((1, tm, MLA_A_WIDTH), lambda b, i: (b, i, 0)),
            pl.BlockSpec((1, tm, MLA_HEAD_PAD), lambda b, i: (b, i, 0)),
            pl.BlockSpec((1, tm, MLA_HEAD_PAD), lambda b, i: (b, i, 0)),
            pl.BlockSpec((1, MLA_Q_RANK), const),
            pl.BlockSpec((1, MLA_KV_RANK), const),
            pl.BlockSpec((MLA_Q_RANK, MLA_QK_WIDTH), const),
            pl.BlockSpec((MLA_Q_RANK, MLA_QK_WIDTH), const),
            pl.BlockSpec((MLA_KV_RANK, MLA_QK_WIDTH), const),
            pl.BlockSpec((MLA_KV_RANK, MLA_QK_WIDTH), const),
            pl.BlockSpec((LANES, MLA_QK_WIDTH), const),
            pl.BlockSpec((LANES, MLA_QK_WIDTH), const),
            pl.BlockSpec((1, MLA_QK_WIDTH), const),
        ],
        out_specs=[pl.BlockSpec((1, tm, MLA_QK_WIDTH), lambda b, i: (b, i, 0))] * 3,
        out_shape=[jax.ShapeDtypeStruct((bsz, s, MLA_QK_WIDTH), _BF16)] * 3,
        compiler_params=_cparams(2),
        name="mla_prep",
    )(proj_main, cos, sin, qg, kvg, wq, wqr, wk, wv, pk, pkr, ones_row)


def _mla_attn_kernel(tq, q_ref, k_ref, v_ref, o_ref, s_ref, p_ref):
    nh = MLA_HEADS_PER_STEP
    half = tq // 2
    qi = pl.program_id(2)
    row = lax.broadcasted_iota(jnp.int32, (tq, tq), 0)
    col = lax.broadcasted_iota(jnp.int32, (tq, tq), 1)
    causal = col <= row
    lanes = [slice(e * MLA_HEAD_PAD, (e + 1) * MLA_HEAD_PAD) for e in range(nh)]
    qs = [q_ref[0, :, lanes[e]] for e in range(nh)]

    def scores(j, slot):
        start = pl.multiple_of(j * tq, tq)
        for e in range(nh):
            s_ref[slot, e] = _dot_nt(qs[e], k_ref[0, pl.ds(start, tq), lanes[e]])

    def softmax_pv(j, slot, carry, masked):
        start = pl.multiple_of(j * tq, tq)
        new = []
        for e in range(nh):
            m, acc = carry[e]
            m_parts = []
            for c in range(tq // MLA_ROW_CHUNK):
                rows = slice(c * MLA_ROW_CHUNK, (c + 1) * MLA_ROW_CHUNK)
                cols = slice(0, half if masked and (c + 1) * MLA_ROW_CHUNK <= half else tq)
                s = s_ref[slot, e, rows, cols]
                if masked:
                    s = jnp.where(causal[rows, cols], s, _NEG)
                m_c = jnp.maximum(m[rows], jnp.max(s, axis=-1, keepdims=True))
                p_ref[slot, e, rows, cols] = jnp.exp2(s - m_c).astype(_BF16)
                m_parts.append(m_c)
            m_new = jnp.concatenate(m_parts, axis=0)
            if masked:
                pv = jnp.concatenate([
                    _dot(p_ref[slot, e, :half, :half], v_ref[0, pl.ds(start, half), lanes[e]]),
                    _dot(p_ref[slot, e, half:, :], v_ref[0, pl.ds(start, tq), lanes[e]])], axis=0)
            else:
                pv = _dot(p_ref[slot, e], v_ref[0, pl.ds(start, tq), lanes[e]])
            acc = jnp.exp2(m - m_new) * acc + pv
            new.append((m_new, acc))
        return tuple(new)

    def pair(jj, carry):
        t = 2 * jj
        scores(t + 1, 1)
        carry = softmax_pv(t, 0, carry, False)
        scores(t + 2, 0)
        return softmax_pv(t + 1, 1, carry, False)

    def tail_odd(carry):
        scores(qi, 1)
        carry = softmax_pv(qi - 1, 0, carry, False)
        return softmax_pv(qi, 1, carry, True)

    def tail_even(carry):
        return softmax_pv(qi, 0, carry, True)

    init = tuple((jnp.full((tq, 1), _NEG, _F32), jnp.zeros((tq, MLA_HEAD_PAD), _F32)) for _ in range(nh))
    scores(0, 0)
    carry = lax.fori_loop(0, qi // 2, pair, init)
    carry = lax.cond(qi % 2 == 1, tail_odd, tail_even, carry)
    lane = lax.broadcasted_iota(jnp.int32, (tq, MLA_HEAD_PAD), 1)
    for e2 in range(nh // 2):
        acc0, acc1 = carry[2 * e2][1], carry[2 * e2 + 1][1]
        out0 = acc0 * (1.0 / acc0[:, MLA_V_DIM:MLA_V_DIM + 1])
        out1 = acc1 * (1.0 / acc1[:, 0:1])
        o_ref[0, :, lanes[e2]] = jnp.where(lane < MLA_V_DIM, out0, out1).astype(_BF16)


def _mla_attn_call(q, k, v):
    bsz, s, _ = q.shape
    tq = min(s, 512)
    nh = MLA_HEADS_PER_STEP
    qk_w = nh * MLA_HEAD_PAD
    return pl.pallas_call(
        functools.partial(_mla_attn_kernel, tq),
        grid=(bsz, MLA_HEADS // nh, s // tq),
        in_specs=[
            pl.BlockSpec((1, tq, qk_w), lambda b, h, i: (b, i, h)),
            pl.BlockSpec((1, s, qk_w), lambda b, h, i: (b, 0, h)),
            pl.BlockSpec((1, s, qk_w), lambda b, h, i: (b, 0, h)),
        ],
        out_specs=pl.BlockSpec((1, tq, nh * MLA_V_DIM), lambda b, h, i: (b, i, h)),
        out_shape=jax.ShapeDtypeStruct((bsz, s, MLA_V_WIDTH), _BF16),
        scratch_shapes=[pltpu.VMEM((2, nh, tq, tq), _F32), pltpu.VMEM((2, nh, tq, tq), _BF16)],
        compiler_params=_cparams(3),
        name="mla_attention",
    )(q, k, v)


def _alibi_slope(head):
    return float(np.exp2(np.float32(-8.0) * (np.float32(head) + np.float32(1.0)) / np.float32(DIL_HEADS)))


def _dil_attn_kernel(tq, kw, single_tile, nres, group, dil, q_ref, k_ref, v_ref, o_ref, lse_ref, bias_ref):
    qi = pl.program_id(2)
    first_step = (pl.program_id(0) == 0) & (pl.program_id(1) == 0) & (qi == 0)
    nb = DIL_BAND

    @pl.when(first_step)
    def _():
        row = lax.broadcasted_iota(jnp.int32, (nb, kw), 0)
        col = lax.broadcasted_iota(jnp.int32, (nb, kw), 1)
        for variant, rel in ((0, nb + row - col), (1, row - col)):
            ok = (rel >= 0) & (rel <= DIL_BAND)
            relf = rel.astype(_F32)
            for i in range(DIL_HPG):
                slope = _alibi_slope(group * DIL_HPG + i) * dil * LOG2E
                bias_ref[i, variant] = jnp.where(ok, -slope * relf, _NEG)

    q0 = pl.multiple_of(qi * tq, tq)
    lane = lax.broadcasted_iota(jnp.int32, (nb, LANES), 1)
    for rr in range(nres):
        for sb in range(tq // nb):
            rows = slice(sb * nb, (sb + 1) * nb)
            if sb == 0:
                start = pl.multiple_of(jnp.maximum(q0 - nb, 0), nb)
            else:
                start = pl.multiple_of(q0 + (sb - 1) * nb, nb)
            m_all = jnp.zeros((nb, LANES), _F32)
            l_all = jnp.ones((nb, LANES), _F32)
            for i in range(DIL_HPG):
                lanes = slice(i * DIL_HEAD_DIM, (i + 1) * DIL_HEAD_DIM)
                if sb > 0:
                    bias = bias_ref[i, 0]
                elif single_tile:
                    bias = bias_ref[i, 1]
                else:
                    bias = jnp.where(qi == 0, bias_ref[i, 1], bias_ref[i, 0])
                s = _dot_nt(q_ref[0, rr, rows, lanes], k_ref[0, rr, pl.ds(start, kw), lanes]) + bias
                m = jnp.max(s, axis=-1, keepdims=True)
                e = jnp.exp2(s - m)
                l = jnp.sum(e, axis=-1, keepdims=True)
                o = _dot(e.astype(_BF16), v_ref[0, rr, pl.ds(start, kw), lanes])
                o_ref[0, rr, rows, lanes] = (o * (1.0 / l)).astype(_BF16)
                m_all = jnp.where(lane == i, m, m_all)
                l_all = jnp.where(lane == i, l, l_all)
            lse_ref[0, rr, rows, :] = (m_all + jnp.log2(l_all)) * (1.0 / LOG2E)


def _dil_attn_call(arr, col_block0, group, dil, tq, nres):
    bsz, _, length, _ = arr.shape
    w = DIL_GROUP_WIDTH
    tq = min(tq, length)
    kw = min(2 * DIL_BAND, length)
    nres = min(nres, dil)
    return pl.pallas_call(
        functools.partial(_dil_attn_kernel, tq, kw, length == tq, nres, group, dil),
        grid=(bsz, dil // nres, length // tq),
        in_specs=[
            pl.BlockSpec((1, nres, tq, w), lambda b, r, i: (b, r, i, col_block0)),
            pl.BlockSpec((1, nres, length, w), lambda b, r, i: (b, r, 0, col_block0 + 1)),
            pl.BlockSpec((1, nres, length, w), lambda b, r, i: (b, r, 0, col_block0 + 2)),
        ],
        out_specs=[
            pl.BlockSpec((1, nres, tq, w), lambda b, r, i: (b, r, i, 0)),
            pl.BlockSpec((1, nres, tq, LANES), lambda b, r, i: (b, r, i, 0)),
        ],
        out_shape=[
            jax.ShapeDtypeStruct((bsz, dil, length, w), _BF16),
            jax.ShapeDtypeStruct((bsz, dil, length, LANES), _F32),
        ],
        scratch_shapes=[pltpu.VMEM((DIL_HPG, 2, DIL_BAND, kw), _F32)],
        compiler_params=_cparams(3),
        name=f"dil_attention_g{group}",
    )(arr, arr, arr)


def _ssm_kernel(nb, u_ref, bw_ref, cw_ref, are_ref, aim_ref, d_ref, o_ref, bx_ref, st_ref):
    ti = pl.program_id(1)

    @pl.when(ti == 0)
    def _():
        st_ref[...] = jnp.zeros(st_ref.shape, _F32)

    for b in range(nb):
        bu = _dot(u_ref[b, 0], bw_ref[0])
        for sidx in range(2 * SSM_SLABS):
            bx_ref[sidx, b * SSM_PITCH:b * SSM_PITCH + SSM_TT, :] = bu[:, sidx * LANES:(sidx + 1) * LANES]

    a_re = [jnp.broadcast_to(are_ref[0, :, k * LANES:(k + 1) * LANES], (nb, LANES)) for k in range(SSM_SLABS)]
    a_im = [jnp.broadcast_to(aim_ref[0, :, k * LANES:(k + 1) * LANES], (nb, LANES)) for k in range(SSM_SLABS)]

    def step(t, carry):
        xr, xi = carry
        nr, ni = [], []
        for k in range(SSM_SLABS):
            rows = pl.ds(t, nb, stride=SSM_PITCH)
            br = bx_ref[k, rows, :]
            bi = bx_ref[SSM_SLABS + k, rows, :]
            r = a_re[k] * xr[k] - a_im[k] * xi[k] + br
            i = a_re[k] * xi[k] + a_im[k] * xr[k] + bi
            bx_ref[k, rows, :] = r
            bx_ref[SSM_SLABS + k, rows, :] = i
            nr.append(r)
            ni.append(i)
        return tuple(nr), tuple(ni)

    x0 = (tuple(st_ref[k] for k in range(SSM_SLABS)),
          tuple(st_ref[SSM_SLABS + k] for k in range(SSM_SLABS)))
    xr, xi = lax.fori_loop(0, SSM_TT, step, x0, unroll=8)
    for k in range(SSM_SLABS):
        st_ref[k] = xr[k]
        st_ref[SSM_SLABS + k] = xi[k]

    dskip = d_ref[0]
    for b in range(nb):
        xs = jnp.concatenate(
            [bx_ref[sidx, b * SSM_PITCH:b * SSM_PITCH + SSM_TT, :] for sidx in range(2 * SSM_SLABS)],
            axis=1).astype(_BF16)
        y = _dot(xs, cw_ref[0]) + dskip * u_ref[b, 0].astype(_F32)
        o_ref[b] = jax.nn.gelu(y).astype(_BF16)


def _ssm_call(proj_main, bw, cw, a_re, a_im, d_skip):
    bsz, _, s, _ = proj_main.shape
    u_block0 = COL_U // SSM_BLOCK_CH
    nstate = 2 * SSM_BLOCK_STATES
    return pl.pallas_call(
        functools.partial(_ssm_kernel, bsz),
        grid=(SSM_BLOCKS, s // SSM_TT),
        in_specs=[
            pl.BlockSpec((bsz, 1, SSM_TT, SSM_BLOCK_CH), lambda m, t: (0, 0, t, u_block0 + m)),
            pl.BlockSpec((1, SSM_BLOCK_CH, nstate), lambda m, t: (m, 0, 0)),
            pl.BlockSpec((1, nstate, SSM_BLOCK_CH), lambda m, t: (m, 0, 0)),
            pl.BlockSpec((1, 1, SSM_BLOCK_STATES), lambda m, t: (m, 0, 0)),
            pl.BlockSpec((1, 1, SSM_BLOCK_STATES), lambda m, t: (m, 0, 0)),
            pl.BlockSpec((1, 1, SSM_BLOCK_CH), lambda m, t: (m, 0, 0)),
        ],
        out_specs=pl.BlockSpec((bsz, SSM_TT, SSM_BLOCK_CH), lambda m, t: (0, t, m)),
        out_shape=jax.ShapeDtypeStruct((bsz, s, SSM_WIDTH), _BF16),
        scratch_shapes=[
            pltpu.VMEM((2 * SSM_SLABS, bsz * SSM_PITCH, LANES), _F32),
            pltpu.VMEM((2 * SSM_SLABS, bsz, LANES), _F32),
        ],
        compiler_params=_cparams(2),
        name="s5_scan",
    )(proj_main, bw, cw, a_re, a_im, d_skip)


def _ssm_weights(lam_re, lam_im, log_dt, b_re, b_im, c_re, c_im, d_skip):
    lam = lax.complex(lam_re.astype(_F32), lam_im.astype(_F32))
    dt = jnp.exp(log_dt.astype(_F32))[:, None]
    lam_bar = jnp.exp(lam * dt)
    b_bar = ((lam_bar - 1.0) / lam)[..., None] * lax.complex(b_re.astype(_F32), b_im.astype(_F32))
    gpb = SSM_GROUPS // SSM_BLOCKS
    eye = jnp.eye(gpb, dtype=_F32)

    def in_weights(part):
        w = part.reshape(SSM_BLOCKS, gpb, SSM_STATE, SSM_GROUP_SIZE)
        w = jnp.einsum("mgph,gk->mghkp", w, eye)
        return w.reshape(SSM_BLOCKS, SSM_BLOCK_CH, SSM_BLOCK_STATES)

    def out_weights(part):
        w = part.reshape(SSM_BLOCKS, gpb, SSM_GROUP_SIZE, SSM_STATE)
        w = jnp.einsum("mgkp,gj->mgpjk", w, eye)
        return w.reshape(SSM_BLOCKS, SSM_BLOCK_STATES, SSM_BLOCK_CH)

    bw = jnp.concatenate([in_weights(jnp.real(b_bar)), in_weights(jnp.imag(b_bar))], axis=2).astype(_BF16)
    cw = jnp.concatenate([out_weights(c_re.astype(_F32)), -out_weights(c_im.astype(_F32))], axis=1).astype(_BF16)
    a_re = jnp.real(lam_bar).reshape(SSM_BLOCKS, 1, SSM_BLOCK_STATES)
    a_im = jnp.imag(lam_bar).reshape(SSM_BLOCKS, 1, SSM_BLOCK_STATES)
    return bw, cw, a_re, a_im, d_skip.astype(_F32).reshape(SSM_BLOCKS, 1, SSM_BLOCK_CH)


def _to_token_order(ref, t, pt_ref):
    _, dil, _, width = ref.shape
    n = PERM_TILE // dil
    x = ref[0, :, t * n:(t + 1) * n, :].reshape(PERM_TILE, width)
    if x.dtype == _BF16:
        return _dot(pt_ref[...], x)
    hi = x.astype(_BF16)
    r1 = x - hi.astype(_F32)
    mid = r1.astype(_BF16)
    lo = (r1 - mid.astype(_F32)).astype(_BF16)
    res = _dot(pt_ref[...], jnp.concatenate([hi, mid, lo], axis=1))
    return (res[:, :width] + res[:, width:2 * width]) + res[:, 2 * width:]


def _merge_kernel(h_ref, ada_ref, ymla_ref, o0_ref, o1_ref, o2_ref, l0_ref, l1_ref, l2_ref, pt1_ref, pt2_ref,
                  ys_ref, gl_ref, wglu_ref, bglu_ref, wbr_ref, wout_ref, g_ref, b_ref, out_ref):
    gate = ada_ref[0, 5:6, :]
    d = h_ref.shape[-1]
    for t in range(h_ref.shape[1] // PERM_TILE):
        rows = slice(t * PERM_TILE, (t + 1) * PERM_TILE)
        h = h_ref[0, rows, :]
        o0 = o0_ref[0, 0, rows, :].astype(_F32)
        o1 = _to_token_order(o1_ref, t, pt1_ref)
        o2 = _to_token_order(o2_ref, t, pt2_ref)
        l0 = l0_ref[0, 0, rows, :]
        l1 = _to_token_order(l1_ref, t, pt1_ref)
        l2 = _to_token_order(l2_ref, t, pt2_ref)
        m = jnp.maximum(jnp.maximum(l0, l1), l2)
        e0, e1, e2 = jnp.exp(l0 - m), jnp.exp(l1 - m), jnp.exp(l2 - m)
        inv = 1.0 / (e0 + e1 + e2)
        w0, w1, w2 = e0 * inv, e1 * inv, e2 * inv
        heads = []
        for i in range(DIL_HPG):
            lanes = slice(i * DIL_HEAD_DIM, (i + 1) * DIL_HEAD_DIM)
            heads.append(w0[:, i:i + 1] * o0[:, lanes] + w1[:, i:i + 1] * o1[:, lanes]
                         + w2[:, i:i + 1] * o2[:, lanes])
        y_dil = jnp.concatenate(heads, axis=1).astype(_BF16)
        ys = ys_ref[0, rows, :]
        glu = jax.nn.sigmoid(_dot(ys, wglu_ref[...]) + bglu_ref[...])
        y_ssm = (ys.astype(_F32) * glu).astype(_BF16)
        merged = (jax.nn.sigmoid(gl_ref[0, rows, 0:d].astype(_F32)) * _dot(ymla_ref[0, rows, :], wbr_ref[0])
                  + jax.nn.sigmoid(gl_ref[0, rows, d:2 * d].astype(_F32)) * _dot(y_dil, wbr_ref[1])
                  + jax.nn.sigmoid(gl_ref[0, rows, 2 * d:3 * d].astype(_F32)) * _dot(y_ssm, wbr_ref[2]))
        out = _dot(merged.astype(_BF16), wout_ref[...])
        y = DEEPNORM_ALPHA * h + gate * out
        out_ref[0, rows, :] = _layer_norm(y, g_ref[...], b_ref[...])


def _merge_call(h, ada, y_mla, dil_o, dil_lse, y_s, proj_main, wglu, bglu, wbr, wout, ln_g, ln_b):
    bsz, s, d = h.shape
    tm = 2 * PERM_TILE
    tok = lambda b, i: (b, i, 0)
    grp = lambda b, i: (b, 0, i, 0)
    const2 = lambda b, i: (0, 0)
    bw = BRANCH_WIDTH
    d1, d2 = DIL_PAIRS[1][1], DIL_PAIRS[2][1]
    pt1 = jnp.asarray(_perm_matrix(d1).T, _BF16)
    pt2 = jnp.asarray(_perm_matrix(d2).T, _BF16)
    return pl.pallas_call(
        _merge_kernel,
        grid=(bsz, s // tm),
        in_specs=[
            pl.BlockSpec((1, tm, d), tok),
            pl.BlockSpec((1, 9, d), lambda b, i: (b, 0, 0)),
            pl.BlockSpec((1, tm, bw), tok),
            pl.BlockSpec((1, 1, tm, bw), grp),
            pl.BlockSpec((1, d1, tm // d1, bw), grp),
            pl.BlockSpec((1, d2, tm // d2, bw), grp),
            pl.BlockSpec((1, 1, tm, LANES), grp),
            pl.BlockSpec((1, d1, tm // d1, LANES), grp),
            pl.BlockSpec((1, d2, tm // d2, LANES), grp),
            pl.BlockSpec((PERM_TILE, PERM_TILE), const2),
            pl.BlockSpec((PERM_TILE, PERM_TILE), const2),
            pl.BlockSpec((1, tm, bw), tok),
            pl.BlockSpec((1, tm, N_BRANCH * d), lambda b, i: (b, i, COL_GATES // (N_BRANCH * d))),
            pl.BlockSpec((bw, bw), const2),
            pl.BlockSpec((1, bw), const2),
            pl.BlockSpec((N_BRANCH, bw, d), lambda b, i: (0, 0, 0)),
            pl.BlockSpec((d, d), const2),
            pl.BlockSpec((1, d), const2),
            pl.BlockSpec((1, d), const2),
        ],
        out_specs=pl.BlockSpec((1, tm, d), tok),
        out_shape=jax.ShapeDtypeStruct((bsz, s, d), _F32),
        compiler_params=_cparams(2),
        name="merge_sublayer",
    )(h, ada, y_mla, dil_o[0], dil_o[1], dil_o[2], dil_lse[0], dil_lse[1], dil_lse[2], pt1, pt2, y_s,
      proj_main, wglu, bglu, wbr, wout, ln_g, ln_b)


def _in_proj_weights(w_in, b_in):
    d = w_in.shape[0]
    o_kr = MLA_Q_RANK + MLA_KV_RANK
    o_dil = o_kr + MLA_ROPE_DIM
    o_u = o_dil + 3 * DIL_HEADS * DIL_HEAD_DIM
    o_g = o_u + SSM_WIDTH

    def dil_cols(arr, part, group):
        start = o_dil + part * DIL_HEADS * DIL_HEAD_DIM + group * DIL_GROUP_WIDTH
        cols = arr[..., start:start + DIL_GROUP_WIDTH]
        return cols * (DIL_SCALE * LOG2E) if part == 0 else cols

    def main(arr):
        pad = jnp.zeros(arr.shape[:-1] + (COL_Q1 - o_dil,), arr.dtype)
        return jnp.concatenate([arr[..., :o_dil], pad, dil_cols(arr, 0, 0), dil_cols(arr, 1, 0),
                                dil_cols(arr, 2, 0), arr[..., o_u:o_g], arr[..., o_g:]], axis=-1)

    def group(arr, g):
        return jnp.concatenate([dil_cols(arr, 0, g), dil_cols(arr, 1, g), dil_cols(arr, 2, g)], axis=-1)

    b2 = b_in.astype(_F32).reshape(1, -1)
    return ((main(w_in).astype(_BF16), main(b2)),
            (group(w_in, 1).astype(_BF16), group(b2, 1)),
            (group(w_in, 2).astype(_BF16), group(b2, 2)))


def _mla_weights(w_qb, w_kvb):
    qd = MLA_NOPE_DIM + MLA_ROPE_DIM
    half = MLA_ROPE_DIM // 2
    wq = w_qb.reshape(MLA_Q_RANK, MLA_HEADS, qd)
    zeros = jnp.zeros((MLA_Q_RANK, MLA_HEADS, MLA_HEAD_PAD - qd), w_qb.dtype)
    zn = jnp.zeros((MLA_Q_RANK, MLA_HEADS, MLA_NOPE_DIM), w_qb.dtype)
    wq_pad = jnp.concatenate([wq, zeros], axis=-1).reshape(MLA_Q_RANK, MLA_QK_WIDTH)
    t1 = wq[..., MLA_NOPE_DIM:MLA_NOPE_DIM + half]
    t2 = wq[..., MLA_NOPE_DIM + half:]
    wq_rot = jnp.concatenate([zn, -t2, t1, zeros], axis=-1).reshape(MLA_Q_RANK, MLA_QK_WIDTH)
    wkv = w_kvb.reshape(MLA_KV_RANK, MLA_HEADS, MLA_NOPE_DIM + MLA_V_DIM)
    zk = jnp.zeros((MLA_KV_RANK, MLA_HEADS, MLA_HEAD_PAD - MLA_NOPE_DIM), w_kvb.dtype)
    wk = jnp.concatenate([wkv[..., :MLA_NOPE_DIM], zk], axis=-1).reshape(MLA_KV_RANK, MLA_QK_WIDTH)
    wv = wkv[..., MLA_NOPE_DIM:].reshape(MLA_KV_RANK, MLA_HEADS // 2, 2, MLA_V_DIM)
    zv = jnp.zeros_like(wv[:, :, 0])
    wv_pad = jnp.concatenate([wv[:, :, 0], zv, zv, wv[:, :, 1]], axis=-1).reshape(MLA_KV_RANK, MLA_QK_WIDTH)
    return wq_pad.astype(_BF16), wq_rot.astype(_BF16), wk.astype(_BF16), wv_pad.astype(_BF16)


def _mla_ones_row():
    ones = np.zeros((1, MLA_QK_WIDTH), np.float32)
    for hd in range(MLA_HEADS):
        ones[0, hd * MLA_HEAD_PAD + (MLA_V_DIM if hd % 2 == 0 else 0)] = 1.0
    return jnp.asarray(ones)


def _rope_key_placement():
    half = MLA_ROPE_DIM // 2
    pk = np.zeros((LANES, MLA_QK_WIDTH), np.float32)
    pkr = np.zeros((LANES, MLA_QK_WIDTH), np.float32)
    for hd in range(MLA_HEADS):
        base = hd * MLA_HEAD_PAD + MLA_NOPE_DIM
        for i in range(MLA_ROPE_DIM):
            pk[i, base + i] = 1.0
        for i in range(half):
            pkr[half + i, base + i] = -1.0
            pkr[i, base + half + i] = 1.0
    return jnp.asarray(pk, _BF16), jnp.asarray(pkr, _BF16)


def _ffn_weights(w1, w3, w2):
    return w1.astype(_BF16), w3.astype(_BF16), w2.astype(_BF16)


def kernel(x, c, positions, w_ada, b_ada, ln_g, ln_b, ffn_w1, ffn_w3, ffn_w2, w_in, b_in, mla_q_norm, mla_kv_norm, mla_w_qb, mla_w_kvb, ssm_lambda_re, ssm_lambda_im, ssm_log_dt, ssm_b_re, ssm_b_im, ssm_c_re, ssm_c_im, ssm_d, ssm_w_glu, ssm_b_glu, w_br, w_out):
    bsz, s, d = x.shape
    assert d == D_MODEL and s % (DIL_PAIRS[2][1] * DIL_BAND) == 0, x.shape
    ada_all = _ada_call(c, w_ada, b_ada).reshape(DEPTH, bsz, 9, d)
    cos, sin = _rope_call(positions)
    pk, pkr = _rope_key_placement()
    ones_row = _mla_ones_row()
    h = x
    for l in range(DEPTH):
        ada = ada_all[l]
        h = _ffn_call(h, ada, 0, *_ffn_weights(ffn_w1[l, 0], ffn_w3[l, 0], ffn_w2[l, 0]),
                      ln_g[l, 0].reshape(1, d), ln_b[l, 0].reshape(1, d))
        (w_main, b_main), (w_g1, b_g1), (w_g2, b_g2) = _in_proj_weights(w_in[l], b_in[l])
        proj_main, proj_g1, proj_g2 = _proj_call(
            h, ada, w_main, b_main, [(DIL_PAIRS[1][1], w_g1, b_g1), (DIL_PAIRS[2][1], w_g2, b_g2)])
        proj_main4 = proj_main.reshape(bsz, 1, s, N_MAIN)
        wq, wqr, wk, wv = _mla_weights(mla_w_qb[l], mla_w_kvb[l])
        q, k, v = _mla_prep_call(proj_main, cos, sin, mla_q_norm[l].reshape(1, -1),
                                 mla_kv_norm[l].reshape(1, -1), wq, wqr, wk, wv, pk, pkr, ones_row)
        y_mla = _mla_attn_call(q, k, v)
        o0, lse0 = _dil_attn_call(proj_main4, COL_Q1 // DIL_GROUP_WIDTH, 0, DIL_PAIRS[0][1], 512, 1)
        o1, lse1 = _dil_attn_call(proj_g1, 0, 1, DIL_PAIRS[1][1], 512, 1)
        o2, lse2 = _dil_attn_call(proj_g2, 0, 2, DIL_PAIRS[2][1], 256, 4)
        y_s = _ssm_call(proj_main4, *_ssm_weights(ssm_lambda_re[l], ssm_lambda_im[l], ssm_log_dt[l],
                                                 ssm_b_re[l], ssm_b_im[l], ssm_c_re[l], ssm_c_im[l], ssm_d[l]))
        h = _merge_call(h, ada, y_mla, (o0, o1, o2), (lse0, lse1, lse2), y_s, proj_main,
                        ssm_w_glu[l].astype(_BF16), ssm_b_glu[l].reshape(1, -1), w_br[l].astype(_BF16),
                        w_out[l].astype(_BF16), ln_g[l, 1].reshape(1, d), ln_b[l, 1].reshape(1, d))
        h = _ffn_call(h, ada, 2, *_ffn_weights(ffn_w1[l, 1], ffn_w3[l, 1], ffn_w2[l, 1]),
                      ln_g[l, 2].reshape(1, d), ln_b[l, 2].reshape(1, d))
    return h
```

```python
import functools
import math

import numpy as np
import jax
import jax.numpy as jnp
from jax import lax
from jax.experimental import pallas as pl
from jax.experimental.pallas import tpu as pltpu

D_MODEL = 1024
DEPTH = 4
D_FF = 2816
MLA_HEADS = 8
MLA_Q_RANK = 384
MLA_KV_RANK = 256
MLA_NOPE_DIM = 64
MLA_ROPE_DIM = 32
MLA_V_DIM = 64
ROPE_THETA = 10000.0
DIL_PAIRS = ((128, 1), (512, 4), (2048, 16))
DIL_HPG = 4
DIL_HEADS = DIL_HPG * len(DIL_PAIRS)
DIL_HEAD_DIM = 128
DIL_BAND = 128
SSM_WIDTH = 512
SSM_GROUP_SIZE = 16
SSM_GROUPS = SSM_WIDTH // SSM_GROUP_SIZE
SSM_STATE = 64
N_BRANCH = 3
BRANCH_WIDTH = 512
DEEPNORM_ALPHA = (2 * DEPTH) ** 0.25
MACARON_WEIGHT = 0.5
LN_EPS = 1e-5
RMS_EPS = 1e-6

LANES = 128
V7X_VMEM_LIMIT_BYTES = 56 * 1024 * 1024

FF_CHUNK = 256
FFN_ROWS = 512
N_FF_CHUNKS = D_FF // FF_CHUNK
MLA_HEAD_PAD = 128
MLA_QK_WIDTH = MLA_HEADS * MLA_HEAD_PAD
MLA_V_WIDTH = MLA_HEADS * MLA_V_DIM
MLA_SCALE = (MLA_NOPE_DIM + MLA_ROPE_DIM) ** -0.5
DIL_SCALE = DIL_HEAD_DIM ** -0.5
DIL_GROUP_WIDTH = DIL_HPG * DIL_HEAD_DIM
MLA_A_WIDTH = 768
COL_Q1 = 1024
COL_U = COL_Q1 + 3 * DIL_GROUP_WIDTH
COL_GATES = COL_U + SSM_WIDTH
N_MAIN = COL_GATES + N_BRANCH * D_MODEL
PROJ_CHUNK = 512
PERM_TILE = 256
LOG2E = math.log2(math.e)
MLA_ROW_CHUNK = 64
MLA_HEADS_PER_STEP = 4
SSM_BLOCKS = 4
SSM_BLOCK_CH = SSM_WIDTH // SSM_BLOCKS
SSM_BLOCK_STATES = SSM_GROUPS * SSM_STATE // SSM_BLOCKS
SSM_SLABS = SSM_BLOCK_STATES // LANES
SSM_TT = 128
SSM_PITCH = 136

_F32 = jnp.float32
_BF16 = jnp.bfloat16
_NEG = -1e30


def _cparams(n_axes):
    return pltpu.CompilerParams(
        dimension_semantics=("arbitrary",) * n_axes,
        vmem_limit_bytes=V7X_VMEM_LIMIT_BYTES,
    )


def _resident(block_shape, index_map):
    return pl.BlockSpec(block_shape, index_map, pipeline_mode=pl.Buffered(1))


def _layer_norm(y, g, b):
    mu = jnp.mean(y, axis=-1, keepdims=True)
    yc = y - mu
    var = jnp.mean(yc * yc, axis=-1, keepdims=True)
    return yc * lax.rsqrt(var + LN_EPS) * g + b


def _dot(a, b):
    return jnp.dot(a, b, preferred_element_type=_F32)


def _dot_nt(a, b):
    return lax.dot_general(a, b, (((1,), (1,)), ((), ())), preferred_element_type=_F32)


def _ada_kernel(c_ref, w_ref, b_ref, o_ref):
    c = c_ref[...]
    cond = (c * jax.nn.sigmoid(c)).astype(_BF16)
    o_ref[0] = _dot(cond, w_ref[0].astype(_BF16)) + b_ref[0]


def _ada_call(c, w_ada, b_ada):
    depth, d, n = w_ada.shape
    bsz = c.shape[0]
    tn = 1536
    return pl.pallas_call(
        _ada_kernel,
        grid=(depth, n // tn),
        in_specs=[
            pl.BlockSpec((bsz, d), lambda l, j: (0, 0)),
            pl.BlockSpec((1, d, tn), lambda l, j: (l, 0, j)),
            pl.BlockSpec((1, 1, tn), lambda l, j: (l, 0, j)),
        ],
        out_specs=pl.BlockSpec((1, bsz, tn), lambda l, j: (l, 0, j)),
        out_shape=jax.ShapeDtypeStruct((depth, bsz, n), _F32),
        compiler_params=_cparams(2),
        name="ada",
    )(c, w_ada, b_ada.reshape(depth, 1, n))


def _rope_freq_lanes():
    half = MLA_ROPE_DIM // 2
    inv_freq = np.power(np.float32(ROPE_THETA), -np.arange(half, dtype=np.float32) / np.float32(half))
    f = np.zeros((1, MLA_HEAD_PAD), np.float32)
    f[0, MLA_NOPE_DIM:MLA_NOPE_DIM + half] = inv_freq
    f[0, MLA_NOPE_DIM + half:MLA_NOPE_DIM + 2 * half] = inv_freq
    return f


def _rope_kernel(pos_ref, f_ref, cos_ref, sin_ref):
    ang = pos_ref[0].astype(_F32) * f_ref[...]
    cos_ref[0] = jnp.cos(ang)
    sin_ref[0] = jnp.sin(ang)


def _rope_call(positions):
    bsz, s = positions.shape
    tm = min(s, 512)
    out = jax.ShapeDtypeStruct((bsz, s, MLA_HEAD_PAD), _F32)
    return pl.pallas_call(
        _rope_kernel,
        grid=(bsz, s // tm),
        in_specs=[
            pl.BlockSpec((1, tm, 1), lambda b, i: (b, i, 0)),
            pl.BlockSpec((1, MLA_HEAD_PAD), lambda b, i: (0, 0)),
        ],
        out_specs=[pl.BlockSpec((1, tm, MLA_HEAD_PAD), lambda b, i: (b, i, 0))] * 2,
        out_shape=[out, out],
        compiler_params=_cparams(2),
        name="rope_tables",
    )(positions.reshape(bsz, s, 1), jnp.asarray(_rope_freq_lanes()))


def _ffn_kernel(sub, h_ref, ada_ref, w1_ref, w3_ref, w2_ref, g_ref, b_ref, o_ref):
    shift = ada_ref[0, 3 * sub + 0:3 * sub + 1, :]
    scale = ada_ref[0, 3 * sub + 1:3 * sub + 2, :]
    gate = ada_ref[0, 3 * sub + 2:3 * sub + 3, :]
    for r0 in range(0, h_ref.shape[1], FFN_ROWS):
        rows = slice(r0, r0 + FFN_ROWS)
        h = h_ref[0, rows, :]
        xm = (h * (1.0 + scale) + shift).astype(_BF16)
        acc = jnp.zeros(h.shape, _F32)
        for j in range(N_FF_CHUNKS):
            cols = slice(j * FF_CHUNK, (j + 1) * FF_CHUNK)
            a = _dot(xm, w1_ref[:, cols])
            b = _dot(xm, w3_ref[:, cols])
            g = (a * jax.nn.sigmoid(a) * b).astype(_BF16)
            acc = acc + _dot(g, w2_ref[cols, :])
        y = DEEPNORM_ALPHA * h + MACARON_WEIGHT * gate * acc
        o_ref[0, rows, :] = _layer_norm(y, g_ref[...], b_ref[...])


def _ffn_call(h, ada, sub, w1, w3, w2, ln_g, ln_b):
    bsz, s, d = h.shape
    tm = min(s, 2 * FFN_ROWS)
    return pl.pallas_call(
        functools.partial(_ffn_kernel, sub),
        grid=(bsz, s // tm),
        in_specs=[
            pl.BlockSpec((1, tm, d), lambda b, i: (b, i, 0)),
            pl.BlockSpec((1, 9, d), lambda b, i: (b, 0, 0)),
            _resident((d, D_FF), lambda b, i: (0, 0)),
            _resident((d, D_FF), lambda b, i: (0, 0)),
            _resident((D_FF, d), lambda b, i: (0, 0)),
            pl.BlockSpec((1, d), lambda b, i: (0, 0)),
            pl.BlockSpec((1, d), lambda b, i: (0, 0)),
        ],
        out_specs=pl.BlockSpec((1, tm, d), lambda b, i: (b, i, 0)),
        out_shape=jax.ShapeDtypeStruct((bsz, s, d), _F32),
        compiler_params=_cparams(2),
        name="ffn_sublayer",
    )(h, ada, w1, w3, w2, ln_g, ln_b)


def _perm_matrix(dil):
    n = PERM_TILE // dil
    p = np.zeros((PERM_TILE, PERM_TILE), np.float32)
    for r in range(dil):
        for i in range(n):
            p[r * n + i, i * dil + r] = 1.0
    return p


def _proj_kernel(tm, dils, x_ref, ada_ref, wm_ref, bm_ref, *rest):
    ngrp = len(dils)
    grp_in = rest[:2 * ngrp]
    om_ref = rest[2 * ngrp]
    grp_out = rest[2 * ngrp + 1:2 * ngrp + 1 + ngrp]
    xs_ref, xp_ref = rest[2 * ngrp + 1 + ngrp:]
    x = x_ref[0]
    shift = ada_ref[0, 3:4, :]
    scale = ada_ref[0, 4:5, :]
    xm_f32 = x * (1.0 + scale) + shift
    xm = xm_f32.astype(_BF16)
    n_main = wm_ref.shape[1]
    for n0 in range(0, n_main, PROJ_CHUNK):
        n1 = min(n0 + PROJ_CHUNK, n_main)
        om_ref[0, :, n0:n1] = (_dot(xm, wm_ref[:, n0:n1]) + bm_ref[:, n0:n1]).astype(_BF16)
    nslab = x.shape[1] // LANES
    for k in range(nslab):
        xs_ref[k] = xm_f32[:, k * LANES:(k + 1) * LANES]
    for g, dil in enumerate(dils):
        w_ref, b_ref = grp_in[2 * g:2 * g + 2]
        o_ref = grp_out[g]
        n = tm // dil
        n_out = w_ref.shape[1]
        for r in range(dil):
            for k in range(nslab):
                xp_ref[g, r * n:(r + 1) * n, k * LANES:(k + 1) * LANES] = (
                    xs_ref[k, pl.ds(r, n, stride=dil), :].astype(_BF16))
        xp = xp_ref[g]
        for n0 in range(0, n_out, PROJ_CHUNK):
            n1 = min(n0 + PROJ_CHUNK, n_out)
            res = (_dot(xp, w_ref[:, n0:n1]) + b_ref[:, n0:n1]).astype(_BF16)
            for r in range(dil):
                o_ref[0, r, :, n0:n1] = res[r * n:(r + 1) * n]


def _proj_call(h, ada, w_main, b_main, groups):
    bsz, s, d = h.shape
    tm = min(s, 512)
    dils = tuple(g[0] for g in groups)
    const = lambda b, i: (0, 0)
    in_specs = [
        pl.BlockSpec((1, tm, d), lambda b, i: (b, i, 0)),
        pl.BlockSpec((1, 9, d), lambda b, i: (b, 0, 0)),
        _resident(w_main.shape, const),
        pl.BlockSpec(b_main.shape, const),
    ]
    args = [h, ada, w_main, b_main]
    out_specs = [pl.BlockSpec((1, tm, w_main.shape[1]), lambda b, i: (b, i, 0))]
    out_shape = [jax.ShapeDtypeStruct((bsz, s, w_main.shape[1]), _BF16)]
    for dil, w, bias in groups:
        in_specs += [_resident(w.shape, const), pl.BlockSpec(bias.shape, const)]
        args += [w, bias]
        out_specs.append(pl.BlockSpec((1, dil, tm // dil, w.shape[1]), lambda b, i: (b, 0, i, 0)))
        out_shape.append(jax.ShapeDtypeStruct((bsz, dil, s // dil, w.shape[1]), _BF16))
    return pl.pallas_call(
        functools.partial(_proj_kernel, tm, dils),
        grid=(bsz, s // tm),
        in_specs=in_specs,
        out_specs=out_specs,
        out_shape=out_shape,
        scratch_shapes=[pltpu.VMEM((d // LANES, tm, LANES), _F32), pltpu.VMEM((len(groups), tm, d), _BF16)],
        compiler_params=_cparams(2),
        name="in_proj",
    )(*args)


def _mla_prep_kernel(a_ref, cos_ref, sin_ref, qg_ref, kvg_ref, wq_ref, wqr_ref, wk_ref, wv_ref,
                     pk_ref, pkr_ref, ones_ref, q_ref, k_ref, v_ref):
    a = a_ref[0]
    qa = a[:, :MLA_Q_RANK].astype(_F32)
    kva = a[:, MLA_Q_RANK:MLA_Q_RANK + MLA_KV_RANK].astype(_F32)
    kr = a[:, MLA_Q_RANK + MLA_KV_RANK:]
    qn = (qa * lax.rsqrt(jnp.mean(qa * qa, axis=-1, keepdims=True) + RMS_EPS) * qg_ref[...]).astype(_BF16)
    kvn = (kva * lax.rsqrt(jnp.mean(kva * kva, axis=-1, keepdims=True) + RMS_EPS) * kvg_ref[...]).astype(_BF16)
    cos = cos_ref[0]
    sin = sin_ref[0]
    q = _dot(qn, wq_ref[...])
    qrot = _dot(qn, wqr_ref[...])
    k = _dot(kvn, wk_ref[...]) + _dot(kr, pk_ref[...])
    krot = _dot(kr, pkr_ref[...])
    for hd in range(MLA_HEADS):
        sl = slice(hd * MLA_HEAD_PAD, (hd + 1) * MLA_HEAD_PAD)
        q_ref[0, :, sl] = ((q[:, sl] * cos + qrot[:, sl] * sin) * (MLA_SCALE * LOG2E)).astype(_BF16)
        k_ref[0, :, sl] = (k[:, sl] * cos + krot[:, sl] * sin).astype(_BF16)
    v_ref[0] = (_dot(kvn, wv_ref[...]) + ones_ref[...]).astype(_BF16)


def _mla_prep_call(proj_main, cos, sin, qg, kvg, wq, wqr, wk, wv, pk, pkr, ones_row):
    bsz, s, _ = proj_main.shape
    tm = min(s, 512)
    const = lambda b, i: (0, 0)
    return pl.pallas_call(
        _mla_prep_kernel,
        grid=(bsz, s // tm),
        in_specs=[
            pl.BlockSpec((1, tm, MLA_A_WIDTH), lambda b, i: (b, i, 0)),
            pl.BlockSpec((1, tm, MLA_HEAD_PAD), lambda b, i: (b, i, 0)),
            pl.BlockSpec((1, tm, MLA_HEAD_PAD), lambda b, i: (b, i, 0)),
            pl.BlockSpec((1, MLA_Q_RANK), const),
            pl.BlockSpec((1, MLA_KV_RANK), const),
            pl.BlockSpec((MLA_Q_RANK, MLA_QK_WIDTH), const),
            pl.BlockSpec((MLA_Q_RANK, MLA_QK_WIDTH), const),
            pl.BlockSpec((MLA_KV_RANK, MLA_QK_WIDTH), const),
            pl.BlockSpec((MLA_KV_RANK, MLA_QK_WIDTH), const),
            pl.BlockSpec((LANES, MLA_QK_WIDTH), const),
            pl.BlockSpec((LANES, MLA_QK_WIDTH), const),
            pl.BlockSpec((1, MLA_QK_WIDTH), const),
        ],
        out_specs=[pl.BlockSpec((1, tm, MLA_QK_WIDTH), lambda b, i: (b, i, 0))] * 3,
        out_shape=[jax.ShapeDtypeStruct((bsz, s, MLA_QK_WIDTH), _BF16)] * 3,
        compiler_params=_cparams(2),
        name="mla_prep",
    )(proj_main, cos, sin, qg, kvg, wq, wqr, wk, wv, pk, pkr, ones_row)


def _mla_attn_kernel(tq, q_ref, k_ref, v_ref, o_ref, s_ref, p_ref, m_ref, acc_ref):
    nh = MLA_HEADS_PER_STEP
    half = tq // 2
    qi = pl.program_id(2)
    row = lax.broadcasted_iota(jnp.int32, (tq, tq), 0)
    col = lax.broadcasted_iota(jnp.int32, (tq, tq), 1)
    causal = col <= row
    lanes = [slice(e * MLA_HEAD_PAD, (e + 1) * MLA_HEAD_PAD) for e in range(nh)]
    qs = [q_ref[0, :, lanes[e]] for e in range(nh)]

    def scores(j, slot):
        start = pl.multiple_of(j * tq, tq)
        for e in range(nh):
            s_ref[slot, e] = _dot_nt(qs[e], k_ref[0, pl.ds(start, tq), lanes[e]])

    def softmax_pv(j, slot, masked):
        start = pl.multiple_of(j * tq, tq)
        for e in range(nh):
            alphas = []
            for c in range(tq // MLA_ROW_CHUNK):
                rows = slice(c * MLA_ROW_CHUNK, (c + 1) * MLA_ROW_CHUNK)
                cols = slice(0, half if masked and (c + 1) * MLA_ROW_CHUNK <= half else tq)
                s = s_ref[slot, e, rows, cols]
                if masked:
                    s = jnp.where(causal[rows, cols], s, _NEG)
                m_old = m_ref[e, rows, :]
                m_c = jnp.maximum(m_old, jnp.max(s, axis=-1, keepdims=True))
                p_ref[slot, e, rows, cols] = jnp.exp2(s - m_c).astype(_BF16)
                m_ref[e, rows, :] = m_c
                alphas.append(jnp.exp2(m_old - m_c))
            if masked:
                pv = jnp.concatenate([
                    _dot(p_ref[slot, e, :half, :half], v_ref[0, pl.ds(start, half), lanes[e]]),
                    _dot(p_ref[slot, e, half:, :], v_ref[0, pl.ds(start, tq), lanes[e]])], axis=0)
            else:
                pv = _dot(p_ref[slot, e], v_ref[0, pl.ds(start, tq), lanes[e]])
            acc_ref[e] = jnp.concatenate(alphas, axis=0) * acc_ref[e] + pv

    def pair(jj, carry):
        t = 2 * jj
        scores(t + 1, 1)
        softmax_pv(t, 0, False)
        scores(t + 2, 0)
        softmax_pv(t + 1, 1, False)
        return carry

    m_ref[...] = jnp.full(m_ref.shape, _NEG, _F32)
    acc_ref[...] = jnp.zeros(acc_ref.shape, _F32)
    scores(0, 0)
    lax.fori_loop(0, qi // 2, pair, 0)

    @pl.when(qi % 2 == 1)
    def _():
        scores(qi, 1)
        softmax_pv(qi - 1, 0, False)
        softmax_pv(qi, 1, True)

    @pl.when(qi % 2 == 0)
    def _():
        softmax_pv(qi, 0, True)

    lane = lax.broadcasted_iota(jnp.int32, (tq, MLA_HEAD_PAD), 1)
    for e2 in range(nh // 2):
        acc0, acc1 = acc_ref[2 * e2], acc_ref[2 * e2 + 1]
        out0 = acc0 * (1.0 / acc0[:, MLA_V_DIM:MLA_V_DIM + 1])
        out1 = acc1 * (1.0 / acc1[:, 0:1])
        o_ref[0, :, lanes[e2]] = jnp.where(lane < MLA_V_DIM, out0, out1).astype(_BF16)


def _mla_attn_call(q, k, v):
    bsz, s, _ = q.shape
    tq = min(s, 512)
    nh = MLA_HEADS_PER_STEP
    qk_w = nh * MLA_HEAD_PAD
    return pl.pallas_call(
        functools.partial(_mla_attn_kernel, tq),
        grid=(bsz, MLA_HEADS // nh, s // tq),
        in_specs=[
            pl.BlockSpec((1, tq, qk_w), lambda b, h, i: (b, i, h)),
            pl.BlockSpec((1, s, qk_w), lambda b, h, i: (b, 0, h)),
            pl.BlockSpec((1, s, qk_w), lambda b, h, i: (b, 0, h)),
        ],
        out_specs=pl.BlockSpec((1, tq, nh * MLA_V_DIM), lambda b, h, i: (b, i, h)),
        out_shape=jax.ShapeDtypeStruct((bsz, s, MLA_V_WIDTH), _BF16),
        scratch_shapes=[pltpu.VMEM((2, nh, tq, tq), _F32), pltpu.VMEM((2, nh, tq, tq), _BF16),
                        pltpu.VMEM((nh, tq, 1), _F32), pltpu.VMEM((nh, tq, MLA_HEAD_PAD), _F32)],
        compiler_params=_cparams(3),
        name="mla_attention",
    )(q, k, v)


def _alibi_slope(head):
    return float(np.exp2(np.float32(-8.0) * (np.float32(head) + np.float32(1.0)) / np.float32(DIL_HEADS)))


def _dil_attn_kernel(tq, kw, single_tile, nres, group, dil, q_ref, k_ref, v_ref, o_ref, lse_ref, bias_ref):
    qi = pl.program_id(2)
    first_step = (pl.program_id(0) == 0) & (pl.program_id(1) == 0) & (qi == 0)
    nb = DIL_BAND

    @pl.when(first_step)
    def _():
        row = lax.broadcasted_iota(jnp.int32, (nb, kw), 0)
        col = lax.broadcasted_iota(jnp.int32, (nb, kw), 1)
        for variant, rel in ((0, nb + row - col), (1, row - col)):
            ok = (rel >= 0) & (rel <= DIL_BAND)
            relf = rel.astype(_F32)
            for i in range(DIL_HPG):
                slope = _alibi_slope(group * DIL_HPG + i) * dil * LOG2E
                bias_ref[i, variant] = jnp.where(ok, -slope * relf, _NEG)

    q0 = pl.multiple_of(qi * tq, tq)
    lane = lax.broadcasted_iota(jnp.int32, (nb, LANES), 1)
    for rr in range(nres):
        for sb in range(tq // nb):
            rows = slice(sb * nb, (sb + 1) * nb)
            if sb == 0:
                start = pl.multiple_of(jnp.maximum(q0 - nb, 0), nb)
            else:
                start = pl.multiple_of(q0 + (sb - 1) * nb, nb)
            m_all = jnp.zeros((nb, LANES), _F32)
            l_all = jnp.ones((nb, LANES), _F32)
            for i in range(DIL_HPG):
                lanes = slice(i * DIL_HEAD_DIM, (i + 1) * DIL_HEAD_DIM)
                if sb > 0:
                    bias = bias_ref[i, 0]
                elif single_tile:
                    bias = bias_ref[i, 1]
                else:
                    bias = jnp.where(qi == 0, bias_ref[i, 1], bias_ref[i, 0])
                s = _dot_nt(q_ref[0, rr, rows, lanes], k_ref[0, rr, pl.ds(start, kw), lanes]) + bias
                m = jnp.max(s, axis=-1, keepdims=True)
                e = jnp.exp2(s - m)
                l = jnp.sum(e, axis=-1, keepdims=True)
                o = _dot(e.astype(_BF16), v_ref[0, rr, pl.ds(start, kw), lanes])
                o_ref[0, rr, rows, lanes] = (o * (1.0 / l)).astype(_BF16)
                m_all = jnp.where(lane == i, m, m_all)
                l_all = jnp.where(lane == i, l, l_all)
            lse_ref[0, rr, rows, :] = (m_all + jnp.log2(l_all)) * (1.0 / LOG2E)


def _dil_attn_call(arr, col_block0, group, dil, tq, nres):
    bsz, _, length, _ = arr.shape
    w = DIL_GROUP_WIDTH
    tq = min(tq, length)
    kw = min(2 * DIL_BAND, length)
    nres = min(nres, dil)
    return pl.pallas_call(
        functools.partial(_dil_attn_kernel, tq, kw, length == tq, nres, group, dil),
        grid=(bsz, dil // nres, length // tq),
        in_specs=[
            pl.BlockSpec((1, nres, tq, w), lambda b, r, i: (b, r, i, col_block0)),
            pl.BlockSpec((1, nres, length, w), lambda b, r, i: (b, r, 0, col_block0 + 1)),
            pl.BlockSpec((1, nres, length, w), lambda b, r, i: (b, r, 0, col_block0 + 2)),
        ],
        out_specs=[
            pl.BlockSpec((1, nres, tq, w), lambda b, r, i: (b, r, i, 0)),
            pl.BlockSpec((1, nres, tq, LANES), lambda b, r, i: (b, r, i, 0)),
        ],
        out_shape=[
            jax.ShapeDtypeStruct((bsz, dil, length, w), _BF16),
            jax.ShapeDtypeStruct((bsz, dil, length, LANES), _F32),
        ],
        scratch_shapes=[pltpu.VMEM((DIL_HPG, 2, DIL_BAND, kw), _F32)],
        compiler_params=_cparams(3),
        name=f"dil_attention_g{group}",
    )(arr, arr, arr)


def _ssm_kernel(nb, u_ref, bw_ref, cw_ref, are_ref, aim_ref, d_ref, o_ref, bx_ref, st_ref):
    ti = pl.program_id(1)

    @pl.when(ti == 0)
    def _():
        st_ref[...] = jnp.zeros(st_ref.shape, _F32)

    for b in range(nb):
        bu = _dot(u_ref[b, 0], bw_ref[0])
        for sidx in range(2 * SSM_SLABS):
            bx_ref[sidx, b * SSM_PITCH:b * SSM_PITCH + SSM_TT, :] = bu[:, sidx * LANES:(sidx + 1) * LANES]

    a_re = [jnp.broadcast_to(are_ref[0, :, k * LANES:(k + 1) * LANES], (nb, LANES)) for k in range(SSM_SLABS)]
    a_im = [jnp.broadcast_to(aim_ref[0, :, k * LANES:(k + 1) * LANES], (nb, LANES)) for k in range(SSM_SLABS)]

    def step(t, carry):
        xr, xi = carry
        nr, ni = [], []
        for k in range(SSM_SLABS):
            rows = pl.ds(t, nb, stride=SSM_PITCH)
            br = bx_ref[k, rows, :]
            bi = bx_ref[SSM_SLABS + k, rows, :]
            r = a_re[k] * xr[k] - a_im[k] * xi[k] + br
            i = a_re[k] * xi[k] + a_im[k] * xr[k] + bi
            bx_ref[k, rows, :] = r
            bx_ref[SSM_SLABS + k, rows, :] = i
            nr.append(r)
            ni.append(i)
        return tuple(nr), tuple(ni)

    x0 = (tuple(st_ref[k] for k in range(SSM_SLABS)),
          tuple(st_ref[SSM_SLABS + k] for k in range(SSM_SLABS)))
    xr, xi = lax.fori_loop(0, SSM_TT, step, x0, unroll=8)
    for k in range(SSM_SLABS):
        st_ref[k] = xr[k]
        st_ref[SSM_SLABS + k] = xi[k]

    dskip = d_ref[0]
    for b in range(nb):
        xs = jnp.concatenate(
            [bx_ref[sidx, b * SSM_PITCH:b * SSM_PITCH + SSM_TT, :] for sidx in range(2 * SSM_SLABS)],
            axis=1).astype(_BF16)
        y = _dot(xs, cw_ref[0]) + dskip * u_ref[b, 0].astype(_F32)
        o_ref[b] = jax.nn.gelu(y).astype(_BF16)


def _ssm_call(proj_main, bw, cw, a_re, a_im, d_skip):
    bsz, _, s, _ = proj_main.shape
    u_block0 = COL_U // SSM_BLOCK_CH
    nstate = 2 * SSM_BLOCK_STATES
    return pl.pallas_call(
        functools.partial(_ssm_kernel, bsz),
        grid=(SSM_BLOCKS, s // SSM_TT),
        in_specs=[
            pl.BlockSpec((bsz, 1, SSM_TT, SSM_BLOCK_CH), lambda m, t: (0, 0, t, u_block0 + m)),
            pl.BlockSpec((1, SSM_BLOCK_CH, nstate), lambda m, t: (m, 0, 0)),
            pl.BlockSpec((1, nstate, SSM_BLOCK_CH), lambda m, t: (m, 0, 0)),
            pl.BlockSpec((1, 1, SSM_BLOCK_STATES), lambda m, t: (m, 0, 0)),
            pl.BlockSpec((1, 1, SSM_BLOCK_STATES), lambda m, t: (m, 0, 0)),
            pl.BlockSpec((1, 1, SSM_BLOCK_CH), lambda m, t: (m, 0, 0)),
        ],
        out_specs=pl.BlockSpec((bsz, SSM_TT, SSM_BLOCK_CH), lambda m, t: (0, t, m)),
        out_shape=jax.ShapeDtypeStruct((bsz, s, SSM_WIDTH), _BF16),
        scratch_shapes=[
            pltpu.VMEM((2 * SSM_SLABS, bsz * SSM_PITCH, LANES), _F32),
            pltpu.VMEM((2 * SSM_SLABS, bsz, LANES), _F32),
        ],
        compiler_params=_cparams(2),
        name="s5_scan",
    )(proj_main, bw, cw, a_re, a_im, d_skip)


def _ssm_weights(lam_re, lam_im, log_dt, b_re, b_im, c_re, c_im, d_skip):
    lam = lax.complex(lam_re.astype(_F32), lam_im.astype(_F32))
    dt = jnp.exp(log_dt.astype(_F32))[:, None]
    lam_bar = jnp.exp(lam * dt)
    b_bar = ((lam_bar - 1.0) / lam)[..., None] * lax.complex(b_re.astype(_F32), b_im.astype(_F32))
    gpb = SSM_GROUPS // SSM_BLOCKS
    eye = jnp.eye(gpb, dtype=_F32)

    def in_weights(part):
        w = part.reshape(SSM_BLOCKS, gpb, SSM_STATE, SSM_GROUP_SIZE)
        w = jnp.einsum("mgph,gk->mghkp", w, eye)
        return w.reshape(SSM_BLOCKS, SSM_BLOCK_CH, SSM_BLOCK_STATES)

    def out_weights(part):
        w = part.reshape(SSM_BLOCKS, gpb, SSM_GROUP_SIZE, SSM_STATE)
        w = jnp.einsum("mgkp,gj->mgpjk", w, eye)
        return w.reshape(SSM_BLOCKS, SSM_BLOCK_STATES, SSM_BLOCK_CH)

    bw = jnp.concatenate([in_weights(jnp.real(b_bar)), in_weights(jnp.imag(b_bar))], axis=2).astype(_BF16)
    cw = jnp.concatenate([out_weights(c_re.astype(_F32)), -out_weights(c_im.astype(_F32))], axis=1).astype(_BF16)
    a_re = jnp.real(lam_bar).reshape(SSM_BLOCKS, 1, SSM_BLOCK_STATES)
    a_im = jnp.imag(lam_bar).reshape(SSM_BLOCKS, 1, SSM_BLOCK_STATES)
    return bw, cw, a_re, a_im, d_skip.astype(_F32).reshape(SSM_BLOCKS, 1, SSM_BLOCK_CH)


def _to_token_order(ref, t, pt_ref):
    _, dil, _, width = ref.shape
    n = PERM_TILE // dil
    x = ref[0, :, t * n:(t + 1) * n, :].reshape(PERM_TILE, width)
    if x.dtype == _BF16:
        return _dot(pt_ref[...], x)
    hi = x.astype(_BF16)
    r1 = x - hi.astype(_F32)
    mid = r1.astype(_BF16)
    lo = (r1 - mid.astype(_F32)).astype(_BF16)
    res = _dot(pt_ref[...], jnp.concatenate([hi, mid, lo], axis=1))
    return (res[:, :width] + res[:, width:2 * width]) + res[:, 2 * width:]


def _merge_kernel(h_ref, ada_ref, ymla_ref, o0_ref, o1_ref, o2_ref, l0_ref, l1_ref, l2_ref, pt1_ref, pt2_ref,
                  ys_ref, gl_ref, wglu_ref, bglu_ref, wbr_ref, wout_ref, g_ref, b_ref, out_ref):
    gate = ada_ref[0, 5:6, :]
    d = h_ref.shape[-1]
    for t in range(h_ref.shape[1] // PERM_TILE):
        rows = slice(t * PERM_TILE, (t + 1) * PERM_TILE)
        h = h_ref[0, rows, :]
        o0 = o0_ref[0, 0, rows, :].astype(_F32)
        o1 = _to_token_order(o1_ref, t, pt1_ref)
        o2 = _to_token_order(o2_ref, t, pt2_ref)
        l0 = l0_ref[0, 0, rows, :]
        l1 = _to_token_order(l1_ref, t, pt1_ref)
        l2 = _to_token_order(l2_ref, t, pt2_ref)
        m = jnp.maximum(jnp.maximum(l0, l1), l2)
        e0, e1, e2 = jnp.exp(l0 - m), jnp.exp(l1 - m), jnp.exp(l2 - m)
        inv = 1.0 / (e0 + e1 + e2)
        w0, w1, w2 = e0 * inv, e1 * inv, e2 * inv
        heads = []
        for i in range(DIL_HPG):
            lanes = slice(i * DIL_HEAD_DIM, (i + 1) * DIL_HEAD_DIM)
            heads.append(w0[:, i:i + 1] * o0[:, lanes] + w1[:, i:i + 1] * o1[:, lanes]
                         + w2[:, i:i + 1] * o2[:, lanes])
        y_dil = jnp.concatenate(heads, axis=1).astype(_BF16)
        ys = ys_ref[0, rows, :]
        glu = jax.nn.sigmoid(_dot(ys, wglu_ref[...]) + bglu_ref[...])
        y_ssm = (ys.astype(_F32) * glu).astype(_BF16)
        merged = (jax.nn.sigmoid(gl_ref[0, rows, 0:d].astype(_F32)) * _dot(ymla_ref[0, rows, :], wbr_ref[0])
                  + jax.nn.sigmoid(gl_ref[0, rows, d:2 * d].astype(_F32)) * _dot(y_dil, wbr_ref[1])
                  + jax.nn.sigmoid(gl_ref[0, rows, 2 * d:3 * d].astype(_F32)) * _dot(y_ssm, wbr_ref[2]))
        out = _dot(merged.astype(_BF16), wout_ref[...])
        y = DEEPNORM_ALPHA * h + gate * out
        out_ref[0, rows, :] = _layer_norm(y, g_ref[...], b_ref[...])


def _merge_call(h, ada, y_mla, dil_o, dil_lse, y_s, proj_main, wglu, bglu, wbr, wout, ln_g, ln_b):
    bsz, s, d = h.shape
    tm = 2 * PERM_TILE
    tok = lambda b, i: (b, i, 0)
    grp = lambda b, i: (b, 0, i, 0)
    const2 = lambda b, i: (0, 0)
    bw = BRANCH_WIDTH
    d1, d2 = DIL_PAIRS[1][1], DIL_PAIRS[2][1]
    pt1 = jnp.asarray(_perm_matrix(d1).T, _BF16)
    pt2 = jnp.asarray(_perm_matrix(d2).T, _BF16)
    return pl.pallas_call(
        _merge_kernel,
        grid=(bsz, s // tm),
        in_specs=[
            pl.BlockSpec((1, tm, d), tok),
            pl.BlockSpec((1, 9, d), lambda b, i: (b, 0, 0)),
            pl.BlockSpec((1, tm, bw), tok),
            pl.BlockSpec((1, 1, tm, bw), grp),
            pl.BlockSpec((1, d1, tm // d1, bw), grp),
            pl.BlockSpec((1, d2, tm // d2, bw), grp),
            pl.BlockSpec((1, 1, tm, LANES), grp),
            pl.BlockSpec((1, d1, tm // d1, LANES), grp),
            pl.BlockSpec((1, d2, tm // d2, LANES), grp),
            pl.BlockSpec((PERM_TILE, PERM_TILE), const2),
            pl.BlockSpec((PERM_TILE, PERM_TILE), const2),
            pl.BlockSpec((1, tm, bw), tok),
            pl.BlockSpec((1, tm, N_BRANCH * d), lambda b, i: (b, i, COL_GATES // (N_BRANCH * d))),
            pl.BlockSpec((bw, bw), const2),
            pl.BlockSpec((1, bw), const2),
            pl.BlockSpec((N_BRANCH, bw, d), lambda b, i: (0, 0, 0)),
            pl.BlockSpec((d, d), const2),
            pl.BlockSpec((1, d), const2),
            pl.BlockSpec((1, d), const2),
        ],
        out_specs=pl.BlockSpec((1, tm, d), tok),
        out_shape=jax.ShapeDtypeStruct((bsz, s, d), _F32),
        compiler_params=_cparams(2),
        name="merge_sublayer",
    )(h, ada, y_mla, dil_o[0], dil_o[1], dil_o[2], dil_lse[0], dil_lse[1], dil_lse[2], pt1, pt2, y_s,
      proj_main, wglu, bglu, wbr, wout, ln_g, ln_b)


def _in_proj_weights(w_in, b_in):
    d = w_in.shape[0]
    o_kr = MLA_Q_RANK + MLA_KV_RANK
    o_dil = o_kr + MLA_ROPE_DIM
    o_u = o_dil + 3 * DIL_HEADS * DIL_HEAD_DIM
    o_g = o_u + SSM_WIDTH

    def dil_cols(arr, part, group):
        start = o_dil + part * DIL_HEADS * DIL_HEAD_DIM + group * DIL_GROUP_WIDTH
        cols = arr[..., start:start + DIL_GROUP_WIDTH]
        return cols * (DIL_SCALE * LOG2E) if part == 0 else cols

    def main(arr):
        pad = jnp.zeros(arr.shape[:-1] + (COL_Q1 - o_dil,), arr.dtype)
        return jnp.concatenate([arr[..., :o_dil], pad, dil_cols(arr, 0, 0), dil_cols(arr, 1, 0),
                                dil_cols(arr, 2, 0), arr[..., o_u:o_g], arr[..., o_g:]], axis=-1)

    def group(arr, g):
        return jnp.concatenate([dil_cols(arr, 0, g), dil_cols(arr, 1, g), dil_cols(arr, 2, g)], axis=-1)

    b2 = b_in.astype(_F32).reshape(1, -1)
    return ((main(w_in).astype(_BF16), main(b2)),
            (group(w_in, 1).astype(_BF16), group(b2, 1)),
            (group(w_in, 2).astype(_BF16), group(b2, 2)))


def _mla_weights(w_qb, w_kvb):
    qd = MLA_NOPE_DIM + MLA_ROPE_DIM
    half = MLA_ROPE_DIM // 2
    wq = w_qb.reshape(MLA_Q_RANK, MLA_HEADS, qd)
    zeros = jnp.zeros((MLA_Q_RANK, MLA_HEADS, MLA_HEAD_PAD - qd), w_qb.dtype)
    zn = jnp.zeros((MLA_Q_RANK, MLA_HEADS, MLA_NOPE_DIM), w_qb.dtype)
    wq_pad = jnp.concatenate([wq, zeros], axis=-1).reshape(MLA_Q_RANK, MLA_QK_WIDTH)
    t1 = wq[..., MLA_NOPE_DIM:MLA_NOPE_DIM + half]
    t2 = wq[..., MLA_NOPE_DIM + half:]
    wq_rot = jnp.concatenate([zn, -t2, t1, zeros], axis=-1).reshape(MLA_Q_RANK, MLA_QK_WIDTH)
    wkv = w_kvb.reshape(MLA_KV_RANK, MLA_HEADS, MLA_NOPE_DIM + MLA_V_DIM)
    zk = jnp.zeros((MLA_KV_RANK, MLA_HEADS, MLA_HEAD_PAD - MLA_NOPE_DIM), w_kvb.dtype)
    wk = jnp.concatenate([wkv[..., :MLA_NOPE_DIM], zk], axis=-1).reshape(MLA_KV_RANK, MLA_QK_WIDTH)
    wv = wkv[..., MLA_NOPE_DIM:].reshape(MLA_KV_RANK, MLA_HEADS // 2, 2, MLA_V_DIM)
    zv = jnp.zeros_like(wv[:, :, 0])
    wv_pad = jnp.concatenate([wv[:, :, 0], zv, zv, wv[:, :, 1]], axis=-1).reshape(MLA_KV_RANK, MLA_QK_WIDTH)
    return wq_pad.astype(_BF16), wq_rot.astype(_BF16), wk.astype(_BF16), wv_pad.astype(_BF16)


def _mla_ones_row():
    ones = np.zeros((1, MLA_QK_WIDTH), np.float32)
    for hd in range(MLA_HEADS):
        ones[0, hd * MLA_HEAD_PAD + (MLA_V_DIM if hd % 2 == 0 else 0)] = 1.0
    return jnp.asarray(ones)


def _rope_key_placement():
    half = MLA_ROPE_DIM // 2
    pk = np.zeros((LANES, MLA_QK_WIDTH), np.float32)
    pkr = np.zeros((LANES, MLA_QK_WIDTH), np.float32)
    for hd in range(MLA_HEADS):
        base = hd * MLA_HEAD_PAD + MLA_NOPE_DIM
        for i in range(MLA_ROPE_DIM):
            pk[i, base + i] = 1.0
        for i in range(half):
            pkr[half + i, base + i] = -1.0
            pkr[i, base + half + i] = 1.0
    return jnp.asarray(pk, _BF16), jnp.asarray(pkr, _BF16)


def _ffn_weights(w1, w3, w2):
    return w1.astype(_BF16), w3.astype(_BF16), w2.astype(_BF16)


def kernel(x, c, positions, w_ada, b_ada, ln_g, ln_b, ffn_w1, ffn_w3, ffn_w2, w_in, b_in, mla_q_norm, mla_kv_norm, mla_w_qb, mla_w_kvb, ssm_lambda_re, ssm_lambda_im, ssm_log_dt, ssm_b_re, ssm_b_im, ssm_c_re, ssm_c_im, ssm_d, ssm_w_glu, ssm_b_glu, w_br, w_out):
    bsz, s, d = x.shape
    assert d == D_MODEL and s % (DIL_PAIRS[2][1] * DIL_BAND) == 0, x.shape
    ada_all = _ada_call(c, w_ada, b_ada).reshape(DEPTH, bsz, 9, d)
    cos, sin = _rope_call(positions)
    pk, pkr = _rope_key_placement()
    ones_row = _mla_ones_row()
    h = x
    for l in range(DEPTH):
        ada = ada_all[l]
        h = _ffn_call(h, ada, 0, *_ffn_weights(ffn_w1[l, 0], ffn_w3[l, 0], ffn_w2[l, 0]),
                      ln_g[l, 0].reshape(1, d), ln_b[l, 0].reshape(1, d))
        (w_main, b_main), (w_g1, b_g1), (w_g2, b_g2) = _in_proj_weights(w_in[l], b_in[l])
        proj_main, proj_g1, proj_g2 = _proj_call(
            h, ada, w_main, b_main, [(DIL_PAIRS[1][1], w_g1, b_g1), (DIL_PAIRS[2][1], w_g2, b_g2)])
        proj_main4 = proj_main.reshape(bsz, 1, s, N_MAIN)
        wq, wqr, wk, wv = _mla_weights(mla_w_qb[l], mla_w_kvb[l])
        q, k, v = _mla_prep_call(proj_main, cos, sin, mla_q_norm[l].reshape(1, -1),
                                 mla_kv_norm[l].reshape(1, -1), wq, wqr, wk, wv, pk, pkr, ones_row)
        y_mla = _mla_attn_call(q, k, v)
        o0, lse0 = _dil_attn_call(proj_main4, COL_Q1 // DIL_GROUP_WIDTH, 0, DIL_PAIRS[0][1], 512, 1)
        o1, lse1 = _dil_attn_call(proj_g1, 0, 1, DIL_PAIRS[1][1], 512, 1)
        o2, lse2 = _dil_attn_call(proj_g2, 0, 2, DIL_PAIRS[2][1], 256, 4)
        y_s = _ssm_call(proj_main4, *_ssm_weights(ssm_lambda_re[l], ssm_lambda_im[l], ssm_log_dt[l],
                                                 ssm_b_re[l], ssm_b_im[l], ssm_c_re[l], ssm_c_im[l], ssm_d[l]))
        h = _merge_call(h, ada, y_mla, (o0, o1, o2), (lse0, lse1, lse2), y_s, proj_main,
                        ssm_w_glu[l].astype(_BF16), ssm_b_glu[l].reshape(1, -1), w_br[l].astype(_BF16),
                        w_out[l].astype(_BF16), ln_g[l, 1].reshape(1, d), ln_b[l, 1].reshape(1, d))
        h = _ffn_call(h, ada, 2, *_ffn_weights(ffn_w1[l, 1], ffn_w3[l, 1], ffn_w2[l, 1]),
                      ln_g[l, 2].reshape(1, d), ln_b[l, 2].reshape(1, d))
    return h
```

```python
import functools
import math

import numpy as np
import jax
import jax.numpy as jnp
from jax import lax
from jax.experimental import pallas as pl
from jax.experimental.pallas import tpu as pltpu

D_MODEL = 1024
DEPTH = 4
D_FF = 2816
MLA_HEADS = 8
MLA_Q_RANK = 384
MLA_KV_RANK = 256
MLA_NOPE_DIM = 64
MLA_ROPE_DIM = 32
MLA_V_DIM = 64
ROPE_THETA = 10000.0
DIL_PAIRS = ((128, 1), (512, 4), (2048, 16))
DIL_HPG = 4
DIL_HEADS = DIL_HPG * len(DIL_PAIRS)
DIL_HEAD_DIM = 128
DIL_BAND = 128
SSM_WIDTH = 512
SSM_GROUP_SIZE = 16
SSM_GROUPS = SSM_WIDTH // SSM_GROUP_SIZE
SSM_STATE = 64
N_BRANCH = 3
BRANCH_WIDTH = 512
DEEPNORM_ALPHA = (2 * DEPTH) ** 0.25
MACARON_WEIGHT = 0.5
LN_EPS = 1e-5
RMS_EPS = 1e-6

LANES = 128
V7X_VMEM_LIMIT_BYTES = 56 * 1024 * 1024

FF_CHUNK = 256
FFN_ROWS = 512
N_FF_CHUNKS = D_FF // FF_CHUNK
MLA_HEAD_PAD = 128
MLA_QK_WIDTH = MLA_HEADS * MLA_HEAD_PAD
MLA_V_WIDTH = MLA_HEADS * MLA_V_DIM
MLA_SCALE = (MLA_NOPE_DIM + MLA_ROPE_DIM) ** -0.5
DIL_SCALE = DIL_HEAD_DIM ** -0.5
DIL_GROUP_WIDTH = DIL_HPG * DIL_HEAD_DIM
MLA_A_WIDTH = 768
COL_Q1 = 1024
COL_U = COL_Q1 + 3 * DIL_GROUP_WIDTH
COL_GATES = COL_U + SSM_WIDTH
N_MAIN = COL_GATES + N_BRANCH * D_MODEL
PROJ_CHUNK = 512
PERM_TILE = 256
LOG2E = math.log2(math.e)
MLA_ROW_CHUNK = 64
MLA_HEADS_PER_STEP = 4
SSM_BLOCKS = 4
SSM_BLOCK_CH = SSM_WIDTH // SSM_BLOCKS
SSM_BLOCK_STATES = SSM_GROUPS * SSM_STATE // SSM_BLOCKS
SSM_SLABS = SSM_BLOCK_STATES // LANES
SSM_TT = 128
SSM_PITCH = 132

_F32 = jnp.float32
_BF16 = jnp.bfloat16
_NEG = -1e30


def _cparams(n_axes):
    return pltpu.CompilerParams(
        dimension_semantics=("arbitrary",) * n_axes,
        vmem_limit_bytes=V7X_VMEM_LIMIT_BYTES,
    )


def _resident(block_shape, index_map):
    return pl.BlockSpec(block_shape, index_map, pipeline_mode=pl.Buffered(1))


def _layer_norm(y, g, b):
    mu = jnp.mean(y, axis=-1, keepdims=True)
    yc = y - mu
    var = jnp.mean(yc * yc, axis=-1, keepdims=True)
    return yc * lax.rsqrt(var + LN_EPS) * g + b


def _dot(a, b):
    return jnp.dot(a, b, preferred_element_type=_F32)


def _dot_nt(a, b):
    return lax.dot_general(a, b, (((1,), (1,)), ((), ())), preferred_element_type=_F32)


def _ada_kernel(c_ref, w_ref, b_ref, o_ref):
    c = c_ref[...]
    cond = (c * jax.nn.sigmoid(c)).astype(_BF16)
    o_ref[0] = _dot(cond, w_ref[0].astype(_BF16)) + b_ref[0]


def _ada_call(c, w_ada, b_ada):
    depth, d, n = w_ada.shape
    bsz = c.shape[0]
    tn = 1536
    return pl.pallas_call(
        _ada_kernel,
        grid=(depth, n // tn),
        in_specs=[
            pl.BlockSpec((bsz, d), lambda l, j: (0, 0)),
            pl.BlockSpec((1, d, tn), lambda l, j: (l, 0, j)),
            pl.BlockSpec((1, 1, tn), lambda l, j: (l, 0, j)),
        ],
        out_specs=pl.BlockSpec((1, bsz, tn), lambda l, j: (l, 0, j)),
        out_shape=jax.ShapeDtypeStruct((depth, bsz, n), _F32),
        compiler_params=_cparams(2),
        name="ada",
    )(c, w_ada, b_ada.reshape(depth, 1, n))


def _rope_freq_lanes():
    half = MLA_ROPE_DIM // 2
    inv_freq = np.power(np.float32(ROPE_THETA), -np.arange(half, dtype=np.float32) / np.float32(half))
    f = np.zeros((1, MLA_HEAD_PAD), np.float32)
    f[0, MLA_NOPE_DIM:MLA_NOPE_DIM + half] = inv_freq
    f[0, MLA_NOPE_DIM + half:MLA_NOPE_DIM + 2 * half] = inv_freq
    return f


def _rope_kernel(pos_ref, f_ref, cos_ref, sin_ref):
    ang = pos_ref[0].astype(_F32) * f_ref[...]
    cos_ref[0] = jnp.cos(ang)
    sin_ref[0] = jnp.sin(ang)


def _rope_call(positions):
    bsz, s = positions.shape
    tm = min(s, 512)
    out = jax.ShapeDtypeStruct((bsz, s, MLA_HEAD_PAD), _F32)
    return pl.pallas_call(
        _rope_kernel,
        grid=(bsz, s // tm),
        in_specs=[
            pl.BlockSpec((1, tm, 1), lambda b, i: (b, i, 0)),
            pl.BlockSpec((1, MLA_HEAD_PAD), lambda b, i: (0, 0)),
        ],
        out_specs=[pl.BlockSpec((1, tm, MLA_HEAD_PAD), lambda b, i: (b, i, 0))] * 2,
        out_shape=[out, out],
        compiler_params=_cparams(2),
        name="rope_tables",
    )(positions.reshape(bsz, s, 1), jnp.asarray(_rope_freq_lanes()))


def _ffn_kernel(sub, h_ref, ada_ref, w1_ref, w3_ref, w2_ref, g_ref, b_ref, o_ref):
    shift = ada_ref[0, 3 * sub + 0:3 * sub + 1, :]
    scale = ada_ref[0, 3 * sub + 1:3 * sub + 2, :]
    gate = ada_ref[0, 3 * sub + 2:3 * sub + 3, :]
    for r0 in range(0, h_ref.shape[1], FFN_ROWS):
        rows = slice(r0, r0 + FFN_ROWS)
        h = h_ref[0, rows, :]
        xm = (h * (1.0 + scale) + shift).astype(_BF16)
        acc = jnp.zeros(h.shape, _F32)
        for j in range(N_FF_CHUNKS):
            cols = slice(j * FF_CHUNK, (j + 1) * FF_CHUNK)
            a = _dot(xm, w1_ref[:, cols])
            b = _dot(xm, w3_ref[:, cols])
            g = (a * jax.nn.sigmoid(a) * b).astype(_BF16)
            acc = acc + _dot(g, w2_ref[cols, :])
        y = DEEPNORM_ALPHA * h + MACARON_WEIGHT * gate * acc
        o_ref[0, rows, :] = _layer_norm(y, g_ref[...], b_ref[...])


def _ffn_call(h, ada, sub, w1, w3, w2, ln_g, ln_b):
    bsz, s, d = h.shape
    tm = min(s, 2 * FFN_ROWS)
    return pl.pallas_call(
        functools.partial(_ffn_kernel, sub),
        grid=(bsz, s // tm),
        in_specs=[
            pl.BlockSpec((1, tm, d), lambda b, i: (b, i, 0)),
            pl.BlockSpec((1, 9, d), lambda b, i: (b, 0, 0)),
            _resident((d, D_FF), lambda b, i: (0, 0)),
            _resident((d, D_FF), lambda b, i: (0, 0)),
            _resident((D_FF, d), lambda b, i: (0, 0)),
            pl.BlockSpec((1, d), lambda b, i: (0, 0)),
            pl.BlockSpec((1, d), lambda b, i: (0, 0)),
        ],
        out_specs=pl.BlockSpec((1, tm, d), lambda b, i: (b, i, 0)),
        out_shape=jax.ShapeDtypeStruct((bsz, s, d), _F32),
        compiler_params=_cparams(2),
        name="ffn_sublayer",
    )(h, ada, w1, w3, w2, ln_g, ln_b)


def _perm_matrix(dil):
    n = PERM_TILE // dil
    p = np.zeros((PERM_TILE, PERM_TILE), np.float32)
    for r in range(dil):
        for i in range(n):
            p[r * n + i, i * dil + r] = 1.0
    return p


def _proj_kernel(tm, dils, x_ref, ada_ref, wm_ref, bm_ref, *rest):
    ngrp = len(dils)
    grp_in = rest[:2 * ngrp]
    om_ref = rest[2 * ngrp]
    grp_out = rest[2 * ngrp + 1:2 * ngrp + 1 + ngrp]
    xs_ref, xp_ref = rest[2 * ngrp + 1 + ngrp:]
    x = x_ref[0]
    shift = ada_ref[0, 3:4, :]
    scale = ada_ref[0, 4:5, :]
    xm_f32 = x * (1.0 + scale) + shift
    xm = xm_f32.astype(_BF16)
    n_main = wm_ref.shape[1]
    for n0 in range(0, n_main, PROJ_CHUNK):
        n1 = min(n0 + PROJ_CHUNK, n_main)
        om_ref[0, :, n0:n1] = (_dot(xm, wm_ref[:, n0:n1]) + bm_ref[:, n0:n1]).astype(_BF16)
    nslab = x.shape[1] // LANES
    for k in range(nslab):
        xs_ref[k] = xm_f32[:, k * LANES:(k + 1) * LANES]
    for g, dil in enumerate(dils):
        w_ref, b_ref = grp_in[2 * g:2 * g + 2]
        o_ref = grp_out[g]
        n = tm // dil
        n_out = w_ref.shape[1]
        for r in range(dil):
            for k in range(nslab):
                xp_ref[g, r * n:(r + 1) * n, k * LANES:(k + 1) * LANES] = (
                    xs_ref[k, pl.ds(r, n, stride=dil), :].astype(_BF16))
        xp = xp_ref[g]
        for n0 in range(0, n_out, PROJ_CHUNK):
            n1 = min(n0 + PROJ_CHUNK, n_out)
            res = (_dot(xp, w_ref[:, n0:n1]) + b_ref[:, n0:n1]).astype(_BF16)
            for r in range(dil):
                o_ref[0, r, :, n0:n1] = res[r * n:(r + 1) * n]


def _proj_call(h, ada, w_main, b_main, groups):
    bsz, s, d = h.shape
    tm = min(s, 512)
    dils = tuple(g[0] for g in groups)
    const = lambda b, i: (0, 0)
    in_specs = [
        pl.BlockSpec((1, tm, d), lambda b, i: (b, i, 0)),
        pl.BlockSpec((1, 9, d), lambda b, i: (b, 0, 0)),
        _resident(w_main.shape, const),
        pl.BlockSpec(b_main.shape, const),
    ]
    args = [h, ada, w_main, b_main]
    out_specs = [pl.BlockSpec((1, tm, w_main.shape[1]), lambda b, i: (b, i, 0))]
    out_shape = [jax.ShapeDtypeStruct((bsz, s, w_main.shape[1]), _BF16)]
    for dil, w, bias in groups:
        in_specs += [_resident(w.shape, const), pl.BlockSpec(bias.shape, const)]
        args += [w, bias]
        out_specs.append(pl.BlockSpec((1, dil, tm // dil, w.shape[1]), lambda b, i: (b, 0, i, 0)))
        out_shape.append(jax.ShapeDtypeStruct((bsz, dil, s // dil, w.shape[1]), _BF16))
    return pl.pallas_call(
        functools.partial(_proj_kernel, tm, dils),
        grid=(bsz, s // tm),
        in_specs=in_specs,
        out_specs=out_specs,
        out_shape=out_shape,
        scratch_shapes=[pltpu.VMEM((d // LANES, tm, LANES), _F32), pltpu.VMEM((len(groups), tm, d), _BF16)],
        compiler_params=_cparams(2),
        name="in_proj",
    )(*args)


def _mla_prep_kernel(a_ref, cos_ref, sin_ref, qg_ref, kvg_ref, wq_ref, wqr_ref, wk_ref, wv_ref,
                     pk_ref, pkr_ref, ones_ref, q_ref, k_ref, v_ref):
    a = a_ref[0]
    qa = a[:, :MLA_Q_RANK].astype(_F32)
    kva = a[:, MLA_Q_RANK:MLA_Q_RANK + MLA_KV_RANK].astype(_F32)
    kr = a[:, MLA_Q_RANK + MLA_KV_RANK:]
    qn = (qa * lax.rsqrt(jnp.mean(qa * qa, axis=-1, keepdims=True) + RMS_EPS) * qg_ref[...]).astype(_BF16)
    kvn = (kva * lax.rsqrt(jnp.mean(kva * kva, axis=-1, keepdims=True) + RMS_EPS) * kvg_ref[...]).astype(_BF16)
    cos = jnp.concatenate([cos_ref[0]] * 2, axis=1) * (MLA_SCALE * LOG2E)
    sin = jnp.concatenate([sin_ref[0]] * 2, axis=1) * (MLA_SCALE * LOG2E)
    kcos = jnp.concatenate([cos_ref[0]] * 2, axis=1)
    ksin = jnp.concatenate([sin_ref[0]] * 2, axis=1)
    for c in range(MLA_HEADS // 2):
        sl = slice(2 * c * MLA_HEAD_PAD, 2 * (c + 1) * MLA_HEAD_PAD)
        q = _dot(qn, wq_ref[:, sl])
        qrot = _dot(qn, wqr_ref[:, sl])
        q_ref[0, :, sl] = (q * cos + qrot * sin).astype(_BF16)
        k = _dot(kvn, wk_ref[:, sl]) + _dot(kr, pk_ref[:, sl])
        krot = _dot(kr, pkr_ref[:, sl])
        k_ref[0, :, sl] = (k * kcos + krot * ksin).astype(_BF16)
        v_ref[0, :, sl] = (_dot(kvn, wv_ref[:, sl]) + ones_ref[:, sl]).astype(_BF16)


def _mla_prep_call(proj_main, cos, sin, qg, kvg, wq, wqr, wk, wv, pk, pkr, ones_row):
    bsz, s, _ = proj_main.shape
    tm = min(s, 512)
    const = lambda b, i: (0, 0)
    return pl.pallas_call(
        _mla_prep_kernel,
        grid=(bsz, s // tm),
        in_specs=[
            pl.BlockSpec((1, tm, MLA_A_WIDTH), lambda b, i: (b, i, 0)),
            pl.BlockSpec((1, tm, MLA_HEAD_PAD), lambda b, i: (b, i, 0)),
            pl.BlockSpec((1, tm, MLA_HEAD_PAD), lambda b, i: (b, i, 0)),
            pl.BlockSpec((1, MLA_Q_RANK), const),
            pl.BlockSpec((1, MLA_KV_RANK), const),
            pl.BlockSpec((MLA_Q_RANK, MLA_QK_WIDTH), const),
            pl.BlockSpec((MLA_Q_RANK, MLA_QK_WIDTH), const),
            pl.BlockSpec((MLA_KV_RANK, MLA_QK_WIDTH), const),
            pl.BlockSpec((MLA_KV_RANK, MLA_QK_WIDTH), const),
            pl.BlockSpec((LANES, MLA_QK_WIDTH), const),
            pl.BlockSpec((LANES, MLA_QK_WIDTH), const),
            pl.BlockSpec((1, MLA_QK_WIDTH), const),
        ],
        out_specs=[pl.BlockSpec((1, tm, MLA_QK_WIDTH), lambda b, i: (b, i, 0))] * 3,
        out_shape=[jax.ShapeDtypeStruct((bsz, s, MLA_QK_WIDTH), _BF16)] * 3,
        compiler_params=_cparams(2),
        name="mla_prep",
    )(proj_main, cos, sin, qg, kvg, wq, wqr, wk, wv, pk, pkr, ones_row)


def _mla_attn_kernel(tq, q_ref, k_ref, v_ref, o_ref, s_ref, p_ref, m_ref, acc_ref):
    nh = MLA_HEADS_PER_STEP
    half = tq // 2
    qi = pl.program_id(2)
    row = lax.broadcasted_iota(jnp.int32, (tq, tq), 0)
    col = lax.broadcasted_iota(jnp.int32, (tq, tq), 1)
    causal = col <= row
    lanes = [slice(e * MLA_HEAD_PAD, (e + 1) * MLA_HEAD_PAD) for e in range(nh)]
    qs = [q_ref[0, :, lanes[e]] for e in range(nh)]

    def scores(j, slot):
        start = pl.multiple_of(j * tq, tq)
        for e in range(nh):
            s_ref[slot, e] = _dot_nt(qs[e], k_ref[0, pl.ds(start, tq), lanes[e]])

    def softmax_pv(j, slot, masked):
        start = pl.multiple_of(j * tq, tq)
        for e in range(nh):
            alphas = []
            for c in range(tq // MLA_ROW_CHUNK):
                rows = slice(c * MLA_ROW_CHUNK, (c + 1) * MLA_ROW_CHUNK)
                cols = slice(0, half if masked and (c + 1) * MLA_ROW_CHUNK <= half else tq)
                s = s_ref[slot, e, rows, cols]
                if masked:
                    s = jnp.where(causal[rows, cols], s, _NEG)
                m_old = m_ref[e, rows, :]
                m_c = jnp.maximum(m_old, jnp.max(s, axis=-1, keepdims=True))
                p_ref[e, rows, cols] = jnp.exp2(s - m_c).astype(_BF16)
                m_ref[e, rows, :] = m_c
                alphas.append(jnp.exp2(m_old - m_c))
            if masked:
                pv = jnp.concatenate([
                    _dot(p_ref[e, :half, :half], v_ref[0, pl.ds(start, half), lanes[e]]),
                    _dot(p_ref[e, half:, :], v_ref[0, pl.ds(start, tq), lanes[e]])], axis=0)
            else:
                pv = _dot(p_ref[e], v_ref[0, pl.ds(start, tq), lanes[e]])
            acc_ref[e] = jnp.concatenate(alphas, axis=0) * acc_ref[e] + pv

    def pair(jj, carry):
        t = 2 * jj
        scores(t + 1, 1)
        softmax_pv(t, 0, False)
        scores(t + 2, 0)
        softmax_pv(t + 1, 1, False)
        return carry

    m_ref[...] = jnp.full(m_ref.shape, _NEG, _F32)
    acc_ref[...] = jnp.zeros(acc_ref.shape, _F32)
    scores(0, 0)
    lax.fori_loop(0, qi // 2, pair, 0)

    @pl.when(qi % 2 == 1)
    def _():
        scores(qi, 1)
        softmax_pv(qi - 1, 0, False)
        softmax_pv(qi, 1, True)

    @pl.when(qi % 2 == 0)
    def _():
        softmax_pv(qi, 0, True)

    lane = lax.broadcasted_iota(jnp.int32, (tq, MLA_HEAD_PAD), 1)
    for e2 in range(nh // 2):
        acc0, acc1 = acc_ref[2 * e2], acc_ref[2 * e2 + 1]
        out0 = acc0 * (1.0 / acc0[:, MLA_V_DIM:MLA_V_DIM + 1])
        out1 = acc1 * (1.0 / acc1[:, 0:1])
        o_ref[0, :, lanes[e2]] = jnp.where(lane < MLA_V_DIM, out0, out1).astype(_BF16)


def _mla_attn_call(q, k, v):
    bsz, s, _ = q.shape
    tq = min(s, 512)
    nh = MLA_HEADS_PER_STEP
    qk_w = nh * MLA_HEAD_PAD
    return pl.pallas_call(
        functools.partial(_mla_attn_kernel, tq),
        grid=(bsz, MLA_HEADS // nh, s // tq),
        in_specs=[
            pl.BlockSpec((1, tq, qk_w), lambda b, h, i: (b, i, h)),
            pl.BlockSpec((1, s, qk_w), lambda b, h, i: (b, 0, h)),
            pl.BlockSpec((1, s, qk_w), lambda b, h, i: (b, 0, h)),
        ],
        out_specs=pl.BlockSpec((1, tq, nh * MLA_V_DIM), lambda b, h, i: (b, i, h)),
        out_shape=jax.ShapeDtypeStruct((bsz, s, MLA_V_WIDTH), _BF16),
        scratch_shapes=[pltpu.VMEM((2, nh, tq, tq), _F32), pltpu.VMEM((nh, tq, tq), _BF16),
                        pltpu.VMEM((nh, tq, 1), _F32), pltpu.VMEM((nh, tq, MLA_HEAD_PAD), _F32)],
        compiler_params=_cparams(3),
        name="mla_attention",
    )(q, k, v)


def _alibi_slope(head):
    return float(np.exp2(np.float32(-8.0) * (np.float32(head) + np.float32(1.0)) / np.float32(DIL_HEADS)))


def _dil_attn_kernel(tq, kw, single_tile, nres, group, dil, q_ref, k_ref, v_ref, o_ref, lse_ref, bias_ref):
    qi = pl.program_id(2)
    first_step = (pl.program_id(0) == 0) & (pl.program_id(1) == 0) & (qi == 0)
    nb = DIL_BAND

    @pl.when(first_step)
    def _():
        row = lax.broadcasted_iota(jnp.int32, (nb, kw), 0)
        col = lax.broadcasted_iota(jnp.int32, (nb, kw), 1)
        for variant, rel in ((0, nb + row - col), (1, row - col)):
            ok = (rel >= 0) & (rel <= DIL_BAND)
            relf = rel.astype(_F32)
            for i in range(DIL_HPG):
                slope = _alibi_slope(group * DIL_HPG + i) * dil * LOG2E
                bias_ref[i, variant] = jnp.where(ok, -slope * relf, _NEG)

    q0 = pl.multiple_of(qi * tq, tq)
    lane = lax.broadcasted_iota(jnp.int32, (nb, LANES), 1)
    for rr in range(nres):
        for sb in range(tq // nb):
            rows = slice(sb * nb, (sb + 1) * nb)
            if sb == 0:
                start = pl.multiple_of(jnp.maximum(q0 - nb, 0), nb)
            else:
                start = pl.multiple_of(q0 + (sb - 1) * nb, nb)
            m_all = jnp.zeros((nb, LANES), _F32)
            l_all = jnp.ones((nb, LANES), _F32)
            for i in range(DIL_HPG):
                lanes = slice(i * DIL_HEAD_DIM, (i + 1) * DIL_HEAD_DIM)
                if sb > 0:
                    bias = bias_ref[i, 0]
                elif single_tile:
                    bias = bias_ref[i, 1]
                else:
                    bias = jnp.where(qi == 0, bias_ref[i, 1], bias_ref[i, 0])
                s = _dot_nt(q_ref[0, rr, rows, lanes], k_ref[0, rr, pl.ds(start, kw), lanes]) + bias
                m = jnp.max(s, axis=-1, keepdims=True)
                e = jnp.exp2(s - m)
                l = jnp.sum(e, axis=-1, keepdims=True)
                o = _dot(e.astype(_BF16), v_ref[0, rr, pl.ds(start, kw), lanes])
                o_ref[0, rr, rows, lanes] = (o * (1.0 / l)).astype(_BF16)
                m_all = jnp.where(lane == i, m, m_all)
                l_all = jnp.where(lane == i, l, l_all)
            lse_ref[0, rr, rows, :] = (m_all + jnp.log2(l_all)) * (1.0 / LOG2E)


def _dil_attn_call(arr, col_block0, group, dil, tq, nres):
    bsz, _, length, _ = arr.shape
    w = DIL_GROUP_WIDTH
    tq = min(tq, length)
    kw = min(2 * DIL_BAND, length)
    nres = min(nres, dil)
    return pl.pallas_call(
        functools.partial(_dil_attn_kernel, tq, kw, length == tq, nres, group, dil),
        grid=(bsz, dil // nres, length // tq),
        in_specs=[
            pl.BlockSpec((1, nres, tq, w), lambda b, r, i: (b, r, i, col_block0)),
            pl.BlockSpec((1, nres, length, w), lambda b, r, i: (b, r, 0, col_block0 + 1)),
            pl.BlockSpec((1, nres, length, w), lambda b, r, i: (b, r, 0, col_block0 + 2)),
        ],
        out_specs=[
            pl.BlockSpec((1, nres, tq, w), lambda b, r, i: (b, r, i, 0)),
            pl.BlockSpec((1, nres, tq, LANES), lambda b, r, i: (b, r, i, 0)),
        ],
        out_shape=[
            jax.ShapeDtypeStruct((bsz, dil, length, w), _BF16),
            jax.ShapeDtypeStruct((bsz, dil, length, LANES), _F32),
        ],
        scratch_shapes=[pltpu.VMEM((DIL_HPG, 2, DIL_BAND, kw), _F32)],
        compiler_params=_cparams(3),
        name=f"dil_attention_g{group}",
    )(arr, arr, arr)


def _ssm_kernel(nb, u_ref, bw_ref, cw_ref, are_ref, aim_ref, d_ref, o_ref, bx_ref, st_ref):
    ti = pl.program_id(1)

    @pl.when(ti == 0)
    def _():
        st_ref[...] = jnp.zeros(st_ref.shape, _F32)

    for b in range(nb):
        bu = _dot(u_ref[b, 0], bw_ref[0])
        for sidx in range(2 * SSM_SLABS):
            bx_ref[sidx, b * SSM_PITCH:b * SSM_PITCH + SSM_TT, :] = bu[:, sidx * LANES:(sidx + 1) * LANES]

    a_re = [jnp.broadcast_to(are_ref[0, :, k * LANES:(k + 1) * LANES], (nb, LANES)) for k in range(SSM_SLABS)]
    a_im = [jnp.broadcast_to(aim_ref[0, :, k * LANES:(k + 1) * LANES], (nb, LANES)) for k in range(SSM_SLABS)]

    def step(t, carry):
        xr, xi = carry
        nr, ni = [], []
        for k in range(SSM_SLABS):
            rows = pl.ds(t, nb, stride=SSM_PITCH)
            br = bx_ref[k, rows, :]
            bi = bx_ref[SSM_SLABS + k, rows, :]
            r = a_re[k] * xr[k] - a_im[k] * xi[k] + br
            i = a_re[k] * xi[k] + a_im[k] * xr[k] + bi
            bx_ref[k, rows, :] = r
            bx_ref[SSM_SLABS + k, rows, :] = i
            nr.append(r)
            ni.append(i)
        return tuple(nr), tuple(ni)

    x0 = (tuple(st_ref[k] for k in range(SSM_SLABS)),
          tuple(st_ref[SSM_SLABS + k] for k in range(SSM_SLABS)))
    xr, xi = lax.fori_loop(0, SSM_TT, step, x0, unroll=8)
    for k in range(SSM_SLABS):
        st_ref[k] = xr[k]
        st_ref[SSM_SLABS + k] = xi[k]

    dskip = d_ref[0]
    for b in range(nb):
        xs = jnp.concatenate(
            [bx_ref[sidx, b * SSM_PITCH:b * SSM_PITCH + SSM_TT, :] for sidx in range(2 * SSM_SLABS)],
            axis=1).astype(_BF16)
        y = _dot(xs, cw_ref[0]) + dskip * u_ref[b, 0].astype(_F32)
        o_ref[b] = jax.nn.gelu(y).astype(_BF16)


def _ssm_call(proj_main, bw, cw, a_re, a_im, d_skip):
    bsz, _, s, _ = proj_main.shape
    u_block0 = COL_U // SSM_BLOCK_CH
    nstate = 2 * SSM_BLOCK_STATES
    return pl.pallas_call(
        functools.partial(_ssm_kernel, bsz),
        grid=(SSM_BLOCKS, s // SSM_TT),
        in_specs=[
            pl.BlockSpec((bsz, 1, SSM_TT, SSM_BLOCK_CH), lambda m, t: (0, 0, t, u_block0 + m)),
            pl.BlockSpec((1, SSM_BLOCK_CH, nstate), lambda m, t: (m, 0, 0)),
            pl.BlockSpec((1, nstate, SSM_BLOCK_CH), lambda m, t: (m, 0, 0)),
            pl.BlockSpec((1, 1, SSM_BLOCK_STATES), lambda m, t: (m, 0, 0)),
            pl.BlockSpec((1, 1, SSM_BLOCK_STATES), lambda m, t: (m, 0, 0)),
            pl.BlockSpec((1, 1, SSM_BLOCK_CH), lambda m, t: (m, 0, 0)),
        ],
        out_specs=pl.BlockSpec((bsz, SSM_TT, SSM_BLOCK_CH), lambda m, t: (0, t, m)),
        out_shape=jax.ShapeDtypeStruct((bsz, s, SSM_WIDTH), _BF16),
        scratch_shapes=[
            pltpu.VMEM((2 * SSM_SLABS, bsz * SSM_PITCH, LANES), _F32),
            pltpu.VMEM((2 * SSM_SLABS, bsz, LANES), _F32),
        ],
        compiler_params=_cparams(2),
        name="s5_scan",
    )(proj_main, bw, cw, a_re, a_im, d_skip)


def _ssm_weights(lam_re, lam_im, log_dt, b_re, b_im, c_re, c_im, d_skip):
    lam = lax.complex(lam_re.astype(_F32), lam_im.astype(_F32))
    dt = jnp.exp(log_dt.astype(_F32))[:, None]
    lam_bar = jnp.exp(lam * dt)
    b_bar = ((lam_bar - 1.0) / lam)[..., None] * lax.complex(b_re.astype(_F32), b_im.astype(_F32))
    gpb = SSM_GROUPS // SSM_BLOCKS
    eye = jnp.eye(gpb, dtype=_F32)

    def in_weights(part):
        w = part.reshape(SSM_BLOCKS, gpb, SSM_STATE, SSM_GROUP_SIZE)
        w = jnp.einsum("mgph,gk->mghkp", w, eye)
        return w.reshape(SSM_BLOCKS, SSM_BLOCK_CH, SSM_BLOCK_STATES)

    def out_weights(part):
        w = part.reshape(SSM_BLOCKS, gpb, SSM_GROUP_SIZE, SSM_STATE)
        w = jnp.einsum("mgkp,gj->mgpjk", w, eye)
        return w.reshape(SSM_BLOCKS, SSM_BLOCK_STATES, SSM_BLOCK_CH)

    bw = jnp.concatenate([in_weights(jnp.real(b_bar)), in_weights(jnp.imag(b_bar))], axis=2).astype(_BF16)
    cw = jnp.concatenate([out_weights(c_re.astype(_F32)), -out_weights(c_im.astype(_F32))], axis=1).astype(_BF16)
    a_re = jnp.real(lam_bar).reshape(SSM_BLOCKS, 1, SSM_BLOCK_STATES)
    a_im = jnp.imag(lam_bar).reshape(SSM_BLOCKS, 1, SSM_BLOCK_STATES)
    return bw, cw, a_re, a_im, d_skip.astype(_F32).reshape(SSM_BLOCKS, 1, SSM_BLOCK_CH)


def _to_token_order(ref, t, pt_ref):
    _, dil, _, width = ref.shape
    n = PERM_TILE // dil
    x = ref[0, :, t * n:(t + 1) * n, :].reshape(PERM_TILE, width)
    if x.dtype == _BF16:
        return _dot(pt_ref[...], x)
    hi = x.astype(_BF16)
    r1 = x - hi.astype(_F32)
    mid = r1.astype(_BF16)
    lo = (r1 - mid.astype(_F32)).astype(_BF16)
    res = _dot(pt_ref[...], jnp.concatenate([hi, mid, lo], axis=1))
    return (res[:, :width] + res[:, width:2 * width]) + res[:, 2 * width:]


def _merge_kernel(h_ref, ada_ref, ymla_ref, o0_ref, o1_ref, o2_ref, l0_ref, l1_ref, l2_ref, pt1_ref, pt2_ref,
                  ys_ref, gl_ref, wglu_ref, bglu_ref, wbr_ref, wout_ref, g_ref, b_ref, out_ref):
    gate = ada_ref[0, 5:6, :]
    d = h_ref.shape[-1]
    for t in range(h_ref.shape[1] // PERM_TILE):
        rows = slice(t * PERM_TILE, (t + 1) * PERM_TILE)
        h = h_ref[0, rows, :]
        o0 = o0_ref[0, 0, rows, :].astype(_F32)
        o1 = _to_token_order(o1_ref, t, pt1_ref)
        o2 = _to_token_order(o2_ref, t, pt2_ref)
        l0 = l0_ref[0, 0, rows, :]
        l1 = _to_token_order(l1_ref, t, pt1_ref)
        l2 = _to_token_order(l2_ref, t, pt2_ref)
        m = jnp.maximum(jnp.maximum(l0, l1), l2)
        e0, e1, e2 = jnp.exp(l0 - m), jnp.exp(l1 - m), jnp.exp(l2 - m)
        inv = 1.0 / (e0 + e1 + e2)
        w0, w1, w2 = e0 * inv, e1 * inv, e2 * inv
        heads = []
        for i in range(DIL_HPG):
            lanes = slice(i * DIL_HEAD_DIM, (i + 1) * DIL_HEAD_DIM)
            heads.append(w0[:, i:i + 1] * o0[:, lanes] + w1[:, i:i + 1] * o1[:, lanes]
                         + w2[:, i:i + 1] * o2[:, lanes])
        y_dil = jnp.concatenate(heads, axis=1).astype(_BF16)
        ys = ys_ref[0, rows, :]
        glu = jax.nn.sigmoid(_dot(ys, wglu_ref[...]) + bglu_ref[...])
        y_ssm = (ys.astype(_F32) * glu).astype(_BF16)
        merged = (jax.nn.sigmoid(gl_ref[0, rows, 0:d].astype(_F32)) * _dot(ymla_ref[0, rows, :], wbr_ref[0])
                  + jax.nn.sigmoid(gl_ref[0, rows, d:2 * d].astype(_F32)) * _dot(y_dil, wbr_ref[1])
                  + jax.nn.sigmoid(gl_ref[0, rows, 2 * d:3 * d].astype(_F32)) * _dot(y_ssm, wbr_ref[2]))
        out = _dot(merged.astype(_BF16), wout_ref[...])
        y = DEEPNORM_ALPHA * h + gate * out
        out_ref[0, rows, :] = _layer_norm(y, g_ref[...], b_ref[...])


def _merge_call(h, ada, y_mla, dil_o, dil_lse, y_s, proj_main, wglu, bglu, wbr, wout, ln_g, ln_b):
    bsz, s, d = h.shape
    tm = 2 * PERM_TILE
    tok = lambda b, i: (b, i, 0)
    grp = lambda b, i: (b, 0, i, 0)
    const2 = lambda b, i: (0, 0)
    bw = BRANCH_WIDTH
    d1, d2 = DIL_PAIRS[1][1], DIL_PAIRS[2][1]
    pt1 = jnp.asarray(_perm_matrix(d1).T, _BF16)
    pt2 = jnp.asarray(_perm_matrix(d2).T, _BF16)
    return pl.pallas_call(
        _merge_kernel,
        grid=(bsz, s // tm),
        in_specs=[
            pl.BlockSpec((1, tm, d), tok),
            pl.BlockSpec((1, 9, d), lambda b, i: (b, 0, 0)),
            pl.BlockSpec((1, tm, bw), tok),
            pl.BlockSpec((1, 1, tm, bw), grp),
            pl.BlockSpec((1, d1, tm // d1, bw), grp),
            pl.BlockSpec((1, d2, tm // d2, bw), grp),
            pl.BlockSpec((1, 1, tm, LANES), grp),
            pl.BlockSpec((1, d1, tm // d1, LANES), grp),
            pl.BlockSpec((1, d2, tm // d2, LANES), grp),
            pl.BlockSpec((PERM_TILE, PERM_TILE), const2),
            pl.BlockSpec((PERM_TILE, PERM_TILE), const2),
            pl.BlockSpec((1, tm, bw), tok),
            pl.BlockSpec((1, tm, N_BRANCH * d), lambda b, i: (b, i, COL_GATES // (N_BRANCH * d))),
            pl.BlockSpec((bw, bw), const2),
            pl.BlockSpec((1, bw), const2),
            pl.BlockSpec((N_BRANCH, bw, d), lambda b, i: (0, 0, 0)),
            pl.BlockSpec((d, d), const2),
            pl.BlockSpec((1, d), const2),
            pl.BlockSpec((1, d), const2),
        ],
        out_specs=pl.BlockSpec((1, tm, d), tok),
        out_shape=jax.ShapeDtypeStruct((bsz, s, d), _F32),
        compiler_params=_cparams(2),
        name="merge_sublayer",
    )(h, ada, y_mla, dil_o[0], dil_o[1], dil_o[2], dil_lse[0], dil_lse[1], dil_lse[2], pt1, pt2, y_s,
      proj_main, wglu, bglu, wbr, wout, ln_g, ln_b)


def _in_proj_weights(w_in, b_in):
    d = w_in.shape[0]
    o_kr = MLA_Q_RANK + MLA_KV_RANK
    o_dil = o_kr + MLA_ROPE_DIM
    o_u = o_dil + 3 * DIL_HEADS * DIL_HEAD_DIM
    o_g = o_u + SSM_WIDTH

    def dil_cols(arr, part, group):
        start = o_dil + part * DIL_HEADS * DIL_HEAD_DIM + group * DIL_GROUP_WIDTH
        cols = arr[..., start:start + DIL_GROUP_WIDTH]
        return cols * (DIL_SCALE * LOG2E) if part == 0 else cols

    def main(arr):
        pad = jnp.zeros(arr.shape[:-1] + (COL_Q1 - o_dil,), arr.dtype)
        return jnp.concatenate([arr[..., :o_dil], pad, dil_cols(arr, 0, 0), dil_cols(arr, 1, 0),
                                dil_cols(arr, 2, 0), arr[..., o_u:o_g], arr[..., o_g:]], axis=-1)

    def group(arr, g):
        return jnp.concatenate([dil_cols(arr, 0, g), dil_cols(arr, 1, g), dil_cols(arr, 2, g)], axis=-1)

    b2 = b_in.astype(_F32).reshape(1, -1)
    return ((main(w_in).astype(_BF16), main(b2)),
            (group(w_in, 1).astype(_BF16), group(b2, 1)),
            (group(w_in, 2).astype(_BF16), group(b2, 2)))


def _mla_weights(w_qb, w_kvb):
    qd = MLA_NOPE_DIM + MLA_ROPE_DIM
    half = MLA_ROPE_DIM // 2
    wq = w_qb.reshape(MLA_Q_RANK, MLA_HEADS, qd)
    zeros = jnp.zeros((MLA_Q_RANK, MLA_HEADS, MLA_HEAD_PAD - qd), w_qb.dtype)
    zn = jnp.zeros((MLA_Q_RANK, MLA_HEADS, MLA_NOPE_DIM), w_qb.dtype)
    wq_pad = jnp.concatenate([wq, zeros], axis=-1).reshape(MLA_Q_RANK, MLA_QK_WIDTH)
    t1 = wq[..., MLA_NOPE_DIM:MLA_NOPE_DIM + half]
    t2 = wq[..., MLA_NOPE_DIM + half:]
    wq_rot = jnp.concatenate([zn, -t2, t1, zeros], axis=-1).reshape(MLA_Q_RANK, MLA_QK_WIDTH)
    wkv = w_kvb.reshape(MLA_KV_RANK, MLA_HEADS, MLA_NOPE_DIM + MLA_V_DIM)
    zk = jnp.zeros((MLA_KV_RANK, MLA_HEADS, MLA_HEAD_PAD - MLA_NOPE_DIM), w_kvb.dtype)
    wk = jnp.concatenate([wkv[..., :MLA_NOPE_DIM], zk], axis=-1).reshape(MLA_KV_RANK, MLA_QK_WIDTH)
    wv = wkv[..., MLA_NOPE_DIM:].reshape(MLA_KV_RANK, MLA_HEADS // 2, 2, MLA_V_DIM)
    zv = jnp.zeros_like(wv[:, :, 0])
    wv_pad = jnp.concatenate([wv[:, :, 0], zv, zv, wv[:, :, 1]], axis=-1).reshape(MLA_KV_RANK, MLA_QK_WIDTH)
    return wq_pad.astype(_BF16), wq_rot.astype(_BF16), wk.astype(_BF16), wv_pad.astype(_BF16)


def _mla_ones_row():
    ones = np.zeros((1, MLA_QK_WIDTH), np.float32)
    for hd in range(MLA_HEADS):
        ones[0, hd * MLA_HEAD_PAD + (MLA_V_DIM if hd % 2 == 0 else 0)] = 1.0
    return jnp.asarray(ones)


def _rope_key_placement():
    half = MLA_ROPE_DIM // 2
    pk = np.zeros((LANES, MLA_QK_WIDTH), np.float32)
    pkr = np.zeros((LANES, MLA_QK_WIDTH), np.float32)
    for hd in range(MLA_HEADS):
        base = hd * MLA_HEAD_PAD + MLA_NOPE_DIM
        for i in range(MLA_ROPE_DIM):
            pk[i, base + i] = 1.0
        for i in range(half):
            pkr[half + i, base + i] = -1.0
            pkr[i, base + half + i] = 1.0
    return jnp.asarray(pk, _BF16), jnp.asarray(pkr, _BF16)


def _ffn_weights(w1, w3, w2):
    return w1.astype(_BF16), w3.astype(_BF16), w2.astype(_BF16)


def kernel(x, c, positions, w_ada, b_ada, ln_g, ln_b, ffn_w1, ffn_w3, ffn_w2, w_in, b_in, mla_q_norm, mla_kv_norm, mla_w_qb, mla_w_kvb, ssm_lambda_re, ssm_lambda_im, ssm_log_dt, ssm_b_re, ssm_b_im, ssm_c_re, ssm_c_im, ssm_d, ssm_w_glu, ssm_b_glu, w_br, w_out):
    bsz, s, d = x.shape
    assert d == D_MODEL and s % (DIL_PAIRS[2][1] * DIL_BAND) == 0, x.shape
    ada_all = _ada_call(c, w_ada, b_ada).reshape(DEPTH, bsz, 9, d)
    cos, sin = _rope_call(positions)
    pk, pkr = _rope_key_placement()
    ones_row = _mla_ones_row()
    h = x
    for l in range(DEPTH):
        ada = ada_all[l]
        h = _ffn_call(h, ada, 0, *_ffn_weights(ffn_w1[l, 0], ffn_w3[l, 0], ffn_w2[l, 0]),
                      ln_g[l, 0].reshape(1, d), ln_b[l, 0].reshape(1, d))
        (w_main, b_main), (w_g1, b_g1), (w_g2, b_g2) = _in_proj_weights(w_in[l], b_in[l])
        proj_main, proj_g1, proj_g2 = _proj_call(
            h, ada, w_main, b_main, [(DIL_PAIRS[1][1], w_g1, b_g1), (DIL_PAIRS[2][1], w_g2, b_g2)])
        proj_main4 = proj_main.reshape(bsz, 1, s, N_MAIN)
        wq, wqr, wk, wv = _mla_weights(mla_w_qb[l], mla_w_kvb[l])
        q, k, v = _mla_prep_call(proj_main, cos, sin, mla_q_norm[l].reshape(1, -1),
                                 mla_kv_norm[l].reshape(1, -1), wq, wqr, wk, wv, pk, pkr, ones_row)
        y_mla = _mla_attn_call(q, k, v)
        o0, lse0 = _dil_attn_call(proj_main4, COL_Q1 // DIL_GROUP_WIDTH, 0, DIL_PAIRS[0][1], 512, 1)
        o1, lse1 = _dil_attn_call(proj_g1, 0, 1, DIL_PAIRS[1][1], 512, 1)
        o2, lse2 = _dil_attn_call(proj_g2, 0, 2, DIL_PAIRS[2][1], 256, 4)
        y_s = _ssm_call(proj_main4, *_ssm_weights(ssm_lambda_re[l], ssm_lambda_im[l], ssm_log_dt[l],
                                                 ssm_b_re[l], ssm_b_im[l], ssm_c_re[l], ssm_c_im[l], ssm_d[l]))
        h = _merge_call(h, ada, y_mla, (o0, o1, o2), (lse0, lse1, lse2), y_s, proj_main,
                        ssm_w_glu[l].astype(_BF16), ssm_b_glu[l].reshape(1, -1), w_br[l].astype(_BF16),
                        w_out[l].astype(_BF16), ln_g[l, 1].reshape(1, d), ln_b[l, 1].reshape(1, d))
        h = _ffn_call(h, ada, 2, *_ffn_weights(ffn_w1[l, 1], ffn_w3[l, 1], ffn_w2[l, 1]),
                      ln_g[l, 2].reshape(1, d), ln_b[l, 2].reshape(1, d))
    return h
```

```python
import functools
import math

import numpy as np
import jax
import jax.numpy as jnp
from jax import lax
from jax.experimental import pallas as pl
from jax.experimental.pallas import tpu as pltpu

D_MODEL = 1024
DEPTH = 4
D_FF = 2816
MLA_HEADS = 8
MLA_Q_RANK = 384
MLA_KV_RANK = 256
MLA_NOPE_DIM = 64
MLA_ROPE_DIM = 32
MLA_V_DIM = 64
ROPE_THETA = 10000.0
DIL_PAIRS = ((128, 1), (512, 4), (2048, 16))
DIL_HPG = 4
DIL_HEADS = DIL_HPG * len(DIL_PAIRS)
DIL_HEAD_DIM = 128
DIL_BAND = 128
SSM_WIDTH = 512
SSM_GROUP_SIZE = 16
SSM_GROUPS = SSM_WIDTH // SSM_GROUP_SIZE
SSM_STATE = 64
N_BRANCH = 3
BRANCH_WIDTH = 512
DEEPNORM_ALPHA = (2 * DEPTH) ** 0.25
MACARON_WEIGHT = 0.5
LN_EPS = 1e-5
RMS_EPS = 1e-6

LANES = 128
V7X_VMEM_LIMIT_BYTES = 56 * 1024 * 1024

FF_CHUNK = 256
FFN_ROWS = 512
N_FF_CHUNKS = D_FF // FF_CHUNK
MLA_HEAD_PAD = 128
MLA_QK_WIDTH = MLA_HEADS * MLA_HEAD_PAD
MLA_V_WIDTH = MLA_HEADS * MLA_V_DIM
MLA_SCALE = (MLA_NOPE_DIM + MLA_ROPE_DIM) ** -0.5
DIL_SCALE = DIL_HEAD_DIM ** -0.5
DIL_GROUP_WIDTH = DIL_HPG * DIL_HEAD_DIM
MLA_A_WIDTH = 768
COL_Q1 = 1024
COL_U = COL_Q1 + 3 * DIL_GROUP_WIDTH
COL_GATES = COL_U + SSM_WIDTH
N_MAIN = COL_GATES + N_BRANCH * D_MODEL
PROJ_CHUNK = 512
PERM_TILE = 256
LOG2E = math.log2(math.e)
MLA_ROW_CHUNK = 64
MLA_HEADS_PER_STEP = 4
SSM_BLOCKS = 4
SSM_BLOCK_CH = SSM_WIDTH // SSM_BLOCKS
SSM_BLOCK_STATES = SSM_GROUPS * SSM_STATE // SSM_BLOCKS
SSM_SLABS = SSM_BLOCK_STATES // LANES
SSM_TT = 128
SSM_PITCH = 132

_F32 = jnp.float32
_BF16 = jnp.bfloat16
_NEG = -1e30


def _cparams(n_axes):
    return pltpu.CompilerParams(
        dimension_semantics=("arbitrary",) * n_axes,
        vmem_limit_bytes=V7X_VMEM_LIMIT_BYTES,
    )


def _resident(block_shape, index_map):
    return pl.BlockSpec(block_shape, index_map, pipeline_mode=pl.Buffered(1))


def _layer_norm(y, g, b):
    mu = jnp.mean(y, axis=-1, keepdims=True)
    yc = y - mu
    var = jnp.mean(yc * yc, axis=-1, keepdims=True)
    return yc * lax.rsqrt(var + LN_EPS) * g + b


def _dot(a, b):
    return jnp.dot(a, b, preferred_element_type=_F32)


def _dot_nt(a, b):
    return lax.dot_general(a, b, (((1,), (1,)), ((), ())), preferred_element_type=_F32)


def _ada_kernel(c_ref, w_ref, b_ref, o_ref):
    c = c_ref[...]
    cond = (c * jax.nn.sigmoid(c)).astype(_BF16)
    o_ref[0] = _dot(cond, w_ref[0].astype(_BF16)) + b_ref[0]


def _ada_call(c, w_ada, b_ada):
    depth, d, n = w_ada.shape
    bsz = c.shape[0]
    tn = 1536
    return pl.pallas_call(
        _ada_kernel,
        grid=(depth, n // tn),
        in_specs=[
            pl.BlockSpec((bsz, d), lambda l, j: (0, 0)),
            pl.BlockSpec((1, d, tn), lambda l, j: (l, 0, j)),
            pl.BlockSpec((1, 1, tn), lambda l, j: (l, 0, j)),
        ],
        out_specs=pl.BlockSpec((1, bsz, tn), lambda l, j: (l, 0, j)),
        out_shape=jax.ShapeDtypeStruct((depth, bsz, n), _F32),
        compiler_params=_cparams(2),
        name="ada",
    )(c, w_ada, b_ada.reshape(depth, 1, n))


def _rope_freq_lanes():
    half = MLA_ROPE_DIM // 2
    inv_freq = np.power(np.float32(ROPE_THETA), -np.arange(half, dtype=np.float32) / np.float32(half))
    f = np.zeros((1, MLA_HEAD_PAD), np.float32)
    f[0, MLA_NOPE_DIM:MLA_NOPE_DIM + half] = inv_freq
    f[0, MLA_NOPE_DIM + half:MLA_NOPE_DIM + 2 * half] = inv_freq
    return f


def _rope_kernel(pos_ref, f_ref, cos_ref, sin_ref):
    ang = pos_ref[0].astype(_F32) * f_ref[...]
    cos_ref[0] = jnp.cos(ang)
    sin_ref[0] = jnp.sin(ang)


def _rope_call(positions):
    bsz, s = positions.shape
    tm = min(s, 512)
    out = jax.ShapeDtypeStruct((bsz, s, MLA_HEAD_PAD), _F32)
    return pl.pallas_call(
        _rope_kernel,
        grid=(bsz, s // tm),
        in_specs=[
            pl.BlockSpec((1, tm, 1), lambda b, i: (b, i, 0)),
            pl.BlockSpec((1, MLA_HEAD_PAD), lambda b, i: (0, 0)),
        ],
        out_specs=[pl.BlockSpec((1, tm, MLA_HEAD_PAD), lambda b, i: (b, i, 0))] * 2,
        out_shape=[out, out],
        compiler_params=_cparams(2),
        name="rope_tables",
    )(positions.reshape(bsz, s, 1), jnp.asarray(_rope_freq_lanes()))


def _ffn_kernel(sub, h_ref, ada_ref, w1_ref, w3_ref, w2_ref, g_ref, b_ref, o_ref):
    shift = ada_ref[0, 3 * sub + 0:3 * sub + 1, :]
    scale = ada_ref[0, 3 * sub + 1:3 * sub + 2, :]
    gate = ada_ref[0, 3 * sub + 2:3 * sub + 3, :]
    for r0 in range(0, h_ref.shape[1], FFN_ROWS):
        rows = slice(r0, r0 + FFN_ROWS)
        h = h_ref[0, rows, :]
        xm = (h * (1.0 + scale) + shift).astype(_BF16)
        acc = jnp.zeros(h.shape, _F32)
        for j in range(N_FF_CHUNKS):
            cols = slice(j * FF_CHUNK, (j + 1) * FF_CHUNK)
            a = _dot(xm, w1_ref[:, cols])
            b = _dot(xm, w3_ref[:, cols])
            g = (a * jax.nn.sigmoid(a) * b).astype(_BF16)
            acc = acc + _dot(g, w2_ref[cols, :])
        y = DEEPNORM_ALPHA * h + MACARON_WEIGHT * gate * acc
        o_ref[0, rows, :] = _layer_norm(y, g_ref[...], b_ref[...])


def _ffn_call(h, ada, sub, w1, w3, w2, ln_g, ln_b):
    bsz, s, d = h.shape
    tm = min(s, 2 * FFN_ROWS)
    return pl.pallas_call(
        functools.partial(_ffn_kernel, sub),
        grid=(bsz, s // tm),
        in_specs=[
            pl.BlockSpec((1, tm, d), lambda b, i: (b, i, 0)),
            pl.BlockSpec((1, 9, d), lambda b, i: (b, 0, 0)),
            _resident((d, D_FF), lambda b, i: (0, 0)),
            _resident((d, D_FF), lambda b, i: (0, 0)),
            _resident((D_FF, d), lambda b, i: (0, 0)),
            pl.BlockSpec((1, d), lambda b, i: (0, 0)),
            pl.BlockSpec((1, d), lambda b, i: (0, 0)),
        ],
        out_specs=pl.BlockSpec((1, tm, d), lambda b, i: (b, i, 0)),
        out_shape=jax.ShapeDtypeStruct((bsz, s, d), _F32),
        compiler_params=_cparams(2),
        name="ffn_sublayer",
    )(h, ada, w1, w3, w2, ln_g, ln_b)


def _perm_matrix(dil):
    n = PERM_TILE // dil
    p = np.zeros((PERM_TILE, PERM_TILE), np.float32)
    for r in range(dil):
        for i in range(n):
            p[r * n + i, i * dil + r] = 1.0
    return p


def _proj_kernel(tm, dils, x_ref, ada_ref, wm_ref, bm_ref, *rest):
    ngrp = len(dils)
    grp_in = rest[:2 * ngrp]
    om_ref = rest[2 * ngrp]
    grp_out = rest[2 * ngrp + 1:2 * ngrp + 1 + ngrp]
    xs_ref, xp_ref = rest[2 * ngrp + 1 + ngrp:]
    x = x_ref[0]
    shift = ada_ref[0, 3:4, :]
    scale = ada_ref[0, 4:5, :]
    xm_f32 = x * (1.0 + scale) + shift
    xm = xm_f32.astype(_BF16)
    n_main = wm_ref.shape[1]
    for n0 in range(0, n_main, PROJ_CHUNK):
        n1 = min(n0 + PROJ_CHUNK, n_main)
        om_ref[0, :, n0:n1] = (_dot(xm, wm_ref[:, n0:n1]) + bm_ref[:, n0:n1]).astype(_BF16)
    nslab = x.shape[1] // LANES
    for k in range(nslab):
        xs_ref[k] = xm_f32[:, k * LANES:(k + 1) * LANES]
    for g, dil in enumerate(dils):
        w_ref, b_ref = grp_in[2 * g:2 * g + 2]
        o_ref = grp_out[g]
        n = tm // dil
        n_out = w_ref.shape[1]
        for r in range(dil):
            for k in range(nslab):
                xp_ref[g, r * n:(r + 1) * n, k * LANES:(k + 1) * LANES] = (
                    xs_ref[k, pl.ds(r, n, stride=dil), :].astype(_BF16))
        xp = xp_ref[g]
        for n0 in range(0, n_out, PROJ_CHUNK):
            n1 = min(n0 + PROJ_CHUNK, n_out)
            res = (_dot(xp, w_ref[:, n0:n1]) + b_ref[:, n0:n1]).astype(_BF16)
            for r in range(dil):
                o_ref[0, r, :, n0:n1] = res[r * n:(r + 1) * n]


def _proj_call(h, ada, w_main, b_main, groups):
    bsz, s, d = h.shape
    tm = min(s, 512)
    dils = tuple(g[0] for g in groups)
    const = lambda b, i: (0, 0)
    in_specs = [
        pl.BlockSpec((1, tm, d), lambda b, i: (b, i, 0)),
        pl.BlockSpec((1, 9, d), lambda b, i: (b, 0, 0)),
        _resident(w_main.shape, const),
        pl.BlockSpec(b_main.shape, const),
    ]
    args = [h, ada, w_main, b_main]
    out_specs = [pl.BlockSpec((1, tm, w_main.shape[1]), lambda b, i: (b, i, 0))]
    out_shape = [jax.ShapeDtypeStruct((bsz, s, w_main.shape[1]), _BF16)]
    for dil, w, bias in groups:
        in_specs += [_resident(w.shape, const), pl.BlockSpec(bias.shape, const)]
        args += [w, bias]
        out_specs.append(pl.BlockSpec((1, dil, tm // dil, w.shape[1]), lambda b, i: (b, 0, i, 0)))
        out_shape.append(jax.ShapeDtypeStruct((bsz, dil, s // dil, w.shape[1]), _BF16))
    return pl.pallas_call(
        functools.partial(_proj_kernel, tm, dils),
        grid=(bsz, s // tm),
        in_specs=in_specs,
        out_specs=out_specs,
        out_shape=out_shape,
        scratch_shapes=[pltpu.VMEM((d // LANES, tm, LANES), _F32), pltpu.VMEM((len(groups), tm, d), _BF16)],
        compiler_params=_cparams(2),
        name="in_proj",
    )(*args)


def _mla_prep_kernel(a_ref, cos_ref, sin_ref, qg_ref, kvg_ref, wq_ref, wqr_ref, wk_ref, wv_ref,
                     pk_ref, pkr_ref, ones_ref, q_ref, k_ref, v_ref):
    a = a_ref[0]
    qa = a[:, :MLA_Q_RANK].astype(_F32)
    kva = a[:, MLA_Q_RANK:MLA_Q_RANK + MLA_KV_RANK].astype(_F32)
    kr = a[:, MLA_Q_RANK + MLA_KV_RANK:]
    qn = (qa * lax.rsqrt(jnp.mean(qa * qa, axis=-1, keepdims=True) + RMS_EPS) * qg_ref[...]).astype(_BF16)
    kvn = (kva * lax.rsqrt(jnp.mean(kva * kva, axis=-1, keepdims=True) + RMS_EPS) * kvg_ref[...]).astype(_BF16)
    cos = jnp.concatenate([cos_ref[0]] * 2, axis=1) * (MLA_SCALE * LOG2E)
    sin = jnp.concatenate([sin_ref[0]] * 2, axis=1) * (MLA_SCALE * LOG2E)
    kcos = jnp.concatenate([cos_ref[0]] * 2, axis=1)
    ksin = jnp.concatenate([sin_ref[0]] * 2, axis=1)
    for c in range(MLA_HEADS // 2):
        sl = slice(2 * c * MLA_HEAD_PAD, 2 * (c + 1) * MLA_HEAD_PAD)
        q = _dot(qn, wq_ref[:, sl])
        qrot = _dot(qn, wqr_ref[:, sl])
        q_ref[0, :, sl] = (q * cos + qrot * sin).astype(_BF16)
        k = _dot(kvn, wk_ref[:, sl]) + _dot(kr, pk_ref[:, sl])
        krot = _dot(kr, pkr_ref[:, sl])
        k_ref[0, :, sl] = (k * kcos + krot * ksin).astype(_BF16)
        v_ref[0, :, sl] = (_dot(kvn, wv_ref[:, sl]) + ones_ref[:, sl]).astype(_BF16)


def _mla_prep_call(proj_main, cos, sin, qg, kvg, wq, wqr, wk, wv, pk, pkr, ones_row):
    bsz, s, _ = proj_main.shape
    tm = min(s, 512)
    const = lambda b, i: (0, 0)
    return pl.pallas_call(
        _mla_prep_kernel,
        grid=(bsz, s // tm),
        in_specs=[
            pl.BlockSpec((1, tm, MLA_A_WIDTH), lambda b, i: (b, i, 0)),
            pl.BlockSpec((1, tm, MLA_HEAD_PAD), lambda b, i: (b, i, 0)),
            pl.BlockSpec((1, tm, MLA_HEAD_PAD), lambda b, i: (b, i, 0)),
            pl.BlockSpec((1, MLA_Q_RANK), const),
            pl.BlockSpec((1, MLA_KV_RANK), const),
            pl.BlockSpec((MLA_Q_RANK, MLA_QK_WIDTH), const),
            pl.BlockSpec((MLA_Q_RANK, MLA_QK_WIDTH), const),
            pl.BlockSpec((MLA_KV_RANK, MLA_QK_WIDTH), const),
            pl.BlockSpec((MLA_KV_RANK, MLA_QK_WIDTH), const),
            pl.BlockSpec((LANES, MLA_QK_WIDTH), const),
            pl.BlockSpec((LANES, MLA_QK_WIDTH), const),
            pl.BlockSpec((1, MLA_QK_WIDTH), const),
        ],
        out_specs=[pl.BlockSpec((1, tm, MLA_QK_WIDTH), lambda b, i: (b, i, 0))] * 3,
        out_shape=[jax.ShapeDtypeStruct((bsz, s, MLA_QK_WIDTH), _BF16)] * 3,
        compiler_params=_cparams(2),
        name="mla_prep",
    )(proj_main, cos, sin, qg, kvg, wq, wqr, wk, wv, pk, pkr, ones_row)


def _mla_attn_kernel(tq, q_ref, k_ref, v_ref, o_ref, s_ref, p_ref, m_ref, acc_ref):
    nh = MLA_HEADS_PER_STEP
    half = tq // 2
    qi = pl.program_id(2)
    row = lax.broadcasted_iota(jnp.int32, (tq, tq), 0)
    col = lax.broadcasted_iota(jnp.int32, (tq, tq), 1)
    causal = col <= row
    lanes = [slice(e * MLA_HEAD_PAD, (e + 1) * MLA_HEAD_PAD) for e in range(nh)]
    qs = [q_ref[0, :, lanes[e]] for e in range(nh)]

    def scores(j, slot):
        start = pl.multiple_of(j * tq, tq)
        for e in range(nh):
            s_ref[slot, e] = _dot_nt(qs[e], k_ref[0, pl.ds(start, tq), lanes[e]])

    def softmax_pv(j, slot, masked):
        start = pl.multiple_of(j * tq, tq)
        for e in range(nh):
            alphas = []
            for c in range(tq // MLA_ROW_CHUNK):
                rows = slice(c * MLA_ROW_CHUNK, (c + 1) * MLA_ROW_CHUNK)
                cols = slice(0, half if masked and (c + 1) * MLA_ROW_CHUNK <= half else tq)
                s = s_ref[slot, e, rows, cols]
                if masked:
                    s = jnp.where(causal[rows, cols], s, _NEG)
                m_old = m_ref[e, rows, :]
                m_c = jnp.maximum(m_old, jnp.max(s, axis=-1, keepdims=True))
                p_ref[e, rows, cols] = jnp.exp2(s - m_c).astype(_BF16)
                m_ref[e, rows, :] = m_c
                alphas.append(jnp.exp2(m_old - m_c))
            if masked:
                pv = jnp.concatenate([
                    _dot(p_ref[e, :half, :half], v_ref[0, pl.ds(start, half), lanes[e]]),
                    _dot(p_ref[e, half:, :], v_ref[0, pl.ds(start, tq), lanes[e]])], axis=0)
            else:
                pv = _dot(p_ref[e], v_ref[0, pl.ds(start, tq), lanes[e]])
            acc_ref[e] = jnp.concatenate(alphas, axis=0) * acc_ref[e] + pv

    def pair(jj, carry):
        t = 2 * jj
        scores(t + 1, 1)
        softmax_pv(t, 0, False)
        scores(t + 2, 0)
        softmax_pv(t + 1, 1, False)
        return carry

    m_ref[...] = jnp.full(m_ref.shape, _NEG, _F32)
    acc_ref[...] = jnp.zeros(acc_ref.shape, _F32)
    scores(0, 0)
    lax.fori_loop(0, qi // 2, pair, 0)

    @pl.when(qi % 2 == 1)
    def _():
        scores(qi, 1)
        softmax_pv(qi - 1, 0, False)
        softmax_pv(qi, 1, True)

    @pl.when(qi % 2 == 0)
    def _():
        softmax_pv(qi, 0, True)

    lane = lax.broadcasted_iota(jnp.int32, (tq, MLA_HEAD_PAD), 1)
    for e2 in range(nh // 2):
        acc0, acc1 = acc_ref[2 * e2], acc_ref[2 * e2 + 1]
        out0 = acc0 * (1.0 / acc0[:, MLA_V_DIM:MLA_V_DIM + 1])
        out1 = acc1 * (1.0 / acc1[:, 0:1])
        o_ref[0, :, lanes[e2]] = jnp.where(lane < MLA_V_DIM, out0, out1).astype(_BF16)


def _mla_attn_call(q, k, v):
    bsz, s, _ = q.shape
    tq = min(s, 512)
    nh = MLA_HEADS_PER_STEP
    qk_w = nh * MLA_HEAD_PAD
    return pl.pallas_call(
        functools.partial(_mla_attn_kernel, tq),
        grid=(bsz, MLA_HEADS // nh, s // tq),
        in_specs=[
            pl.BlockSpec((1, tq, qk_w), lambda b, h, i: (b, i, h)),
            pl.BlockSpec((1, s, qk_w), lambda b, h, i: (b, 0, h)),
            pl.BlockSpec((1, s, qk_w), lambda b, h, i: (b, 0, h)),
        ],
        out_specs=pl.BlockSpec((1, tq, nh * MLA_V_DIM), lambda b, h, i: (b, i, h)),
        out_shape=jax.ShapeDtypeStruct((bsz, s, MLA_V_WIDTH), _BF16),
        scratch_shapes=[pltpu.VMEM((2, nh, tq, tq), _F32), pltpu.VMEM((nh, tq, tq), _BF16),
                        pltpu.VMEM((nh, tq, 1), _F32), pltpu.VMEM((nh, tq, MLA_HEAD_PAD), _F32)],
        compiler_params=_cparams(3),
        name="mla_attention",
    )(q, k, v)


def _alibi_slope(head):
    return float(np.exp2(np.float32(-8.0) * (np.float32(head) + np.float32(1.0)) / np.float32(DIL_HEADS)))


def _dil_attn_kernel(tq, kw, single_tile, nres, group, dil, q_ref, k_ref, v_ref, o_ref, lse_ref, bias_ref):
    qi = pl.program_id(2)
    first_step = (pl.program_id(0) == 0) & (pl.program_id(1) == 0) & (qi == 0)
    nb = DIL_BAND

    @pl.when(first_step)
    def _():
        row = lax.broadcasted_iota(jnp.int32, (nb, kw), 0)
        col = lax.broadcasted_iota(jnp.int32, (nb, kw), 1)
        for variant, rel in ((0, nb + row - col), (1, row - col)):
            ok = (rel >= 0) & (rel <= DIL_BAND)
            relf = rel.astype(_F32)
            for i in range(DIL_HPG):
                slope = _alibi_slope(group * DIL_HPG + i) * dil * LOG2E
                bias_ref[i, variant] = jnp.where(ok, -slope * relf, _NEG)

    q0 = pl.multiple_of(qi * tq, tq)
    lane = lax.broadcasted_iota(jnp.int32, (nb, LANES), 1)
    for rr in range(nres):
        for sb in range(tq // nb):
            rows = slice(sb * nb, (sb + 1) * nb)
            if sb == 0:
                start = pl.multiple_of(jnp.maximum(q0 - nb, 0), nb)
            else:
                start = pl.multiple_of(q0 + (sb - 1) * nb, nb)
            m_all = jnp.zeros((nb, LANES), _F32)
            l_all = jnp.ones((nb, LANES), _F32)
            for i in range(DIL_HPG):
                lanes = slice(i * DIL_HEAD_DIM, (i + 1) * DIL_HEAD_DIM)
                if sb > 0:
                    bias = bias_ref[i, 0]
                elif single_tile:
                    bias = bias_ref[i, 1]
                else:
                    bias = jnp.where(qi == 0, bias_ref[i, 1], bias_ref[i, 0])
                s = _dot_nt(q_ref[0, rr, rows, lanes], k_ref[0, rr, pl.ds(start, kw), lanes]) + bias
                m = jnp.max(s, axis=-1, keepdims=True)
                e = jnp.exp2(s - m)
                l = jnp.sum(e, axis=-1, keepdims=True)
                o = _dot(e.astype(_BF16), v_ref[0, rr, pl.ds(start, kw), lanes])
                o_ref[0, rr, rows, lanes] = (o * (1.0 / l)).astype(_BF16)
                m_all = jnp.where(lane == i, m, m_all)
                l_all = jnp.where(lane == i, l, l_all)
            lse_ref[0, rr, rows, :] = (m_all + jnp.log2(l_all)) * (1.0 / LOG2E)


def _dil_attn_call(arr, col_block0, group, dil, tq, nres):
    bsz, _, length, _ = arr.shape
    w = DIL_GROUP_WIDTH
    tq = min(tq, length)
    kw = min(2 * DIL_BAND, length)
    nres = min(nres, dil)
    return pl.pallas_call(
        functools.partial(_dil_attn_kernel, tq, kw, length == tq, nres, group, dil),
        grid=(bsz, dil // nres, length // tq),
        in_specs=[
            pl.BlockSpec((1, nres, tq, w), lambda b, r, i: (b, r, i, col_block0)),
            pl.BlockSpec((1, nres, length, w), lambda b, r, i: (b, r, 0, col_block0 + 1)),
            pl.BlockSpec((1, nres, length, w), lambda b, r, i: (b, r, 0, col_block0 + 2)),
        ],
        out_specs=[
            pl.BlockSpec((1, nres, tq, w), lambda b, r, i: (b, r, i, 0)),
            pl.BlockSpec((1, nres, tq, LANES), lambda b, r, i: (b, r, i, 0)),
        ],
        out_shape=[
            jax.ShapeDtypeStruct((bsz, dil, length, w), _BF16),
            jax.ShapeDtypeStruct((bsz, dil, length, LANES), _F32),
        ],
        scratch_shapes=[pltpu.VMEM((DIL_HPG, 2, DIL_BAND, kw), _F32)],
        compiler_params=_cparams(3),
        name=f"dil_attention_g{group}",
    )(arr, arr, arr)


def _ssm_kernel(nb, u_ref, bw_ref, cw_ref, are_ref, aim_ref, d_ref, o_ref, bx_ref, st_ref):
    ti = pl.program_id(1)

    @pl.when(ti == 0)
    def _():
        st_ref[...] = jnp.zeros(st_ref.shape, _F32)

    for b in range(nb):
        bu = _dot(u_ref[b, 0], bw_ref[0])
        for sidx in range(2 * SSM_SLABS):
            bx_ref[sidx, b * SSM_PITCH:b * SSM_PITCH + SSM_TT, :] = bu[:, sidx * LANES:(sidx + 1) * LANES]

    a_re = [jnp.broadcast_to(are_ref[0, :, k * LANES:(k + 1) * LANES], (nb, LANES)) for k in range(SSM_SLABS)]
    a_im = [jnp.broadcast_to(aim_ref[0, :, k * LANES:(k + 1) * LANES], (nb, LANES)) for k in range(SSM_SLABS)]

    def step(t, carry):
        xr, xi = carry
        nr, ni = [], []
        for k in range(SSM_SLABS):
            rows = pl.ds(t, nb, stride=SSM_PITCH)
            br = bx_ref[k, rows, :]
            bi = bx_ref[SSM_SLABS + k, rows, :]
            r = a_re[k] * xr[k] - a_im[k] * xi[k] + br
            i = a_re[k] * xi[k] + a_im[k] * xr[k] + bi
            bx_ref[k, rows, :] = r
            bx_ref[SSM_SLABS + k, rows, :] = i
            nr.append(r)
            ni.append(i)
        return tuple(nr), tuple(ni)

    x0 = (tuple(st_ref[k] for k in range(SSM_SLABS)),
          tuple(st_ref[SSM_SLABS + k] for k in range(SSM_SLABS)))
    xr, xi = lax.fori_loop(0, SSM_TT, step, x0, unroll=8)
    for k in range(SSM_SLABS):
        st_ref[k] = xr[k]
        st_ref[SSM_SLABS + k] = xi[k]

    dskip = d_ref[0]
    for b in range(nb):
        xs = jnp.concatenate(
            [bx_ref[sidx, b * SSM_PITCH:b * SSM_PITCH + SSM_TT, :] for sidx in range(2 * SSM_SLABS)],
            axis=1).astype(_BF16)
        y = _dot(xs, cw_ref[0]) + dskip * u_ref[b, 0].astype(_F32)
        o_ref[b] = jax.nn.gelu(y).astype(_BF16)


def _ssm_call(proj_main, bw, cw, a_re, a_im, d_skip):
    bsz, _, s, _ = proj_main.shape
    u_block0 = COL_U // SSM_BLOCK_CH
    nstate = 2 * SSM_BLOCK_STATES
    return pl.pallas_call(
        functools.partial(_ssm_kernel, bsz),
        grid=(SSM_BLOCKS, s // SSM_TT),
        in_specs=[
            pl.BlockSpec((bsz, 1, SSM_TT, SSM_BLOCK_CH), lambda m, t: (0, 0, t, u_block0 + m)),
            pl.BlockSpec((1, SSM_BLOCK_CH, nstate), lambda m, t: (m, 0, 0)),
            pl.BlockSpec((1, nstate, SSM_BLOCK_CH), lambda m, t: (m, 0, 0)),
            pl.BlockSpec((1, 1, SSM_BLOCK_STATES), lambda m, t: (m, 0, 0)),
            pl.BlockSpec((1, 1, SSM_BLOCK_STATES), lambda m, t: (m, 0, 0)),
            pl.BlockSpec((1, 1, SSM_BLOCK_CH), lambda m, t: (m, 0, 0)),
        ],
        out_specs=pl.BlockSpec((bsz, SSM_TT, SSM_BLOCK_CH), lambda m, t: (0, t, m)),
        out_shape=jax.ShapeDtypeStruct((bsz, s, SSM_WIDTH), _BF16),
        scratch_shapes=[
            pltpu.VMEM((2 * SSM_SLABS, bsz * SSM_PITCH, LANES), _F32),
            pltpu.VMEM((2 * SSM_SLABS, bsz, LANES), _F32),
        ],
        compiler_params=_cparams(2),
        name="s5_scan",
    )(proj_main, bw, cw, a_re, a_im, d_skip)


def _ssm_weights(lam_re, lam_im, log_dt, b_re, b_im, c_re, c_im, d_skip):
    lam = lax.complex(lam_re.astype(_F32), lam_im.astype(_F32))
    dt = jnp.exp(log_dt.astype(_F32))[:, None]
    lam_bar = jnp.exp(lam * dt)
    b_bar = ((lam_bar - 1.0) / lam)[..., None] * lax.complex(b_re.astype(_F32), b_im.astype(_F32))
    gpb = SSM_GROUPS // SSM_BLOCKS
    eye = jnp.eye(gpb, dtype=_F32)

    def in_weights(part):
        w = part.reshape(SSM_BLOCKS, gpb, SSM_STATE, SSM_GROUP_SIZE)
        w = jnp.einsum("mgph,gk->mghkp", w, eye)
        return w.reshape(SSM_BLOCKS, SSM_BLOCK_CH, SSM_BLOCK_STATES)

    def out_weights(part):
        w = part.reshape(SSM_BLOCKS, gpb, SSM_GROUP_SIZE, SSM_STATE)
        w = jnp.einsum("mgkp,gj->mgpjk", w, eye)
        return w.reshape(SSM_BLOCKS, SSM_BLOCK_STATES, SSM_BLOCK_CH)

    bw = jnp.concatenate([in_weights(jnp.real(b_bar)), in_weights(jnp.imag(b_bar))], axis=2).astype(_BF16)
    cw = jnp.concatenate([out_weights(c_re.astype(_F32)), -out_weights(c_im.astype(_F32))], axis=1).astype(_BF16)
    a_re = jnp.real(lam_bar).reshape(SSM_BLOCKS, 1, SSM_BLOCK_STATES)
    a_im = jnp.imag(lam_bar).reshape(SSM_BLOCKS, 1, SSM_BLOCK_STATES)
    return bw, cw, a_re, a_im, d_skip.astype(_F32).reshape(SSM_BLOCKS, 1, SSM_BLOCK_CH)


def _to_token_order(ref, t, pt_ref):
    _, dil, _, width = ref.shape
    n = PERM_TILE // dil
    x = ref[0, :, t * n:(t + 1) * n, :].reshape(PERM_TILE, width)
    if x.dtype == _BF16:
        return _dot(pt_ref[...], x)
    hi = x.astype(_BF16)
    r1 = x - hi.astype(_F32)
    mid = r1.astype(_BF16)
    lo = (r1 - mid.astype(_F32)).astype(_BF16)
    res = _dot(pt_ref[...], jnp.concatenate([hi, mid, lo], axis=1))
    return (res[:, :width] + res[:, width:2 * width]) + res[:, 2 * width:]


def _merge_kernel(h_ref, ada_ref, ymla_ref, o0_ref, o1_ref, o2_ref, l0_ref, l1_ref, l2_ref, pt1_ref, pt2_ref,
                  ys_ref, gl_ref, wglu_ref, bglu_ref, wbr_ref, wout_ref, g_ref, b_ref, out_ref):
    gate = ada_ref[0, 5:6, :]
    d = h_ref.shape[-1]
    for t in range(h_ref.shape[1] // PERM_TILE):
        rows = slice(t * PERM_TILE, (t + 1) * PERM_TILE)
        h = h_ref[0, rows, :]
        o0 = o0_ref[0, 0, rows, :].astype(_F32)
        o1 = _to_token_order(o1_ref, t, pt1_ref)
        o2 = _to_token_order(o2_ref, t, pt2_ref)
        l0 = l0_ref[0, 0, rows, :]
        l1 = _to_token_order(l1_ref, t, pt1_ref)
        l2 = _to_token_order(l2_ref, t, pt2_ref)
        m = jnp.maximum(jnp.maximum(l0, l1), l2)
        e0, e1, e2 = jnp.exp(l0 - m), jnp.exp(l1 - m), jnp.exp(l2 - m)
        inv = 1.0 / (e0 + e1 + e2)
        w0, w1, w2 = e0 * inv, e1 * inv, e2 * inv
        heads = []
        for i in range(DIL_HPG):
            lanes = slice(i * DIL_HEAD_DIM, (i + 1) * DIL_HEAD_DIM)
            heads.append(w0[:, i:i + 1] * o0[:, lanes] + w1[:, i:i + 1] * o1[:, lanes]
                         + w2[:, i:i + 1] * o2[:, lanes])
        y_dil = jnp.concatenate(heads, axis=1).astype(_BF16)
        ys = ys_ref[0, rows, :]
        glu = jax.nn.sigmoid(_dot(ys, wglu_ref[...]) + bglu_ref[...])
        y_ssm = (ys.astype(_F32) * glu).astype(_BF16)
        merged = (jax.nn.sigmoid(gl_ref[0, rows, 0:d].astype(_F32)) * _dot(ymla_ref[0, rows, :], wbr_ref[0])
                  + jax.nn.sigmoid(gl_ref[0, rows, d:2 * d].astype(_F32)) * _dot(y_dil, wbr_ref[1])
                  + jax.nn.sigmoid(gl_ref[0, rows, 2 * d:3 * d].astype(_F32)) * _dot(y_ssm, wbr_ref[2]))
        out = _dot(merged.astype(_BF16), wout_ref[...])
        y = DEEPNORM_ALPHA * h + gate * out
        out_ref[0, rows, :] = _layer_norm(y, g_ref[...], b_ref[...])


def _merge_call(h, ada, y_mla, dil_o, dil_lse, y_s, proj_main, wglu, bglu, wbr, wout, ln_g, ln_b):
    bsz, s, d = h.shape
    tm = 2 * PERM_TILE
    tok = lambda b, i: (b, i, 0)
    grp = lambda b, i: (b, 0, i, 0)
    const2 = lambda b, i: (0, 0)
    bw = BRANCH_WIDTH
    d1, d2 = DIL_PAIRS[1][1], DIL_PAIRS[2][1]
    pt1 = jnp.asarray(_perm_matrix(d1).T, _BF16)
    pt2 = jnp.asarray(_perm_matrix(d2).T, _BF16)
    return pl.pallas_call(
        _merge_kernel,
        grid=(bsz, s // tm),
        in_specs=[
            pl.BlockSpec((1, tm, d), tok),
            pl.BlockSpec((1, 9, d), lambda b, i: (b, 0, 0)),
            pl.BlockSpec((1, tm, bw), tok),
            pl.BlockSpec((1, 1, tm, bw), grp),
            pl.BlockSpec((1, d1, tm // d1, bw), grp),
            pl.BlockSpec((1, d2, tm // d2, bw), grp),
            pl.BlockSpec((1, 1, tm, LANES), grp),
            pl.BlockSpec((1, d1, tm // d1, LANES), grp),
            pl.BlockSpec((1, d2, tm // d2, LANES), grp),
            pl.BlockSpec((PERM_TILE, PERM_TILE), const2),
            pl.BlockSpec((PERM_TILE, PERM_TILE), const2),
            pl.BlockSpec((1, tm, bw), tok),
            pl.BlockSpec((1, tm, N_BRANCH * d), lambda b, i: (b, i, COL_GATES // (N_BRANCH * d))),
            pl.BlockSpec((bw, bw), const2),
            pl.BlockSpec((1, bw), const2),
            pl.BlockSpec((N_BRANCH, bw, d), lambda b, i: (0, 0, 0)),
            pl.BlockSpec((d, d), const2),
            pl.BlockSpec((1, d), const2),
            pl.BlockSpec((1, d), const2),
        ],
        out_specs=pl.BlockSpec((1, tm, d), tok),
        out_shape=jax.ShapeDtypeStruct((bsz, s, d), _F32),
        compiler_params=_cparams(2),
        name="merge_sublayer",
    )(h, ada, y_mla, dil_o[0], dil_o[1], dil_o[2], dil_lse[0], dil_lse[1], dil_lse[2], pt1, pt2, y_s,
      proj_main, wglu, bglu, wbr, wout, ln_g, ln_b)


def _in_proj_weights(w_in, b_in):
    d = w_in.shape[0]
    o_kr = MLA_Q_RANK + MLA_KV_RANK
    o_dil = o_kr + MLA_ROPE_DIM
    o_u = o_dil + 3 * DIL_HEADS * DIL_HEAD_DIM
    o_g = o_u + SSM_WIDTH

    def dil_cols(arr, part, group):
        start = o_dil + part * DIL_HEADS * DIL_HEAD_DIM + group * DIL_GROUP_WIDTH
        cols = arr[..., start:start + DIL_GROUP_WIDTH]
        return cols * (DIL_SCALE * LOG2E) if part == 0 else cols

    def main(arr):
        pad = jnp.zeros(arr.shape[:-1] + (COL_Q1 - o_dil,), arr.dtype)
        return jnp.concatenate([arr[..., :o_dil], pad, dil_cols(arr, 0, 0), dil_cols(arr, 1, 0),
                                dil_cols(arr, 2, 0), arr[..., o_u:o_g], arr[..., o_g:]], axis=-1)

    def group(arr, g):
        return jnp.concatenate([dil_cols(arr, 0, g), dil_cols(arr, 1, g), dil_cols(arr, 2, g)], axis=-1)

    b2 = b_in.astype(_F32).reshape(1, -1)
    return ((main(w_in).astype(_BF16), main(b2)),
            (group(w_in, 1).astype(_BF16), group(b2, 1)),
            (group(w_in, 2).astype(_BF16), group(b2, 2)))


def _mla_weights(w_qb, w_kvb):
    qd = MLA_NOPE_DIM + MLA_ROPE_DIM
    half = MLA_ROPE_DIM // 2
    wq = w_qb.reshape(MLA_Q_RANK, MLA_HEADS, qd)
    zeros = jnp.zeros((MLA_Q_RANK, MLA_HEADS, MLA_HEAD_PAD - qd), w_qb.dtype)
    zn = jnp.zeros((MLA_Q_RANK, MLA_HEADS, MLA_NOPE_DIM), w_qb.dtype)
    wq_pad = jnp.concatenate([wq, zeros], axis=-1).reshape(MLA_Q_RANK, MLA_QK_WIDTH)
    t1 = wq[..., MLA_NOPE_DIM:MLA_NOPE_DIM + half]
    t2 = wq[..., MLA_NOPE_DIM + half:]
    wq_rot = jnp.concatenate([zn, -t2, t1, zeros], axis=-1).reshape(MLA_Q_RANK, MLA_QK_WIDTH)
    wkv = w_kvb.reshape(MLA_KV_RANK, MLA_HEADS, MLA_NOPE_DIM + MLA_V_DIM)
    zk = jnp.zeros((MLA_KV_RANK, MLA_HEADS, MLA_HEAD_PAD - MLA_NOPE_DIM), w_kvb.dtype)
    wk = jnp.concatenate([wkv[..., :MLA_NOPE_DIM], zk], axis=-1).reshape(MLA_KV_RANK, MLA_QK_WIDTH)
    wv = wkv[..., MLA_NOPE_DIM:].reshape(MLA_KV_RANK, MLA_HEADS // 2, 2, MLA_V_DIM)
    zv = jnp.zeros_like(wv[:, :, 0])
    wv_pad = jnp.concatenate([wv[:, :, 0], zv, zv, wv[:, :, 1]], axis=-1).reshape(MLA_KV_RANK, MLA_QK_WIDTH)
    return wq_pad.astype(_BF16), wq_rot.astype(_BF16), wk.astype(_BF16), wv_pad.astype(_BF16)


def _mla_ones_row():
    ones = np.zeros((1, MLA_QK_WIDTH), np.float32)
    for hd in range(MLA_HEADS):
        ones[0, hd * MLA_HEAD_PAD + (MLA_V_DIM if hd % 2 == 0 else 0)] = 1.0
    return jnp.asarray(ones)


def _rope_key_placement():
    half = MLA_ROPE_DIM // 2
    pk = np.zeros((LANES, MLA_QK_WIDTH), np.float32)
    pkr = np.zeros((LANES, MLA_QK_WIDTH), np.float32)
    for hd in range(MLA_HEADS):
        base = hd * MLA_HEAD_PAD + MLA_NOPE_DIM
        for i in range(MLA_ROPE_DIM):
            pk[i, base + i] = 1.0
        for i in range(half):
            pkr[half + i, base + i] = -1.0
            pkr[i, base + half + i] = 1.0
    return jnp.asarray(pk, _BF16), jnp.asarray(pkr, _BF16)


def _ffn_weights(w1, w3, w2):
    return w1.astype(_BF16), w3.astype(_BF16), w2.astype(_BF16)


def kernel(x, c, positions, w_ada, b_ada, ln_g, ln_b, ffn_w1, ffn_w3, ffn_w2, w_in, b_in, mla_q_norm, mla_kv_norm, mla_w_qb, mla_w_kvb, ssm_lambda_re, ssm_lambda_im, ssm_log_dt, ssm_b_re, ssm_b_im, ssm_c_re, ssm_c_im, ssm_d, ssm_w_glu, ssm_b_glu, w_br, w_out):
    bsz, s, d = x.shape
    assert d == D_MODEL and s % (DIL_PAIRS[2][1] * DIL_BAND) == 0, x.shape
    ada_all = _ada_call(c, w_ada, b_ada).reshape(DEPTH, bsz, 9, d)
    cos, sin = _rope_call(positions)
    pk, pkr = _rope_key_placement()
    ones_row = _mla_ones_row()
    h = x
    for l in range(DEPTH):
        ada = ada_all[l]
        h = _ffn_call(h, ada, 0, *_ffn_weights(ffn_w1[l, 0], ffn_w3[l, 0], ffn_w2[l, 0]),
                      ln_g[l, 0].reshape(1, d), ln_b[l, 0].reshape(1, d))
        (w_main, b_main), (w_g1, b_g1), (w_g2, b_g2) = _in_proj_weights(w_in[l], b_in[l])
        proj_main, proj_g1, proj_g2 = _proj_call(
            h, ada, w_main, b_main, [(DIL_PAIRS[1][1], w_g1, b_g1), (DIL_PAIRS[2][1], w_g2, b_g2)])
        proj_main4 = proj_main.reshape(bsz, 1, s, N_MAIN)
        wq, wqr, wk, wv = _mla_weights(mla_w_qb[l], mla_w_kvb[l])
        q, k, v = _mla_prep_call(proj_main, cos, sin, mla_q_norm[l].reshape(1, -1),
                                 mla_kv_norm[l].reshape(1, -1), wq, wqr, wk, wv, pk, pkr, ones_row)
        y_mla = _mla_attn_call(q, k, v)
        o0, lse0 = _dil_attn_call(proj_main4, COL_Q1 // DIL_GROUP_WIDTH, 0, DIL_PAIRS[0][1], 2048, 1)
        o1, lse1 = _dil_attn_call(proj_g1, 0, 1, DIL_PAIRS[1][1], 1024, 2)
        o2, lse2 = _dil_attn_call(proj_g2, 0, 2, DIL_PAIRS[2][1], 256, 16)
        y_s = _ssm_call(proj_main4, *_ssm_weights(ssm_lambda_re[l], ssm_lambda_im[l], ssm_log_dt[l],
                                                 ssm_b_re[l], ssm_b_im[l], ssm_c_re[l], ssm_c_im[l], ssm_d[l]))
        h = _merge_call(h, ada, y_mla, (o0, o1, o2), (lse0, lse1, lse2), y_s, proj_main,
                        ssm_w_glu[l].astype(_BF16), ssm_b_glu[l].reshape(1, -1), w_br[l].astype(_BF16),
                        w_out[l].astype(_BF16), ln_g[l, 1].reshape(1, d), ln_b[l, 1].reshape(1, d))
        h = _ffn_call(h, ada, 2, *_ffn_weights(ffn_w1[l, 1], ffn_w3[l, 1], ffn_w2[l, 1]),
                      ln_g[l, 2].reshape(1, d), ln_b[l, 2].reshape(1, d))
    return h
```

```python
import functools
import math

import numpy as np
import jax
import jax.numpy as jnp
from jax import lax
from jax.experimental import pallas as pl
from jax.experimental.pallas import tpu as pltpu

D_MODEL = 1024
DEPTH = 4
D_FF = 2816
MLA_HEADS = 8
MLA_Q_RANK = 384
MLA_KV_RANK = 256
MLA_NOPE_DIM = 64
MLA_ROPE_DIM = 32
MLA_V_DIM = 64
ROPE_THETA = 10000.0
DIL_PAIRS = ((128, 1), (512, 4), (2048, 16))
DIL_HPG = 4
DIL_HEADS = DIL_HPG * len(DIL_PAIRS)
DIL_HEAD_DIM = 128
DIL_BAND = 128
SSM_WIDTH = 512
SSM_GROUP_SIZE = 16
SSM_GROUPS = SSM_WIDTH // SSM_GROUP_SIZE
SSM_STATE = 64
N_BRANCH = 3
BRANCH_WIDTH = 512
DEEPNORM_ALPHA = (2 * DEPTH) ** 0.25
MACARON_WEIGHT = 0.5
LN_EPS = 1e-5
RMS_EPS = 1e-6

LANES = 128
V7X_VMEM_LIMIT_BYTES = 56 * 1024 * 1024

FF_CHUNK = 256
FFN_ROWS = 512
N_FF_CHUNKS = D_FF // FF_CHUNK
MLA_HEAD_PAD = 128
MLA_QK_WIDTH = MLA_HEADS * MLA_HEAD_PAD
MLA_V_WIDTH = MLA_HEADS * MLA_V_DIM
MLA_SCALE = (MLA_NOPE_DIM + MLA_ROPE_DIM) ** -0.5
DIL_SCALE = DIL_HEAD_DIM ** -0.5
DIL_GROUP_WIDTH = DIL_HPG * DIL_HEAD_DIM
MLA_A_WIDTH = 768
COL_Q1 = 1024
COL_U = COL_Q1 + 3 * DIL_GROUP_WIDTH
COL_GATES = COL_U + SSM_WIDTH
N_MAIN = COL_GATES + N_BRANCH * D_MODEL
PROJ_CHUNK = 512
PERM_TILE = 256
LOG2E = math.log2(math.e)
MLA_ROW_CHUNK = 64
MLA_HEADS_PER_STEP = 4
SSM_BLOCKS = 4
SSM_BLOCK_CH = SSM_WIDTH // SSM_BLOCKS
SSM_BLOCK_STATES = SSM_GROUPS * SSM_STATE // SSM_BLOCKS
SSM_SLABS = SSM_BLOCK_STATES // LANES
SSM_TT = 256
SSM_PITCH = 260

_F32 = jnp.float32
_BF16 = jnp.bfloat16
_NEG = -1e30


def _cparams(n_axes):
    return pltpu.CompilerParams(
        dimension_semantics=("arbitrary",) * n_axes,
        vmem_limit_bytes=V7X_VMEM_LIMIT_BYTES,
    )


def _resident(block_shape, index_map):
    return pl.BlockSpec(block_shape, index_map, pipeline_mode=pl.Buffered(1))


def _layer_norm(y, g, b):
    mu = jnp.mean(y, axis=-1, keepdims=True)
    yc = y - mu
    var = jnp.mean(yc * yc, axis=-1, keepdims=True)
    return yc * lax.rsqrt(var + LN_EPS) * g + b


def _dot(a, b):
    return jnp.dot(a, b, preferred_element_type=_F32)


def _dot_nt(a, b):
    return lax.dot_general(a, b, (((1,), (1,)), ((), ())), preferred_element_type=_F32)


def _ada_kernel(c_ref, w_ref, b_ref, o_ref):
    c = c_ref[...]
    cond = (c * jax.nn.sigmoid(c)).astype(_BF16)
    o_ref[0] = _dot(cond, w_ref[0].astype(_BF16)) + b_ref[0]


def _ada_call(c, w_ada, b_ada):
    depth, d, n = w_ada.shape
    bsz = c.shape[0]
    tn = 1536
    return pl.pallas_call(
        _ada_kernel,
        grid=(depth, n // tn),
        in_specs=[
            pl.BlockSpec((bsz, d), lambda l, j: (0, 0)),
            pl.BlockSpec((1, d, tn), lambda l, j: (l, 0, j)),
            pl.BlockSpec((1, 1, tn), lambda l, j: (l, 0, j)),
        ],
        out_specs=pl.BlockSpec((1, bsz, tn), lambda l, j: (l, 0, j)),
        out_shape=jax.ShapeDtypeStruct((depth, bsz, n), _F32),
        compiler_params=_cparams(2),
        name="ada",
    )(c, w_ada, b_ada.reshape(depth, 1, n))


def _rope_freq_lanes():
    half = MLA_ROPE_DIM // 2
    inv_freq = np.power(np.float32(ROPE_THETA), -np.arange(half, dtype=np.float32) / np.float32(half))
    f = np.zeros((1, MLA_HEAD_PAD), np.float32)
    f[0, MLA_NOPE_DIM:MLA_NOPE_DIM + half] = inv_freq
    f[0, MLA_NOPE_DIM + half:MLA_NOPE_DIM + 2 * half] = inv_freq
    return f


def _rope_kernel(pos_ref, f_ref, cos_ref, sin_ref):
    ang = pos_ref[0].astype(_F32) * f_ref[...]
    cos_ref[0] = jnp.cos(ang)
    sin_ref[0] = jnp.sin(ang)


def _rope_call(positions):
    bsz, s = positions.shape
    tm = min(s, 512)
    out = jax.ShapeDtypeStruct((bsz, s, MLA_HEAD_PAD), _F32)
    return pl.pallas_call(
        _rope_kernel,
        grid=(bsz, s // tm),
        in_specs=[
            pl.BlockSpec((1, tm, 1), lambda b, i: (b, i, 0)),
            pl.BlockSpec((1, MLA_HEAD_PAD), lambda b, i: (0, 0)),
        ],
        out_specs=[pl.BlockSpec((1, tm, MLA_HEAD_PAD), lambda b, i: (b, i, 0))] * 2,
        out_shape=[out, out],
        compiler_params=_cparams(2),
        name="rope_tables",
    )(positions.reshape(bsz, s, 1), jnp.asarray(_rope_freq_lanes()))


def _ffn_kernel(sub, h_ref, ada_ref, w1_ref, w3_ref, w2_ref, g_ref, b_ref, o_ref):
    shift = ada_ref[0, 3 * sub + 0:3 * sub + 1, :]
    scale = ada_ref[0, 3 * sub + 1:3 * sub + 2, :]
    gate = ada_ref[0, 3 * sub + 2:3 * sub + 3, :]
    for r0 in range(0, h_ref.shape[1], FFN_ROWS):
        rows = slice(r0, r0 + FFN_ROWS)
        h = h_ref[0, rows, :]
        xm = (h * (1.0 + scale) + shift).astype(_BF16)
        acc = jnp.zeros(h.shape, _F32)
        for j in range(N_FF_CHUNKS):
            cols = slice(j * FF_CHUNK, (j + 1) * FF_CHUNK)
            a = _dot(xm, w1_ref[:, cols])
            b = _dot(xm, w3_ref[:, cols])
            g = (a * jax.nn.sigmoid(a) * b).astype(_BF16)
            acc = acc + _dot(g, w2_ref[cols, :])
        y = DEEPNORM_ALPHA * h + MACARON_WEIGHT * gate * acc
        o_ref[0, rows, :] = _layer_norm(y, g_ref[...], b_ref[...])


def _ffn_call(h, ada, sub, w1, w3, w2, ln_g, ln_b):
    bsz, s, d = h.shape
    tm = min(s, 2 * FFN_ROWS)
    return pl.pallas_call(
        functools.partial(_ffn_kernel, sub),
        grid=(bsz, s // tm),
        in_specs=[
            pl.BlockSpec((1, tm, d), lambda b, i: (b, i, 0)),
            pl.BlockSpec((1, 9, d), lambda b, i: (b, 0, 0)),
            _resident((d, D_FF), lambda b, i: (0, 0)),
            _resident((d, D_FF), lambda b, i: (0, 0)),
            _resident((D_FF, d), lambda b, i: (0, 0)),
            pl.BlockSpec((1, d), lambda b, i: (0, 0)),
            pl.BlockSpec((1, d), lambda b, i: (0, 0)),
        ],
        out_specs=pl.BlockSpec((1, tm, d), lambda b, i: (b, i, 0)),
        out_shape=jax.ShapeDtypeStruct((bsz, s, d), _F32),
        compiler_params=_cparams(2),
        name="ffn_sublayer",
    )(h, ada, w1, w3, w2, ln_g, ln_b)


def _perm_matrix(dil):
    n = PERM_TILE // dil
    p = np.zeros((PERM_TILE, PERM_TILE), np.float32)
    for r in range(dil):
        for i in range(n):
            p[r * n + i, i * dil + r] = 1.0
    return p


def _proj_kernel(tm, dils, x_ref, ada_ref, wm_ref, bm_ref, *rest):
    ngrp = len(dils)
    grp_in = rest[:2 * ngrp]
    om_ref = rest[2 * ngrp]
    grp_out = rest[2 * ngrp + 1:2 * ngrp + 1 + ngrp]
    xs_ref, xp_ref = rest[2 * ngrp + 1 + ngrp:]
    x = x_ref[0]
    shift = ada_ref[0, 3:4, :]
    scale = ada_ref[0, 4:5, :]
    xm_f32 = x * (1.0 + scale) + shift
    xm = xm_f32.astype(_BF16)
    n_main = wm_ref.shape[1]
    for n0 in range(0, n_main, PROJ_CHUNK):
        n1 = min(n0 + PROJ_CHUNK, n_main)
        om_ref[0, :, n0:n1] = (_dot(xm, wm_ref[:, n0:n1]) + bm_ref[:, n0:n1]).astype(_BF16)
    nslab = x.shape[1] // LANES
    for k in range(nslab):
        xs_ref[k] = xm_f32[:, k * LANES:(k + 1) * LANES]
    for g, dil in enumerate(dils):
        w_ref, b_ref = grp_in[2 * g:2 * g + 2]
        o_ref = grp_out[g]
        n = tm // dil
        n_out = w_ref.shape[1]
        for r in range(dil):
            for k in range(nslab):
                xp_ref[g, r * n:(r + 1) * n, k * LANES:(k + 1) * LANES] = (
                    xs_ref[k, pl.ds(r, n, stride=dil), :].astype(_BF16))
        xp = xp_ref[g]
        for n0 in range(0, n_out, PROJ_CHUNK):
            n1 = min(n0 + PROJ_CHUNK, n_out)
            res = (_dot(xp, w_ref[:, n0:n1]) + b_ref[:, n0:n1]).astype(_BF16)
            for r in range(dil):
                o_ref[0, r, :, n0:n1] = res[r * n:(r + 1) * n]


def _proj_call(h, ada, w_main, b_main, groups):
    bsz, s, d = h.shape
    tm = min(s, 512)
    dils = tuple(g[0] for g in groups)
    const = lambda b, i: (0, 0)
    in_specs = [
        pl.BlockSpec((1, tm, d), lambda b, i: (b, i, 0)),
        pl.BlockSpec((1, 9, d), lambda b, i: (b, 0, 0)),
        _resident(w_main.shape, const),
        pl.BlockSpec(b_main.shape, const),
    ]
    args = [h, ada, w_main, b_main]
    out_specs = [pl.BlockSpec((1, tm, w_main.shape[1]), lambda b, i: (b, i, 0))]
    out_shape = [jax.ShapeDtypeStruct((bsz, s, w_main.shape[1]), _BF16)]
    for dil, w, bias in groups:
        in_specs += [_resident(w.shape, const), pl.BlockSpec(bias.shape, const)]
        args += [w, bias]
        out_specs.append(pl.BlockSpec((1, dil, tm // dil, w.shape[1]), lambda b, i: (b, 0, i, 0)))
        out_shape.append(jax.ShapeDtypeStruct((bsz, dil, s // dil, w.shape[1]), _BF16))
    return pl.pallas_call(
        functools.partial(_proj_kernel, tm, dils),
        grid=(bsz, s // tm),
        in_specs=in_specs,
        out_specs=out_specs,
        out_shape=out_shape,
        scratch_shapes=[pltpu.VMEM((d // LANES, tm, LANES), _F32), pltpu.VMEM((len(groups), tm, d), _BF16)],
        compiler_params=_cparams(2),
        name="in_proj",
    )(*args)


def _mla_prep_kernel(a_ref, cos_ref, sin_ref, qg_ref, kvg_ref, wq_ref, wqr_ref, wk_ref, wv_ref,
                     pk_ref, pkr_ref, ones_ref, q_ref, k_ref, v_ref):
    a = a_ref[0]
    qa = a[:, :MLA_Q_RANK].astype(_F32)
    kva = a[:, MLA_Q_RANK:MLA_Q_RANK + MLA_KV_RANK].astype(_F32)
    kr = a[:, MLA_Q_RANK + MLA_KV_RANK:]
    qn = (qa * lax.rsqrt(jnp.mean(qa * qa, axis=-1, keepdims=True) + RMS_EPS) * qg_ref[...]).astype(_BF16)
    kvn = (kva * lax.rsqrt(jnp.mean(kva * kva, axis=-1, keepdims=True) + RMS_EPS) * kvg_ref[...]).astype(_BF16)
    cos = jnp.concatenate([cos_ref[0]] * 2, axis=1) * (MLA_SCALE * LOG2E)
    sin = jnp.concatenate([sin_ref[0]] * 2, axis=1) * (MLA_SCALE * LOG2E)
    kcos = jnp.concatenate([cos_ref[0]] * 2, axis=1)
    ksin = jnp.concatenate([sin_ref[0]] * 2, axis=1)
    for c in range(MLA_HEADS // 2):
        sl = slice(2 * c * MLA_HEAD_PAD, 2 * (c + 1) * MLA_HEAD_PAD)
        q = _dot(qn, wq_ref[:, sl])
        qrot = _dot(qn, wqr_ref[:, sl])
        q_ref[0, :, sl] = (q * cos + qrot * sin).astype(_BF16)
        k = _dot(kvn, wk_ref[:, sl]) + _dot(kr, pk_ref[:, sl])
        krot = _dot(kr, pkr_ref[:, sl])
        k_ref[0, :, sl] = (k * kcos + krot * ksin).astype(_BF16)
        v_ref[0, :, sl] = (_dot(kvn, wv_ref[:, sl]) + ones_ref[:, sl]).astype(_BF16)


def _mla_prep_call(proj_main, cos, sin, qg, kvg, wq, wqr, wk, wv, pk, pkr, ones_row):
    bsz, s, _ = proj_main.shape
    tm = min(s, 512)
    const = lambda b, i: (0, 0)
    return pl.pallas_call(
        _mla_prep_kernel,
        grid=(bsz, s // tm),
        in_specs=[
            pl.BlockSpec((1, tm, MLA_A_WIDTH), lambda b, i: (b, i, 0)),
            pl.BlockSpec((1, tm, MLA_HEAD_PAD), lambda b, i: (b, i, 0)),
            pl.BlockSpec((1, tm, MLA_HEAD_PAD), lambda b, i: (b, i, 0)),
            pl.BlockSpec((1, MLA_Q_RANK), const),
            pl.BlockSpec((1, MLA_KV_RANK), const),
            pl.BlockSpec((MLA_Q_RANK, MLA_QK_WIDTH), const),
            pl.BlockSpec((MLA_Q_RANK, MLA_QK_WIDTH), const),
            pl.BlockSpec((MLA_KV_RANK, MLA_QK_WIDTH), const),
            pl.BlockSpec((MLA_KV_RANK, MLA_QK_WIDTH), const),
            pl.BlockSpec((LANES, MLA_QK_WIDTH), const),
            pl.BlockSpec((LANES, MLA_QK_WIDTH), const),
            pl.BlockSpec((1, MLA_QK_WIDTH), const),
        ],
        out_specs=[pl.BlockSpec((1, tm, MLA_QK_WIDTH), lambda b, i: (b, i, 0))] * 3,
        out_shape=[jax.ShapeDtypeStruct((bsz, s, MLA_QK_WIDTH), _BF16)] * 3,
        compiler_params=_cparams(2),
        name="mla_prep",
    )(proj_main, cos, sin, qg, kvg, wq, wqr, wk, wv, pk, pkr, ones_row)


def _mla_attn_kernel(tq, q_ref, k_ref, v_ref, o_ref, s_ref, p_ref, m_ref, acc_ref):
    nh = MLA_HEADS_PER_STEP
    half = tq // 2
    qi = pl.program_id(2)
    row = lax.broadcasted_iota(jnp.int32, (tq, tq), 0)
    col = lax.broadcasted_iota(jnp.int32, (tq, tq), 1)
    causal = col <= row
    lanes = [slice(e * MLA_HEAD_PAD, (e + 1) * MLA_HEAD_PAD) for e in range(nh)]
    qs = [q_ref[0, :, lanes[e]] for e in range(nh)]

    def scores(j, slot):
        start = pl.multiple_of(j * tq, tq)
        for e in range(nh):
            s_ref[slot, e] = _dot_nt(qs[e], k_ref[0, pl.ds(start, tq), lanes[e]])

    def softmax_pv(j, slot, masked):
        start = pl.multiple_of(j * tq, tq)
        for e in range(nh):
            alphas = []
            for c in range(tq // MLA_ROW_CHUNK):
                rows = slice(c * MLA_ROW_CHUNK, (c + 1) * MLA_ROW_CHUNK)
                cols = slice(0, half if masked and (c + 1) * MLA_ROW_CHUNK <= half else tq)
                s = s_ref[slot, e, rows, cols]
                if masked:
                    s = jnp.where(causal[rows, cols], s, _NEG)
                m_old = m_ref[e, rows, :]
                m_c = jnp.maximum(m_old, jnp.max(s, axis=-1, keepdims=True))
                p_ref[e, rows, cols] = jnp.exp2(s - m_c).astype(_BF16)
                m_ref[e, rows, :] = m_c
                alphas.append(jnp.exp2(m_old - m_c))
            if masked:
                pv = jnp.concatenate([
                    _dot(p_ref[e, :half, :half], v_ref[0, pl.ds(start, half), lanes[e]]),
                    _dot(p_ref[e, half:, :], v_ref[0, pl.ds(start, tq), lanes[e]])], axis=0)
            else:
                pv = _dot(p_ref[e], v_ref[0, pl.ds(start, tq), lanes[e]])
            acc_ref[e] = jnp.concatenate(alphas, axis=0) * acc_ref[e] + pv

    def pair(jj, carry):
        t = 2 * jj
        scores(t + 1, 1)
        softmax_pv(t, 0, False)
        scores(t + 2, 0)
        softmax_pv(t + 1, 1, False)
        return carry

    m_ref[...] = jnp.full(m_ref.shape, _NEG, _F32)
    acc_ref[...] = jnp.zeros(acc_ref.shape, _F32)
    scores(0, 0)
    lax.fori_loop(0, qi // 2, pair, 0)

    @pl.when(qi % 2 == 1)
    def _():
        scores(qi, 1)
        softmax_pv(qi - 1, 0, False)
        softmax_pv(qi, 1, True)

    @pl.when(qi % 2 == 0)
    def _():
        softmax_pv(qi, 0, True)

    lane = lax.broadcasted_iota(jnp.int32, (tq, MLA_HEAD_PAD), 1)
    for e2 in range(nh // 2):
        acc0, acc1 = acc_ref[2 * e2], acc_ref[2 * e2 + 1]
        out0 = acc0 * (1.0 / acc0[:, MLA_V_DIM:MLA_V_DIM + 1])
        out1 = acc1 * (1.0 / acc1[:, 0:1])
        o_ref[0, :, lanes[e2]] = jnp.where(lane < MLA_V_DIM, out0, out1).astype(_BF16)


def _mla_attn_call(q, k, v):
    bsz, s, _ = q.shape
    tq = min(s, 512)
    nh = MLA_HEADS_PER_STEP
    qk_w = nh * MLA_HEAD_PAD
    return pl.pallas_call(
        functools.partial(_mla_attn_kernel, tq),
        grid=(bsz, MLA_HEADS // nh, s // tq),
        in_specs=[
            pl.BlockSpec((1, tq, qk_w), lambda b, h, i: (b, i, h)),
            pl.BlockSpec((1, s, qk_w), lambda b, h, i: (b, 0, h)),
            pl.BlockSpec((1, s, qk_w), lambda b, h, i: (b, 0, h)),
        ],
        out_specs=pl.BlockSpec((1, tq, nh * MLA_V_DIM), lambda b, h, i: (b, i, h)),
        out_shape=jax.ShapeDtypeStruct((bsz, s, MLA_V_WIDTH), _BF16),
        scratch_shapes=[pltpu.VMEM((2, nh, tq, tq), _F32), pltpu.VMEM((nh, tq, tq), _BF16),
                        pltpu.VMEM((nh, tq, 1), _F32), pltpu.VMEM((nh, tq, MLA_HEAD_PAD), _F32)],
        compiler_params=_cparams(3),
        name="mla_attention",
    )(q, k, v)


def _alibi_slope(head):
    return float(np.exp2(np.float32(-8.0) * (np.float32(head) + np.float32(1.0)) / np.float32(DIL_HEADS)))


def _dil_attn_kernel(tq, kw, single_tile, nres, group, dil, q_ref, k_ref, v_ref, o_ref, lse_ref, bias_ref):
    qi = pl.program_id(2)
    first_step = (pl.program_id(0) == 0) & (pl.program_id(1) == 0) & (qi == 0)
    nb = DIL_BAND

    @pl.when(first_step)
    def _():
        row = lax.broadcasted_iota(jnp.int32, (nb, kw), 0)
        col = lax.broadcasted_iota(jnp.int32, (nb, kw), 1)
        for variant, rel in ((0, nb + row - col), (1, row - col)):
            ok = (rel >= 0) & (rel <= DIL_BAND)
            relf = rel.astype(_F32)
            for i in range(DIL_HPG):
                slope = _alibi_slope(group * DIL_HPG + i) * dil * LOG2E
                bias_ref[i, variant] = jnp.where(ok, -slope * relf, _NEG)

    q0 = pl.multiple_of(qi * tq, tq)
    lane = lax.broadcasted_iota(jnp.int32, (nb, LANES), 1)
    for rr in range(nres):
        for sb in range(tq // nb):
            rows = slice(sb * nb, (sb + 1) * nb)
            if sb == 0:
                start = pl.multiple_of(jnp.maximum(q0 - nb, 0), nb)
            else:
                start = pl.multiple_of(q0 + (sb - 1) * nb, nb)
            m_all = jnp.zeros((nb, LANES), _F32)
            l_all = jnp.ones((nb, LANES), _F32)
            for i in range(DIL_HPG):
                lanes = slice(i * DIL_HEAD_DIM, (i + 1) * DIL_HEAD_DIM)
                if sb > 0:
                    bias = bias_ref[i, 0]
                elif single_tile:
                    bias = bias_ref[i, 1]
                else:
                    bias = jnp.where(qi == 0, bias_ref[i, 1], bias_ref[i, 0])
                s = _dot_nt(q_ref[0, rr, rows, lanes], k_ref[0, rr, pl.ds(start, kw), lanes]) + bias
                m = jnp.max(s, axis=-1, keepdims=True)
                e = jnp.exp2(s - m)
                l = jnp.sum(e, axis=-1, keepdims=True)
                o = _dot(e.astype(_BF16), v_ref[0, rr, pl.ds(start, kw), lanes])
                o_ref[0, rr, rows, lanes] = (o * (1.0 / l)).astype(_BF16)
                m_all = jnp.where(lane == i, m, m_all)
                l_all = jnp.where(lane == i, l, l_all)
            lse_ref[0, rr, rows, :] = (m_all + jnp.log2(l_all)) * (1.0 / LOG2E)


def _dil_attn_call(arr, col_block0, group, dil, tq, nres):
    bsz, _, length, _ = arr.shape
    w = DIL_GROUP_WIDTH
    tq = min(tq, length)
    kw = min(2 * DIL_BAND, length)
    nres = min(nres, dil)
    return pl.pallas_call(
        functools.partial(_dil_attn_kernel, tq, kw, length == tq, nres, group, dil),
        grid=(bsz, dil // nres, length // tq),
        in_specs=[
            pl.BlockSpec((1, nres, tq, w), lambda b, r, i: (b, r, i, col_block0)),
            pl.BlockSpec((1, nres, length, w), lambda b, r, i: (b, r, 0, col_block0 + 1)),
            pl.BlockSpec((1, nres, length, w), lambda b, r, i: (b, r, 0, col_block0 + 2)),
        ],
        out_specs=[
            pl.BlockSpec((1, nres, tq, w), lambda b, r, i: (b, r, i, 0)),
            pl.BlockSpec((1, nres, tq, LANES), lambda b, r, i: (b, r, i, 0)),
        ],
        out_shape=[
            jax.ShapeDtypeStruct((bsz, dil, length, w), _BF16),
            jax.ShapeDtypeStruct((bsz, dil, length, LANES), _F32),
        ],
        scratch_shapes=[pltpu.VMEM((DIL_HPG, 2, DIL_BAND, kw), _F32)],
        compiler_params=_cparams(3),
        name=f"dil_attention_g{group}",
    )(arr, arr, arr)


def _ssm_kernel(nb, u_ref, bw_ref, cw_ref, are_ref, aim_ref, d_ref, o_ref, bx_ref, st_ref):
    ti = pl.program_id(1)

    @pl.when(ti == 0)
    def _():
        st_ref[...] = jnp.zeros(st_ref.shape, _F32)

    for b in range(nb):
        bu = _dot(u_ref[b, 0], bw_ref[0])
        for sidx in range(2 * SSM_SLABS):
            bx_ref[sidx, b * SSM_PITCH:b * SSM_PITCH + SSM_TT, :] = bu[:, sidx * LANES:(sidx + 1) * LANES]

    a_re = [jnp.broadcast_to(are_ref[0, :, k * LANES:(k + 1) * LANES], (nb, LANES)) for k in range(SSM_SLABS)]
    a_im = [jnp.broadcast_to(aim_ref[0, :, k * LANES:(k + 1) * LANES], (nb, LANES)) for k in range(SSM_SLABS)]

    def step(t, carry):
        xr, xi = carry
        nr, ni = [], []
        for k in range(SSM_SLABS):
            rows = pl.ds(t, nb, stride=SSM_PITCH)
            br = bx_ref[k, rows, :]
            bi = bx_ref[SSM_SLABS + k, rows, :]
            r = a_re[k] * xr[k] - a_im[k] * xi[k] + br
            i = a_re[k] * xi[k] + a_im[k] * xr[k] + bi
            bx_ref[k, rows, :] = r
            bx_ref[SSM_SLABS + k, rows, :] = i
            nr.append(r)
            ni.append(i)
        return tuple(nr), tuple(ni)

    x0 = (tuple(st_ref[k] for k in range(SSM_SLABS)),
          tuple(st_ref[SSM_SLABS + k] for k in range(SSM_SLABS)))
    xr, xi = lax.fori_loop(0, SSM_TT, step, x0, unroll=8)
    for k in range(SSM_SLABS):
        st_ref[k] = xr[k]
        st_ref[SSM_SLABS + k] = xi[k]

    dskip = d_ref[0]
    for b in range(nb):
        xs = jnp.concatenate(
            [bx_ref[sidx, b * SSM_PITCH:b * SSM_PITCH + SSM_TT, :] for sidx in range(2 * SSM_SLABS)],
            axis=1).astype(_BF16)
        y = _dot(xs, cw_ref[0]) + dskip * u_ref[b, 0].astype(_F32)
        o_ref[b] = jax.nn.gelu(y).astype(_BF16)


def _ssm_call(proj_main, bw, cw, a_re, a_im, d_skip):
    bsz, _, s, _ = proj_main.shape
    u_block0 = COL_U // SSM_BLOCK_CH
    nstate = 2 * SSM_BLOCK_STATES
    return pl.pallas_call(
        functools.partial(_ssm_kernel, bsz),
        grid=(SSM_BLOCKS, s // SSM_TT),
        in_specs=[
            pl.BlockSpec((bsz, 1, SSM_TT, SSM_BLOCK_CH), lambda m, t: (0, 0, t, u_block0 + m)),
            pl.BlockSpec((1, SSM_BLOCK_CH, nstate), lambda m, t: (m, 0, 0)),
            pl.BlockSpec((1, nstate, SSM_BLOCK_CH), lambda m, t: (m, 0, 0)),
            pl.BlockSpec((1, 1, SSM_BLOCK_STATES), lambda m, t: (m, 0, 0)),
            pl.BlockSpec((1, 1, SSM_BLOCK_STATES), lambda m, t: (m, 0, 0)),
            pl.BlockSpec((1, 1, SSM_BLOCK_CH), lambda m, t: (m, 0, 0)),
        ],
        out_specs=pl.BlockSpec((bsz, SSM_TT, SSM_BLOCK_CH), lambda m, t: (0, t, m)),
        out_shape=jax.ShapeDtypeStruct((bsz, s, SSM_WIDTH), _BF16),
        scratch_shapes=[
            pltpu.VMEM((2 * SSM_SLABS, bsz * SSM_PITCH, LANES), _F32),
            pltpu.VMEM((2 * SSM_SLABS, bsz, LANES), _F32),
        ],
        compiler_params=_cparams(2),
        name="s5_scan",
    )(proj_main, bw, cw, a_re, a_im, d_skip)


def _ssm_weights(lam_re, lam_im, log_dt, b_re, b_im, c_re, c_im, d_skip):
    lam = lax.complex(lam_re.astype(_F32), lam_im.astype(_F32))
    dt = jnp.exp(log_dt.astype(_F32))[:, None]
    lam_bar = jnp.exp(lam * dt)
    b_bar = ((lam_bar - 1.0) / lam)[..., None] * lax.complex(b_re.astype(_F32), b_im.astype(_F32))
    gpb = SSM_GROUPS // SSM_BLOCKS
    eye = jnp.eye(gpb, dtype=_F32)

    def in_weights(part):
        w = part.reshape(SSM_BLOCKS, gpb, SSM_STATE, SSM_GROUP_SIZE)
        w = jnp.einsum("mgph,gk->mghkp", w, eye)
        return w.reshape(SSM_BLOCKS, SSM_BLOCK_CH, SSM_BLOCK_STATES)

    def out_weights(part):
        w = part.reshape(SSM_BLOCKS, gpb, SSM_GROUP_SIZE, SSM_STATE)
        w = jnp.einsum("mgkp,gj->mgpjk", w, eye)
        return w.reshape(SSM_BLOCKS, SSM_BLOCK_STATES, SSM_BLOCK_CH)

    bw = jnp.concatenate([in_weights(jnp.real(b_bar)), in_weights(jnp.imag(b_bar))], axis=2).astype(_BF16)
    cw = jnp.concatenate([out_weights(c_re.astype(_F32)), -out_weights(c_im.astype(_F32))], axis=1).astype(_BF16)
    a_re = jnp.real(lam_bar).reshape(SSM_BLOCKS, 1, SSM_BLOCK_STATES)
    a_im = jnp.imag(lam_bar).reshape(SSM_BLOCKS, 1, SSM_BLOCK_STATES)
    return bw, cw, a_re, a_im, d_skip.astype(_F32).reshape(SSM_BLOCKS, 1, SSM_BLOCK_CH)


def _to_token_order(ref, t, pt_ref):
    _, dil, _, width = ref.shape
    n = PERM_TILE // dil
    x = ref[0, :, t * n:(t + 1) * n, :].reshape(PERM_TILE, width)
    if x.dtype == _BF16:
        return _dot(pt_ref[...], x)
    hi = x.astype(_BF16)
    r1 = x - hi.astype(_F32)
    mid = r1.astype(_BF16)
    lo = (r1 - mid.astype(_F32)).astype(_BF16)
    res = _dot(pt_ref[...], jnp.concatenate([hi, mid, lo], axis=1))
    return (res[:, :width] + res[:, width:2 * width]) + res[:, 2 * width:]


def _merge_kernel(h_ref, ada_ref, ymla_ref, o0_ref, o1_ref, o2_ref, l0_ref, l1_ref, l2_ref, pt1_ref, pt2_ref,
                  ys_ref, gl_ref, wglu_ref, bglu_ref, wbr_ref, wout_ref, g_ref, b_ref, out_ref):
    gate = ada_ref[0, 5:6, :]
    d = h_ref.shape[-1]
    for t in range(h_ref.shape[1] // PERM_TILE):
        rows = slice(t * PERM_TILE, (t + 1) * PERM_TILE)
        h = h_ref[0, rows, :]
        o0 = o0_ref[0, 0, rows, :].astype(_F32)
        o1 = _to_token_order(o1_ref, t, pt1_ref)
        o2 = _to_token_order(o2_ref, t, pt2_ref)
        l0 = l0_ref[0, 0, rows, :]
        l1 = _to_token_order(l1_ref, t, pt1_ref)
        l2 = _to_token_order(l2_ref, t, pt2_ref)
        m = jnp.maximum(jnp.maximum(l0, l1), l2)
        e0, e1, e2 = jnp.exp(l0 - m), jnp.exp(l1 - m), jnp.exp(l2 - m)
        inv = 1.0 / (e0 + e1 + e2)
        w0, w1, w2 = e0 * inv, e1 * inv, e2 * inv
        heads = []
        for i in range(DIL_HPG):
            lanes = slice(i * DIL_HEAD_DIM, (i + 1) * DIL_HEAD_DIM)
            heads.append(w0[:, i:i + 1] * o0[:, lanes] + w1[:, i:i + 1] * o1[:, lanes]
                         + w2[:, i:i + 1] * o2[:, lanes])
        y_dil = jnp.concatenate(heads, axis=1).astype(_BF16)
        ys = ys_ref[0, rows, :]
        glu = jax.nn.sigmoid(_dot(ys, wglu_ref[...]) + bglu_ref[...])
        y_ssm = (ys.astype(_F32) * glu).astype(_BF16)
        merged = (jax.nn.sigmoid(gl_ref[0, rows, 0:d].astype(_F32)) * _dot(ymla_ref[0, rows, :], wbr_ref[0])
                  + jax.nn.sigmoid(gl_ref[0, rows, d:2 * d].astype(_F32)) * _dot(y_dil, wbr_ref[1])
                  + jax.nn.sigmoid(gl_ref[0, rows, 2 * d:3 * d].astype(_F32)) * _dot(y_ssm, wbr_ref[2]))
        out = _dot(merged.astype(_BF16), wout_ref[...])
        y = DEEPNORM_ALPHA * h + gate * out
        out_ref[0, rows, :] = _layer_norm(y, g_ref[...], b_ref[...])


def _merge_call(h, ada, y_mla, dil_o, dil_lse, y_s, proj_main, wglu, bglu, wbr, wout, ln_g, ln_b):
    bsz, s, d = h.shape
    tm = 2 * PERM_TILE
    tok = lambda b, i: (b, i, 0)
    grp = lambda b, i: (b, 0, i, 0)
    const2 = lambda b, i: (0, 0)
    bw = BRANCH_WIDTH
    d1, d2 = DIL_PAIRS[1][1], DIL_PAIRS[2][1]
    pt1 = jnp.asarray(_perm_matrix(d1).T, _BF16)
    pt2 = jnp.asarray(_perm_matrix(d2).T, _BF16)
    return pl.pallas_call(
        _merge_kernel,
        grid=(bsz, s // tm),
        in_specs=[
            pl.BlockSpec((1, tm, d), tok),
            pl.BlockSpec((1, 9, d), lambda b, i: (b, 0, 0)),
            pl.BlockSpec((1, tm, bw), tok),
            pl.BlockSpec((1, 1, tm, bw), grp),
            pl.BlockSpec((1, d1, tm // d1, bw), grp),
            pl.BlockSpec((1, d2, tm // d2, bw), grp),
            pl.BlockSpec((1, 1, tm, LANES), grp),
            pl.BlockSpec((1, d1, tm // d1, LANES), grp),
            pl.BlockSpec((1, d2, tm // d2, LANES), grp),
            pl.BlockSpec((PERM_TILE, PERM_TILE), const2),
            pl.BlockSpec((PERM_TILE, PERM_TILE), const2),
            pl.BlockSpec((1, tm, bw), tok),
            pl.BlockSpec((1, tm, N_BRANCH * d), lambda b, i: (b, i, COL_GATES // (N_BRANCH * d))),
            pl.BlockSpec((bw, bw), const2),
            pl.BlockSpec((1, bw), const2),
            pl.BlockSpec((N_BRANCH, bw, d), lambda b, i: (0, 0, 0)),
            pl.BlockSpec((d, d), const2),
            pl.BlockSpec((1, d), const2),
            pl.BlockSpec((1, d), const2),
        ],
        out_specs=pl.BlockSpec((1, tm, d), tok),
        out_shape=jax.ShapeDtypeStruct((bsz, s, d), _F32),
        compiler_params=_cparams(2),
        name="merge_sublayer",
    )(h, ada, y_mla, dil_o[0], dil_o[1], dil_o[2], dil_lse[0], dil_lse[1], dil_lse[2], pt1, pt2, y_s,
      proj_main, wglu, bglu, wbr, wout, ln_g, ln_b)


def _in_proj_weights(w_in, b_in):
    d = w_in.shape[0]
    o_kr = MLA_Q_RANK + MLA_KV_RANK
    o_dil = o_kr + MLA_ROPE_DIM
    o_u = o_dil + 3 * DIL_HEADS * DIL_HEAD_DIM
    o_g = o_u + SSM_WIDTH

    def dil_cols(arr, part, group):
        start = o_dil + part * DIL_HEADS * DIL_HEAD_DIM + group * DIL_GROUP_WIDTH
        cols = arr[..., start:start + DIL_GROUP_WIDTH]
        return cols * (DIL_SCALE * LOG2E) if part == 0 else cols

    def main(arr):
        pad = jnp.zeros(arr.shape[:-1] + (COL_Q1 - o_dil,), arr.dtype)
        return jnp.concatenate([arr[..., :o_dil], pad, dil_cols(arr, 0, 0), dil_cols(arr, 1, 0),
                                dil_cols(arr, 2, 0), arr[..., o_u:o_g], arr[..., o_g:]], axis=-1)

    def group(arr, g):
        return jnp.concatenate([dil_cols(arr, 0, g), dil_cols(arr, 1, g), dil_cols(arr, 2, g)], axis=-1)

    b2 = b_in.astype(_F32).reshape(1, -1)
    return ((main(w_in).astype(_BF16), main(b2)),
            (group(w_in, 1).astype(_BF16), group(b2, 1)),
            (group(w_in, 2).astype(_BF16), group(b2, 2)))


def _mla_weights(w_qb, w_kvb):
    qd = MLA_NOPE_DIM + MLA_ROPE_DIM
    half = MLA_ROPE_DIM // 2
    wq = w_qb.reshape(MLA_Q_RANK, MLA_HEADS, qd)
    zeros = jnp.zeros((MLA_Q_RANK, MLA_HEADS, MLA_HEAD_PAD - qd), w_qb.dtype)
    zn = jnp.zeros((MLA_Q_RANK, MLA_HEADS, MLA_NOPE_DIM), w_qb.dtype)
    wq_pad = jnp.concatenate([wq, zeros], axis=-1).reshape(MLA_Q_RANK, MLA_QK_WIDTH)
    t1 = wq[..., MLA_NOPE_DIM:MLA_NOPE_DIM + half]
    t2 = wq[..., MLA_NOPE_DIM + half:]
    wq_rot = jnp.concatenate([zn, -t2, t1, zeros], axis=-1).reshape(MLA_Q_RANK, MLA_QK_WIDTH)
    wkv = w_kvb.reshape(MLA_KV_RANK, MLA_HEADS, MLA_NOPE_DIM + MLA_V_DIM)
    zk = jnp.zeros((MLA_KV_RANK, MLA_HEADS, MLA_HEAD_PAD - MLA_NOPE_DIM), w_kvb.dtype)
    wk = jnp.concatenate([wkv[..., :MLA_NOPE_DIM], zk], axis=-1).reshape(MLA_KV_RANK, MLA_QK_WIDTH)
    wv = wkv[..., MLA_NOPE_DIM:].reshape(MLA_KV_RANK, MLA_HEADS // 2, 2, MLA_V_DIM)
    zv = jnp.zeros_like(wv[:, :, 0])
    wv_pad = jnp.concatenate([wv[:, :, 0], zv, zv, wv[:, :, 1]], axis=-1).reshape(MLA_KV_RANK, MLA_QK_WIDTH)
    return wq_pad.astype(_BF16), wq_rot.astype(_BF16), wk.astype(_BF16), wv_pad.astype(_BF16)


def _mla_ones_row():
    ones = np.zeros((1, MLA_QK_WIDTH), np.float32)
    for hd in range(MLA_HEADS):
        ones[0, hd * MLA_HEAD_PAD + (MLA_V_DIM if hd % 2 == 0 else 0)] = 1.0
    return jnp.asarray(ones)


def _rope_key_placement():
    half = MLA_ROPE_DIM // 2
    pk = np.zeros((LANES, MLA_QK_WIDTH), np.float32)
    pkr = np.zeros((LANES, MLA_QK_WIDTH), np.float32)
    for hd in range(MLA_HEADS):
        base = hd * MLA_HEAD_PAD + MLA_NOPE_DIM
        for i in range(MLA_ROPE_DIM):
            pk[i, base + i] = 1.0
        for i in range(half):
            pkr[half + i, base + i] = -1.0
            pkr[i, base + half + i] = 1.0
    return jnp.asarray(pk, _BF16), jnp.asarray(pkr, _BF16)


def _ffn_weights(w1, w3, w2):
    return w1.astype(_BF16), w3.astype(_BF16), w2.astype(_BF16)


def kernel(x, c, positions, w_ada, b_ada, ln_g, ln_b, ffn_w1, ffn_w3, ffn_w2, w_in, b_in, mla_q_norm, mla_kv_norm, mla_w_qb, mla_w_kvb, ssm_lambda_re, ssm_lambda_im, ssm_log_dt, ssm_b_re, ssm_b_im, ssm_c_re, ssm_c_im, ssm_d, ssm_w_glu, ssm_b_glu, w_br, w_out):
    bsz, s, d = x.shape
    assert d == D_MODEL and s % (DIL_PAIRS[2][1] * DIL_BAND) == 0, x.shape
    ada_all = _ada_call(c, w_ada, b_ada).reshape(DEPTH, bsz, 9, d)
    cos, sin = _rope_call(positions)
    pk, pkr = _rope_key_placement()
    ones_row = _mla_ones_row()
    h = x
    for l in range(DEPTH):
        ada = ada_all[l]
        h = _ffn_call(h, ada, 0, *_ffn_weights(ffn_w1[l, 0], ffn_w3[l, 0], ffn_w2[l, 0]),
                      ln_g[l, 0].reshape(1, d), ln_b[l, 0].reshape(1, d))
        (w_main, b_main), (w_g1, b_g1), (w_g2, b_g2) = _in_proj_weights(w_in[l], b_in[l])
        proj_main, proj_g1, proj_g2 = _proj_call(
            h, ada, w_main, b_main, [(DIL_PAIRS[1][1], w_g1, b_g1), (DIL_PAIRS[2][1], w_g2, b_g2)])
        proj_main4 = proj_main.reshape(bsz, 1, s, N_MAIN)
        wq, wqr, wk, wv = _mla_weights(mla_w_qb[l], mla_w_kvb[l])
        q, k, v = _mla_prep_call(proj_main, cos, sin, mla_q_norm[l].reshape(1, -1),
                                 mla_kv_norm[l].reshape(1, -1), wq, wqr, wk, wv, pk, pkr, ones_row)
        y_mla = _mla_attn_call(q, k, v)
        o0, lse0 = _dil_attn_call(proj_main4, COL_Q1 // DIL_GROUP_WIDTH, 0, DIL_PAIRS[0][1], 2048, 1)
        o1, lse1 = _dil_attn_call(proj_g1, 0, 1, DIL_PAIRS[1][1], 1024, 2)
        o2, lse2 = _dil_attn_call(proj_g2, 0, 2, DIL_PAIRS[2][1], 256, 16)
        y_s = _ssm_call(proj_main4, *_ssm_weights(ssm_lambda_re[l], ssm_lambda_im[l], ssm_log_dt[l],
                                                 ssm_b_re[l], ssm_b_im[l], ssm_c_re[l], ssm_c_im[l], ssm_d[l]))
        h = _merge_call(h, ada, y_mla, (o0, o1, o2), (lse0, lse1, lse2), y_s, proj_main,
                        ssm_w_glu[l].astype(_BF16), ssm_b_glu[l].reshape(1, -1), w_br[l].astype(_BF16),
                        w_out[l].astype(_BF16), ln_g[l, 1].reshape(1, d), ln_b[l, 1].reshape(1, d))
        h = _ffn_call(h, ada, 2, *_ffn_weights(ffn_w1[l, 1], ffn_w3[l, 1], ffn_w2[l, 1]),
                      ln_g[l, 2].reshape(1, d), ln_b[l, 2].reshape(1, d))
    return h
```

```python
import functools
import math

import numpy as np
import jax
import jax.numpy as jnp
from jax import lax
from jax.experimental import pallas as pl
from jax.experimental.pallas import tpu as pltpu

D_MODEL = 1024
DEPTH = 4
D_FF = 2816
MLA_HEADS = 8
MLA_Q_RANK = 384
MLA_KV_RANK = 256
MLA_NOPE_DIM = 64
MLA_ROPE_DIM = 32
MLA_V_DIM = 64
ROPE_THETA = 10000.0
DIL_PAIRS = ((128, 1), (512, 4), (2048, 16))
DIL_HPG = 4
DIL_HEADS = DIL_HPG * len(DIL_PAIRS)
DIL_HEAD_DIM = 128
DIL_BAND = 128
SSM_WIDTH = 512
SSM_GROUP_SIZE = 16
SSM_GROUPS = SSM_WIDTH // SSM_GROUP_SIZE
SSM_STATE = 64
N_BRANCH = 3
BRANCH_WIDTH = 512
DEEPNORM_ALPHA = (2 * DEPTH) ** 0.25
MACARON_WEIGHT = 0.5
LN_EPS = 1e-5
RMS_EPS = 1e-6

LANES = 128
V7X_VMEM_LIMIT_BYTES = 56 * 1024 * 1024

FF_CHUNK = 256
FFN_ROWS = 512
N_FF_CHUNKS = D_FF // FF_CHUNK
MLA_HEAD_PAD = 128
MLA_QK_WIDTH = MLA_HEADS * MLA_HEAD_PAD
MLA_V_WIDTH = MLA_HEADS * MLA_V_DIM
MLA_SCALE = (MLA_NOPE_DIM + MLA_ROPE_DIM) ** -0.5
DIL_SCALE = DIL_HEAD_DIM ** -0.5
DIL_GROUP_WIDTH = DIL_HPG * DIL_HEAD_DIM
MLA_A_WIDTH = 768
COL_Q1 = 1024
COL_U = COL_Q1 + 3 * DIL_GROUP_WIDTH
COL_GATES = COL_U + SSM_WIDTH
N_MAIN = COL_GATES + N_BRANCH * D_MODEL
PROJ_CHUNK = 512
PERM_TILE = 256
LOG2E = math.log2(math.e)
MLA_ROW_CHUNK = 64
MLA_HEADS_PER_STEP = 4
SSM_BLOCKS = 4
SSM_BLOCK_CH = SSM_WIDTH // SSM_BLOCKS
SSM_BLOCK_STATES = SSM_GROUPS * SSM_STATE // SSM_BLOCKS
SSM_SLABS = SSM_BLOCK_STATES // LANES
SSM_TT = 256
SSM_PITCH = 260

_F32 = jnp.float32
_BF16 = jnp.bfloat16
_NEG = -1e30


def _cparams(n_axes):
    return pltpu.CompilerParams(
        dimension_semantics=("arbitrary",) * n_axes,
        vmem_limit_bytes=V7X_VMEM_LIMIT_BYTES,
    )


def _resident(block_shape, index_map):
    return pl.BlockSpec(block_shape, index_map, pipeline_mode=pl.Buffered(1))


def _layer_norm(y, g, b):
    mu = jnp.mean(y, axis=-1, keepdims=True)
    yc = y - mu
    var = jnp.mean(yc * yc, axis=-1, keepdims=True)
    return yc * lax.rsqrt(var + LN_EPS) * g + b


def _dot(a, b):
    return jnp.dot(a, b, preferred_element_type=_F32)


def _dot_nt(a, b):
    return lax.dot_general(a, b, (((1,), (1,)), ((), ())), preferred_element_type=_F32)


def _ada_kernel(c_ref, w_ref, b_ref, o_ref):
    c = c_ref[...]
    cond = (c * jax.nn.sigmoid(c)).astype(_BF16)
    o_ref[0] = _dot(cond, w_ref[0].astype(_BF16)) + b_ref[0]


def _ada_call(c, w_ada, b_ada):
    depth, d, n = w_ada.shape
    bsz = c.shape[0]
    tn = 1536
    return pl.pallas_call(
        _ada_kernel,
        grid=(depth, n // tn),
        in_specs=[
            pl.BlockSpec((bsz, d), lambda l, j: (0, 0)),
            pl.BlockSpec((1, d, tn), lambda l, j: (l, 0, j)),
            pl.BlockSpec((1, 1, tn), lambda l, j: (l, 0, j)),
        ],
        out_specs=pl.BlockSpec((1, bsz, tn), lambda l, j: (l, 0, j)),
        out_shape=jax.ShapeDtypeStruct((depth, bsz, n), _F32),
        compiler_params=_cparams(2),
        name="ada",
    )(c, w_ada, b_ada.reshape(depth, 1, n))


def _rope_freq_lanes():
    half = MLA_ROPE_DIM // 2
    inv_freq = np.power(np.float32(ROPE_THETA), -np.arange(half, dtype=np.float32) / np.float32(half))
    f = np.zeros((1, MLA_HEAD_PAD), np.float32)
    f[0, MLA_NOPE_DIM:MLA_NOPE_DIM + half] = inv_freq
    f[0, MLA_NOPE_DIM + half:MLA_NOPE_DIM + 2 * half] = inv_freq
    return f


def _rope_kernel(pos_ref, f_ref, cos_ref, sin_ref):
    ang = pos_ref[0].astype(_F32) * f_ref[...]
    cos_ref[0] = jnp.cos(ang)
    sin_ref[0] = jnp.sin(ang)


def _rope_call(positions):
    bsz, s = positions.shape
    tm = min(s, 512)
    out = jax.ShapeDtypeStruct((bsz, s, MLA_HEAD_PAD), _F32)
    return pl.pallas_call(
        _rope_kernel,
        grid=(bsz, s // tm),
        in_specs=[
            pl.BlockSpec((1, tm, 1), lambda b, i: (b, i, 0)),
            pl.BlockSpec((1, MLA_HEAD_PAD), lambda b, i: (0, 0)),
        ],
        out_specs=[pl.BlockSpec((1, tm, MLA_HEAD_PAD), lambda b, i: (b, i, 0))] * 2,
        out_shape=[out, out],
        compiler_params=_cparams(2),
        name="rope_tables",
    )(positions.reshape(bsz, s, 1), jnp.asarray(_rope_freq_lanes()))


def _ffn_kernel(sub, h_ref, ada_ref, w1_ref, w3_ref, w2_ref, g_ref, b_ref, o_ref):
    shift = ada_ref[0, 3 * sub + 0:3 * sub + 1, :]
    scale = ada_ref[0, 3 * sub + 1:3 * sub + 2, :]
    gate = ada_ref[0, 3 * sub + 2:3 * sub + 3, :]
    for r0 in range(0, h_ref.shape[1], FFN_ROWS):
        rows = slice(r0, r0 + FFN_ROWS)
        h = h_ref[0, rows, :]
        xm = (h * (1.0 + scale) + shift).astype(_BF16)
        acc = jnp.zeros(h.shape, _F32)
        for j in range(N_FF_CHUNKS):
            cols = slice(j * FF_CHUNK, (j + 1) * FF_CHUNK)
            a = _dot(xm, w1_ref[:, cols])
            b = _dot(xm, w3_ref[:, cols])
            g = (a * jax.nn.sigmoid(a) * b).astype(_BF16)
            acc = acc + _dot(g, w2_ref[cols, :])
        y = DEEPNORM_ALPHA * h + MACARON_WEIGHT * gate * acc
        o_ref[0, rows, :] = _layer_norm(y, g_ref[...], b_ref[...])


def _ffn_call(h, ada, sub, w1, w3, w2, ln_g, ln_b):
    bsz, s, d = h.shape
    tm = min(s, 2 * FFN_ROWS)
    return pl.pallas_call(
        functools.partial(_ffn_kernel, sub),
        grid=(bsz, s // tm),
        in_specs=[
            pl.BlockSpec((1, tm, d), lambda b, i: (b, i, 0)),
            pl.BlockSpec((1, 9, d), lambda b, i: (b, 0, 0)),
            _resident((d, D_FF), lambda b, i: (0, 0)),
            _resident((d, D_FF), lambda b, i: (0, 0)),
            _resident((D_FF, d), lambda b, i: (0, 0)),
            pl.BlockSpec((1, d), lambda b, i: (0, 0)),
            pl.BlockSpec((1, d), lambda b, i: (0, 0)),
        ],
        out_specs=pl.BlockSpec((1, tm, d), lambda b, i: (b, i, 0)),
        out_shape=jax.ShapeDtypeStruct((bsz, s, d), _F32),
        compiler_params=_cparams(2),
        name="ffn_sublayer",
    )(h, ada, w1, w3, w2, ln_g, ln_b)


def _perm_matrix(dil):
    n = PERM_TILE // dil
    p = np.zeros((PERM_TILE, PERM_TILE), np.float32)
    for r in range(dil):
        for i in range(n):
            p[r * n + i, i * dil + r] = 1.0
    return p


def _proj_kernel(tm, dils, x_ref, ada_ref, wm_ref, bm_ref, *rest):
    ngrp = len(dils)
    grp_in = rest[:2 * ngrp]
    om_ref = rest[2 * ngrp]
    grp_out = rest[2 * ngrp + 1:2 * ngrp + 1 + ngrp]
    xs_ref, xp_ref = rest[2 * ngrp + 1 + ngrp:]
    x = x_ref[0]
    shift = ada_ref[0, 3:4, :]
    scale = ada_ref[0, 4:5, :]
    xm_f32 = x * (1.0 + scale) + shift
    xm = xm_f32.astype(_BF16)
    n_main = wm_ref.shape[1]
    for n0 in range(0, n_main, PROJ_CHUNK):
        n1 = min(n0 + PROJ_CHUNK, n_main)
        om_ref[0, :, n0:n1] = (_dot(xm, wm_ref[:, n0:n1]) + bm_ref[:, n0:n1]).astype(_BF16)
    nslab = x.shape[1] // LANES
    for k in range(nslab):
        xs_ref[k] = xm_f32[:, k * LANES:(k + 1) * LANES]
    for g, dil in enumerate(dils):
        w_ref, b_ref = grp_in[2 * g:2 * g + 2]
        o_ref = grp_out[g]
        n = tm // dil
        n_out = w_ref.shape[1]
        for r in range(dil):
            for k in range(nslab):
                xp_ref[g, r * n:(r + 1) * n, k * LANES:(k + 1) * LANES] = (
                    xs_ref[k, pl.ds(r, n, stride=dil), :].astype(_BF16))
        xp = xp_ref[g]
        for n0 in range(0, n_out, PROJ_CHUNK):
            n1 = min(n0 + PROJ_CHUNK, n_out)
            res = (_dot(xp, w_ref[:, n0:n1]) + b_ref[:, n0:n1]).astype(_BF16)
            for r in range(dil):
                o_ref[0, r, :, n0:n1] = res[r * n:(r + 1) * n]


def _proj_call(h, ada, w_main, b_main, groups):
    bsz, s, d = h.shape
    tm = min(s, 512)
    dils = tuple(g[0] for g in groups)
    const = lambda b, i: (0, 0)
    in_specs = [
        pl.BlockSpec((1, tm, d), lambda b, i: (b, i, 0)),
        pl.BlockSpec((1, 9, d), lambda b, i: (b, 0, 0)),
        _resident(w_main.shape, const),
        pl.BlockSpec(b_main.shape, const),
    ]
    args = [h, ada, w_main, b_main]
    out_specs = [pl.BlockSpec((1, tm, w_main.shape[1]), lambda b, i: (b, i, 0))]
    out_shape = [jax.ShapeDtypeStruct((bsz, s, w_main.shape[1]), _BF16)]
    for dil, w, bias in groups:
        in_specs += [_resident(w.shape, const), pl.BlockSpec(bias.shape, const)]
        args += [w, bias]
        out_specs.append(pl.BlockSpec((1, dil, tm // dil, w.shape[1]), lambda b, i: (b, 0, i, 0)))
        out_shape.append(jax.ShapeDtypeStruct((bsz, dil, s // dil, w.shape[1]), _BF16))
    return pl.pallas_call(
        functools.partial(_proj_kernel, tm, dils),
        grid=(bsz, s // tm),
        in_specs=in_specs,
        out_specs=out_specs,
        out_shape=out_shape,
        scratch_shapes=[pltpu.VMEM((d // LANES, tm, LANES), _F32), pltpu.VMEM((len(groups), tm, d), _BF16)],
        compiler_params=_cparams(2),
        name="in_proj",
    )(*args)


def _mla_prep_kernel(a_ref, cos_ref, sin_ref, qg_ref, kvg_ref, wq_ref, wqr_ref, wk_ref, wv_ref,
                     pk_ref, pkr_ref, ones_ref, q_ref, k_ref, v_ref):
    a = a_ref[0]
    qa = a[:, :MLA_Q_RANK].astype(_F32)
    kva = a[:, MLA_Q_RANK:MLA_Q_RANK + MLA_KV_RANK].astype(_F32)
    kr = a[:, MLA_Q_RANK + MLA_KV_RANK:]
    qn = (qa * lax.rsqrt(jnp.mean(qa * qa, axis=-1, keepdims=True) + RMS_EPS) * qg_ref[...]).astype(_BF16)
    kvn = (kva * lax.rsqrt(jnp.mean(kva * kva, axis=-1, keepdims=True) + RMS_EPS) * kvg_ref[...]).astype(_BF16)
    cos = jnp.concatenate([cos_ref[0]] * 2, axis=1) * (MLA_SCALE * LOG2E)
    sin = jnp.concatenate([sin_ref[0]] * 2, axis=1) * (MLA_SCALE * LOG2E)
    kcos = jnp.concatenate([cos_ref[0]] * 2, axis=1)
    ksin = jnp.concatenate([sin_ref[0]] * 2, axis=1)
    for c in range(MLA_HEADS // 2):
        sl = slice(2 * c * MLA_HEAD_PAD, 2 * (c + 1) * MLA_HEAD_PAD)
        q = _dot(qn, wq_ref[:, sl])
        qrot = _dot(qn, wqr_ref[:, sl])
        q_ref[0, :, sl] = (q * cos + qrot * sin).astype(_BF16)
        k = _dot(kvn, wk_ref[:, sl]) + _dot(kr, pk_ref[:, sl])
        krot = _dot(kr, pkr_ref[:, sl])
        k_ref[0, :, sl] = (k * kcos + krot * ksin).astype(_BF16)
        v_ref[0, :, sl] = (_dot(kvn, wv_ref[:, sl]) + ones_ref[:, sl]).astype(_BF16)


def _mla_prep_call(proj_main, cos, sin, qg, kvg, wq, wqr, wk, wv, pk, pkr, ones_row):
    bsz, s, _ = proj_main.shape
    tm = min(s, 512)
    const = lambda b, i: (0, 0)
    return pl.pallas_call(
        _mla_prep_kernel,
        grid=(bsz, s // tm),
        in_specs=[
            pl.BlockSpec((1, tm, MLA_A_WIDTH), lambda b, i: (b, i, 0)),
            pl.BlockSpec((1, tm, MLA_HEAD_PAD), lambda b, i: (b, i, 0)),
            pl.BlockSpec((1, tm, MLA_HEAD_PAD), lambda b, i: (b, i, 0)),
            pl.BlockSpec((1, MLA_Q_RANK), const),
            pl.BlockSpec((1, MLA_KV_RANK), const),
            pl.BlockSpec((MLA_Q_RANK, MLA_QK_WIDTH), const),
            pl.BlockSpec((MLA_Q_RANK, MLA_QK_WIDTH), const),
            pl.BlockSpec((MLA_KV_RANK, MLA_QK_WIDTH), const),
            pl.BlockSpec((MLA_KV_RANK, MLA_QK_WIDTH), const),
            pl.BlockSpec((LANES, MLA_QK_WIDTH), const),
            pl.BlockSpec((LANES, MLA_QK_WIDTH), const),
            pl.BlockSpec((1, MLA_QK_WIDTH), const),
        ],
        out_specs=[pl.BlockSpec((1, tm, MLA_QK_WIDTH), lambda b, i: (b, i, 0))] * 3,
        out_shape=[jax.ShapeDtypeStruct((bsz, s, MLA_QK_WIDTH), _BF16)] * 3,
        compiler_params=_cparams(2),
        name="mla_prep",
    )(proj_main, cos, sin, qg, kvg, wq, wqr, wk, wv, pk, pkr, ones_row)


def _mla_attn_kernel(tq, q_ref, k_ref, v_ref, o_ref, s_ref, p_ref, m_ref, acc_ref):
    nh = MLA_HEADS_PER_STEP
    half = tq // 2
    qi = pl.program_id(2)
    row = lax.broadcasted_iota(jnp.int32, (tq, tq), 0)
    col = lax.broadcasted_iota(jnp.int32, (tq, tq), 1)
    causal = col <= row
    lanes = [slice(e * MLA_HEAD_PAD, (e + 1) * MLA_HEAD_PAD) for e in range(nh)]
    qs = [q_ref[0, :, lanes[e]] for e in range(nh)]

    def scores(j, slot):
        start = pl.multiple_of(j * tq, tq)
        for e in range(nh):
            s_ref[slot, e] = _dot_nt(qs[e], k_ref[0, pl.ds(start, tq), lanes[e]])

    def softmax_pv(j, slot, masked):
        start = pl.multiple_of(j * tq, tq)
        for e in range(nh):
            alphas = []
            for c in range(tq // MLA_ROW_CHUNK):
                rows = slice(c * MLA_ROW_CHUNK, (c + 1) * MLA_ROW_CHUNK)
                cols = slice(0, half if masked and (c + 1) * MLA_ROW_CHUNK <= half else tq)
                s = s_ref[slot, e, rows, cols]
                if masked:
                    s = jnp.where(causal[rows, cols], s, _NEG)
                m_old = m_ref[e, rows, :]
                m_c = jnp.maximum(m_old, jnp.max(s, axis=-1, keepdims=True))
                p_ref[e, rows, cols] = jnp.exp2(s - m_c).astype(_BF16)
                m_ref[e, rows, :] = m_c
                alphas.append(jnp.exp2(m_old - m_c))
            if masked:
                pv = jnp.concatenate([
                    _dot(p_ref[e, :half, :half], v_ref[0, pl.ds(start, half), lanes[e]]),
                    _dot(p_ref[e, half:, :], v_ref[0, pl.ds(start, tq), lanes[e]])], axis=0)
            else:
                pv = _dot(p_ref[e], v_ref[0, pl.ds(start, tq), lanes[e]])
            acc_ref[e] = jnp.concatenate(alphas, axis=0) * acc_ref[e] + pv

    def pair(jj, carry):
        t = 2 * jj
        scores(t + 1, 1)
        softmax_pv(t, 0, False)
        scores(t + 2, 0)
        softmax_pv(t + 1, 1, False)
        return carry

    m_ref[...] = jnp.full(m_ref.shape, _NEG, _F32)
    acc_ref[...] = jnp.zeros(acc_ref.shape, _F32)
    scores(0, 0)
    lax.fori_loop(0, qi // 2, pair, 0)

    @pl.when(qi % 2 == 1)
    def _():
        scores(qi, 1)
        softmax_pv(qi - 1, 0, False)
        softmax_pv(qi, 1, True)

    @pl.when(qi % 2 == 0)
    def _():
        softmax_pv(qi, 0, True)

    lane = lax.broadcasted_iota(jnp.int32, (tq, MLA_HEAD_PAD), 1)
    for e2 in range(nh // 2):
        acc0, acc1 = acc_ref[2 * e2], acc_ref[2 * e2 + 1]
        out0 = acc0 * (1.0 / acc0[:, MLA_V_DIM:MLA_V_DIM + 1])
        out1 = acc1 * (1.0 / acc1[:, 0:1])
        o_ref[0, :, lanes[e2]] = jnp.where(lane < MLA_V_DIM, out0, out1).astype(_BF16)


def _mla_attn_call(q, k, v):
    bsz, s, _ = q.shape
    tq = min(s, 512)
    nh = MLA_HEADS_PER_STEP
    qk_w = nh * MLA_HEAD_PAD
    return pl.pallas_call(
        functools.partial(_mla_attn_kernel, tq),
        grid=(bsz, MLA_HEADS // nh, s // tq),
        in_specs=[
            pl.BlockSpec((1, tq, qk_w), lambda b, h, i: (b, i, h)),
            pl.BlockSpec((1, s, qk_w), lambda b, h, i: (b, 0, h)),
            pl.BlockSpec((1, s, qk_w), lambda b, h, i: (b, 0, h)),
        ],
        out_specs=pl.BlockSpec((1, tq, nh * MLA_V_DIM), lambda b, h, i: (b, i, h)),
        out_shape=jax.ShapeDtypeStruct((bsz, s, MLA_V_WIDTH), _BF16),
        scratch_shapes=[pltpu.VMEM((2, nh, tq, tq), _F32), pltpu.VMEM((nh, tq, tq), _BF16),
                        pltpu.VMEM((nh, tq, 1), _F32), pltpu.VMEM((nh, tq, MLA_HEAD_PAD), _F32)],
        compiler_params=_cparams(3),
        name="mla_attention",
    )(q, k, v)


def _alibi_slope(head):
    return float(np.exp2(np.float32(-8.0) * (np.float32(head) + np.float32(1.0)) / np.float32(DIL_HEADS)))


def _dil_attn_kernel(tq, kw, single_tile, nres, group, dil, q_ref, k_ref, v_ref, o_ref, lse_ref, bias_ref):
    qi = pl.program_id(2)
    first_step = (pl.program_id(0) == 0) & (pl.program_id(1) == 0) & (qi == 0)
    nb = DIL_BAND

    @pl.when(first_step)
    def _():
        row = lax.broadcasted_iota(jnp.int32, (nb, kw), 0)
        col = lax.broadcasted_iota(jnp.int32, (nb, kw), 1)
        for variant, rel in ((0, nb + row - col), (1, row - col)):
            ok = (rel >= 0) & (rel <= DIL_BAND)
            relf = rel.astype(_F32)
            for i in range(DIL_HPG):
                slope = _alibi_slope(group * DIL_HPG + i) * dil * LOG2E
                bias_ref[i, variant] = jnp.where(ok, -slope * relf, _NEG)

    q0 = pl.multiple_of(qi * tq, tq)
    lane = lax.broadcasted_iota(jnp.int32, (nb, LANES), 1)
    for rr in range(nres):
        for sb in range(tq // nb):
            rows = slice(sb * nb, (sb + 1) * nb)
            if sb == 0:
                start = pl.multiple_of(jnp.maximum(q0 - nb, 0), nb)
            else:
                start = pl.multiple_of(q0 + (sb - 1) * nb, nb)
            m_all = jnp.zeros((nb, LANES), _F32)
            l_all = jnp.ones((nb, LANES), _F32)
            for i in range(DIL_HPG):
                lanes = slice(i * DIL_HEAD_DIM, (i + 1) * DIL_HEAD_DIM)
                if sb > 0:
                    bias = bias_ref[i, 0]
                elif single_tile:
                    bias = bias_ref[i, 1]
                else:
                    bias = jnp.where(qi == 0, bias_ref[i, 1], bias_ref[i, 0])
                s = _dot_nt(q_ref[0, rr, rows, lanes], k_ref[0, rr, pl.ds(start, kw), lanes]) + bias
                m = jnp.max(s, axis=-1, keepdims=True)
                e = jnp.exp2(s - m)
                l = jnp.sum(e, axis=-1, keepdims=True)
                o = _dot(e.astype(_BF16), v_ref[0, rr, pl.ds(start, kw), lanes])
                o_ref[0, rr, rows, lanes] = (o * (1.0 / l)).astype(_BF16)
                m_all = jnp.where(lane == i, m, m_all)
                l_all = jnp.where(lane == i, l, l_all)
            lse_ref[0, rr, rows, :] = (m_all + jnp.log2(l_all)) * (1.0 / LOG2E)


def _dil_attn_call(arr, col_block0, group, dil, tq, nres):
    bsz, _, length, _ = arr.shape
    w = DIL_GROUP_WIDTH
    tq = min(tq, length)
    kw = min(2 * DIL_BAND, length)
    nres = min(nres, dil)
    return pl.pallas_call(
        functools.partial(_dil_attn_kernel, tq, kw, length == tq, nres, group, dil),
        grid=(bsz, dil // nres, length // tq),
        in_specs=[
            pl.BlockSpec((1, nres, tq, w), lambda b, r, i: (b, r, i, col_block0)),
            pl.BlockSpec((1, nres, length, w), lambda b, r, i: (b, r, 0, col_block0 + 1)),
            pl.BlockSpec((1, nres, length, w), lambda b, r, i: (b, r, 0, col_block0 + 2)),
        ],
        out_specs=[
            pl.BlockSpec((1, nres, tq, w), lambda b, r, i: (b, r, i, 0)),
            pl.BlockSpec((1, nres, tq, LANES), lambda b, r, i: (b, r, i, 0)),
        ],
        out_shape=[
            jax.ShapeDtypeStruct((bsz, dil, length, w), _BF16),
            jax.ShapeDtypeStruct((bsz, dil, length, LANES), _F32),
        ],
        scratch_shapes=[pltpu.VMEM((DIL_HPG, 2, DIL_BAND, kw), _F32)],
        compiler_params=_cparams(3),
        name=f"dil_attention_g{group}",
    )(arr, arr, arr)


def _ssm_kernel(nb, u_ref, bw_ref, cw_ref, are_ref, aim_ref, d_ref, o_ref, bx_ref, st_ref):
    ti = pl.program_id(1)

    @pl.when(ti == 0)
    def _():
        st_ref[...] = jnp.zeros(st_ref.shape, _F32)

    for b in range(nb):
        bu = _dot(u_ref[b, 0], bw_ref[0])
        for sidx in range(2 * SSM_SLABS):
            bx_ref[sidx, b * SSM_PITCH:b * SSM_PITCH + SSM_TT, :] = bu[:, sidx * LANES:(sidx + 1) * LANES]

    a_re = [jnp.broadcast_to(are_ref[0, :, k * LANES:(k + 1) * LANES], (nb, LANES)) for k in range(SSM_SLABS)]
    a_im = [jnp.broadcast_to(aim_ref[0, :, k * LANES:(k + 1) * LANES], (nb, LANES)) for k in range(SSM_SLABS)]

    def step(t, carry):
        xr, xi = carry
        nr, ni = [], []
        for k in range(SSM_SLABS):
            rows = pl.ds(t, nb, stride=SSM_PITCH)
            br = bx_ref[k, rows, :]
            bi = bx_ref[SSM_SLABS + k, rows, :]
            r = a_re[k] * xr[k] - a_im[k] * xi[k] + br
            i = a_re[k] * xi[k] + a_im[k] * xr[k] + bi
            bx_ref[k, rows, :] = r
            bx_ref[SSM_SLABS + k, rows, :] = i
            nr.append(r)
            ni.append(i)
        return tuple(nr), tuple(ni)

    x0 = (tuple(st_ref[k] for k in range(SSM_SLABS)),
          tuple(st_ref[SSM_SLABS + k] for k in range(SSM_SLABS)))
    xr, xi = lax.fori_loop(0, SSM_TT, step, x0, unroll=8)
    for k in range(SSM_SLABS):
        st_ref[k] = xr[k]
        st_ref[SSM_SLABS + k] = xi[k]

    dskip = d_ref[0]
    for b in range(nb):
        xs = jnp.concatenate(
            [bx_ref[sidx, b * SSM_PITCH:b * SSM_PITCH + SSM_TT, :] for sidx in range(2 * SSM_SLABS)],
            axis=1).astype(_BF16)
        y = _dot(xs, cw_ref[0]) + dskip * u_ref[b, 0].astype(_F32)
        o_ref[b] = jax.nn.gelu(y).astype(_BF16)


def _ssm_call(proj_main, bw, cw, a_re, a_im, d_skip):
    bsz, _, s, _ = proj_main.shape
    u_block0 = COL_U // SSM_BLOCK_CH
    nstate = 2 * SSM_BLOCK_STATES
    return pl.pallas_call(
        functools.partial(_ssm_kernel, bsz),
        grid=(SSM_BLOCKS, s // SSM_TT),
        in_specs=[
            pl.BlockSpec((bsz, 1, SSM_TT, SSM_BLOCK_CH), lambda m, t: (0, 0, t, u_block0 + m)),
            pl.BlockSpec((1, SSM_BLOCK_CH, nstate), lambda m, t: (m, 0, 0)),
            pl.BlockSpec((1, nstate, SSM_BLOCK_CH), lambda m, t: (m, 0, 0)),
            pl.BlockSpec((1, 1, SSM_BLOCK_STATES), lambda m, t: (m, 0, 0)),
            pl.BlockSpec((1, 1, SSM_BLOCK_STATES), lambda m, t: (m, 0, 0)),
            pl.BlockSpec((1, 1, SSM_BLOCK_CH), lambda m, t: (m, 0, 0)),
        ],
        out_specs=pl.BlockSpec((bsz, SSM_TT, SSM_BLOCK_CH), lambda m, t: (0, t, m)),
        out_shape=jax.ShapeDtypeStruct((bsz, s, SSM_WIDTH), _BF16),
        scratch_shapes=[
            pltpu.VMEM((2 * SSM_SLABS, bsz * SSM_PITCH, LANES), _F32),
            pltpu.VMEM((2 * SSM_SLABS, bsz, LANES), _F32),
        ],
        compiler_params=_cparams(2),
        name="s5_scan",
    )(proj_main, bw, cw, a_re, a_im, d_skip)


def _ssm_weights(lam_re, lam_im, log_dt, b_re, b_im, c_re, c_im, d_skip):
    lam = lax.complex(lam_re.astype(_F32), lam_im.astype(_F32))
    dt = jnp.exp(log_dt.astype(_F32))[:, None]
    lam_bar = jnp.exp(lam * dt)
    b_bar = ((lam_bar - 1.0) / lam)[..., None] * lax.complex(b_re.astype(_F32), b_im.astype(_F32))
    gpb = SSM_GROUPS // SSM_BLOCKS
    eye = jnp.eye(gpb, dtype=_F32)

    def in_weights(part):
        w = part.reshape(SSM_BLOCKS, gpb, SSM_STATE, SSM_GROUP_SIZE)
        w = jnp.einsum("mgph,gk->mghkp", w, eye)
        return w.reshape(SSM_BLOCKS, SSM_BLOCK_CH, SSM_BLOCK_STATES)

    def out_weights(part):
        w = part.reshape(SSM_BLOCKS, gpb, SSM_GROUP_SIZE, SSM_STATE)
        w = jnp.einsum("mgkp,gj->mgpjk", w, eye)
        return w.reshape(SSM_BLOCKS, SSM_BLOCK_STATES, SSM_BLOCK_CH)

    bw = jnp.concatenate([in_weights(jnp.real(b_bar)), in_weights(jnp.imag(b_bar))], axis=2).astype(_BF16)
    cw = jnp.concatenate([out_weights(c_re.astype(_F32)), -out_weights(c_im.astype(_F32))], axis=1).astype(_BF16)
    a_re = jnp.real(lam_bar).reshape(SSM_BLOCKS, 1, SSM_BLOCK_STATES)
    a_im = jnp.imag(lam_bar).reshape(SSM_BLOCKS, 1, SSM_BLOCK_STATES)
    return bw, cw, a_re, a_im, d_skip.astype(_F32).reshape(SSM_BLOCKS, 1, SSM_BLOCK_CH)


def _to_token_order(ref, t, pt_ref):
    _, dil, _, width = ref.shape
    n = PERM_TILE // dil
    x = ref[0, :, t * n:(t + 1) * n, :].reshape(PERM_TILE, width)
    if x.dtype == _BF16:
        return _dot(pt_ref[...], x)
    hi = x.astype(_BF16)
    r1 = x - hi.astype(_F32)
    mid = r1.astype(_BF16)
    lo = (r1 - mid.astype(_F32)).astype(_BF16)
    res = _dot(pt_ref[...], jnp.concatenate([hi, mid, lo], axis=1))
    return (res[:, :width] + res[:, width:2 * width]) + res[:, 2 * width:]


def _merge_kernel(h_ref, ada_ref, ymla_ref, o0_ref, o1_ref, o2_ref, l0_ref, l1_ref, l2_ref, pt1_ref, pt2_ref,
                  ys_ref, gl_ref, wglu_ref, bglu_ref, wbr_ref, wout_ref, g_ref, b_ref, out_ref):
    gate = ada_ref[0, 5:6, :]
    d = h_ref.shape[-1]
    for t in range(h_ref.shape[1] // PERM_TILE):
        rows = slice(t * PERM_TILE, (t + 1) * PERM_TILE)
        h = h_ref[0, rows, :]
        o0 = o0_ref[0, 0, rows, :].astype(_F32)
        o1 = _to_token_order(o1_ref, t, pt1_ref)
        o2 = _to_token_order(o2_ref, t, pt2_ref)
        l0 = l0_ref[0, 0, rows, :]
        l1 = _to_token_order(l1_ref, t, pt1_ref)
        l2 = _to_token_order(l2_ref, t, pt2_ref)
        m = jnp.maximum(jnp.maximum(l0, l1), l2)
        e0, e1, e2 = jnp.exp(l0 - m), jnp.exp(l1 - m), jnp.exp(l2 - m)
        inv = 1.0 / (e0 + e1 + e2)
        w0, w1, w2 = e0 * inv, e1 * inv, e2 * inv
        heads = []
        for i in range(DIL_HPG):
            lanes = slice(i * DIL_HEAD_DIM, (i + 1) * DIL_HEAD_DIM)
            heads.append(w0[:, i:i + 1] * o0[:, lanes] + w1[:, i:i + 1] * o1[:, lanes]
                         + w2[:, i:i + 1] * o2[:, lanes])
        y_dil = jnp.concatenate(heads, axis=1).astype(_BF16)
        ys = ys_ref[0, rows, :]
        glu = jax.nn.sigmoid(_dot(ys, wglu_ref[...]) + bglu_ref[...])
        y_ssm = (ys.astype(_F32) * glu).astype(_BF16)
        merged = (jax.nn.sigmoid(gl_ref[0, rows, 0:d].astype(_F32)) * _dot(ymla_ref[0, rows, :], wbr_ref[0])
                  + jax.nn.sigmoid(gl_ref[0, rows, d:2 * d].astype(_F32)) * _dot(y_dil, wbr_ref[1])
                  + jax.nn.sigmoid(gl_ref[0, rows, 2 * d:3 * d].astype(_F32)) * _dot(y_ssm, wbr_ref[2]))
        out = _dot(merged.astype(_BF16), wout_ref[...])
        y = DEEPNORM_ALPHA * h + gate * out
        out_ref[0, rows, :] = _layer_norm(y, g_ref[...], b_ref[...])


def _merge_call(h, ada, y_mla, dil_o, dil_lse, y_s, proj_main, wglu, bglu, wbr, wout, ln_g, ln_b):
    bsz, s, d = h.shape
    tm = 2 * PERM_TILE
    tok = lambda b, i: (b, i, 0)
    grp = lambda b, i: (b, 0, i, 0)
    const2 = lambda b, i: (0, 0)
    bw = BRANCH_WIDTH
    d1, d2 = DIL_PAIRS[1][1], DIL_PAIRS[2][1]
    pt1 = jnp.asarray(_perm_matrix(d1).T, _BF16)
    pt2 = jnp.asarray(_perm_matrix(d2).T, _BF16)
    return pl.pallas_call(
        _merge_kernel,
        grid=(bsz, s // tm),
        in_specs=[
            pl.BlockSpec((1, tm, d), tok),
            pl.BlockSpec((1, 9, d), lambda b, i: (b, 0, 0)),
            pl.BlockSpec((1, tm, bw), tok),
            pl.BlockSpec((1, 1, tm, bw), grp),
            pl.BlockSpec((1, d1, tm // d1, bw), grp),
            pl.BlockSpec((1, d2, tm // d2, bw), grp),
            pl.BlockSpec((1, 1, tm, LANES), grp),
            pl.BlockSpec((1, d1, tm // d1, LANES), grp),
            pl.BlockSpec((1, d2, tm // d2, LANES), grp),
            pl.BlockSpec((PERM_TILE, PERM_TILE), const2),
            pl.BlockSpec((PERM_TILE, PERM_TILE), const2),
            pl.BlockSpec((1, tm, bw), tok),
            pl.BlockSpec((1, tm, N_BRANCH * d), lambda b, i: (b, i, COL_GATES // (N_BRANCH * d))),
            pl.BlockSpec((bw, bw), const2),
            pl.BlockSpec((1, bw), const2),
            pl.BlockSpec((N_BRANCH, bw, d), lambda b, i: (0, 0, 0)),
            pl.BlockSpec((d, d), const2),
            pl.BlockSpec((1, d), const2),
            pl.BlockSpec((1, d), const2),
        ],
        out_specs=pl.BlockSpec((1, tm, d), tok),
        out_shape=jax.ShapeDtypeStruct((bsz, s, d), _F32),
        compiler_params=_cparams(2),
        name="merge_sublayer",
    )(h, ada, y_mla, dil_o[0], dil_o[1], dil_o[2], dil_lse[0], dil_lse[1], dil_lse[2], pt1, pt2, y_s,
      proj_main, wglu, bglu, wbr, wout, ln_g, ln_b)


def _in_proj_weights(w_in, b_in):
    d = w_in.shape[0]
    o_kr = MLA_Q_RANK + MLA_KV_RANK
    o_dil = o_kr + MLA_ROPE_DIM
    o_u = o_dil + 3 * DIL_HEADS * DIL_HEAD_DIM
    o_g = o_u + SSM_WIDTH

    def dil_cols(arr, part, group):
        start = o_dil + part * DIL_HEADS * DIL_HEAD_DIM + group * DIL_GROUP_WIDTH
        cols = arr[..., start:start + DIL_GROUP_WIDTH]
        return cols * (DIL_SCALE * LOG2E) if part == 0 else cols

    def main(arr):
        pad = jnp.zeros(arr.shape[:-1] + (COL_Q1 - o_dil,), arr.dtype)
        return jnp.concatenate([arr[..., :o_dil], pad, dil_cols(arr, 0, 0), dil_cols(arr, 1, 0),
                                dil_cols(arr, 2, 0), arr[..., o_u:o_g], arr[..., o_g:]], axis=-1)

    def group(arr, g):
        return jnp.concatenate([dil_cols(arr, 0, g), dil_cols(arr, 1, g), dil_cols(arr, 2, g)], axis=-1)

    b2 = b_in.astype(_F32).reshape(1, -1)
    return ((main(w_in).astype(_BF16), main(b2)),
            (group(w_in, 1).astype(_BF16), group(b2, 1)),
            (group(w_in, 2).astype(_BF16), group(b2, 2)))


def _mla_weights(w_qb, w_kvb):
    qd = MLA_NOPE_DIM + MLA_ROPE_DIM
    half = MLA_ROPE_DIM // 2
    wq = w_qb.reshape(MLA_Q_RANK, MLA_HEADS, qd)
    zeros = jnp.zeros((MLA_Q_RANK, MLA_HEADS, MLA_HEAD_PAD - qd), w_qb.dtype)
    zn = jnp.zeros((MLA_Q_RANK, MLA_HEADS, MLA_NOPE_DIM), w_qb.dtype)
    wq_pad = jnp.concatenate([wq, zeros], axis=-1).reshape(MLA_Q_RANK, MLA_QK_WIDTH)
    t1 = wq[..., MLA_NOPE_DIM:MLA_NOPE_DIM + half]
    t2 = wq[..., MLA_NOPE_DIM + half:]
    wq_rot = jnp.concatenate([zn, -t2, t1, zeros], axis=-1).reshape(MLA_Q_RANK, MLA_QK_WIDTH)
    wkv = w_kvb.reshape(MLA_KV_RANK, MLA_HEADS, MLA_NOPE_DIM + MLA_V_DIM)
    zk = jnp.zeros((MLA_KV_RANK, MLA_HEADS, MLA_HEAD_PAD - MLA_NOPE_DIM), w_kvb.dtype)
    wk = jnp.concatenate([wkv[..., :MLA_NOPE_DIM], zk], axis=-1).reshape(MLA_KV_RANK, MLA_QK_WIDTH)
    wv = wkv[..., MLA_NOPE_DIM:].reshape(MLA_KV_RANK, MLA_HEADS // 2, 2, MLA_V_DIM)
    zv = jnp.zeros_like(wv[:, :, 0])
    wv_pad = jnp.concatenate([wv[:, :, 0], zv, zv, wv[:, :, 1]], axis=-1).reshape(MLA_KV_RANK, MLA_QK_WIDTH)
    return wq_pad.astype(_BF16), wq_rot.astype(_BF16), wk.astype(_BF16), wv_pad.astype(_BF16)


def _mla_ones_row():
    ones = np.zeros((1, MLA_QK_WIDTH), np.float32)
    for hd in range(MLA_HEADS):
        ones[0, hd * MLA_HEAD_PAD + (MLA_V_DIM if hd % 2 == 0 else 0)] = 1.0
    return jnp.asarray(ones)


def _rope_key_placement():
    half = MLA_ROPE_DIM // 2
    pk = np.zeros((LANES, MLA_QK_WIDTH), np.float32)
    pkr = np.zeros((LANES, MLA_QK_WIDTH), np.float32)
    for hd in range(MLA_HEADS):
        base = hd * MLA_HEAD_PAD + MLA_NOPE_DIM
        for i in range(MLA_ROPE_DIM):
            pk[i, base + i] = 1.0
        for i in range(half):
            pkr[half + i, base + i] = -1.0
            pkr[i, base + half + i] = 1.0
    return jnp.asarray(pk, _BF16), jnp.asarray(pkr, _BF16)


def _ffn_weights(w1, w3, w2):
    return w1.astype(_BF16), w3.astype(_BF16), w2.astype(_BF16)


def kernel(x, c, positions, w_ada, b_ada, ln_g, ln_b, ffn_w1, ffn_w3, ffn_w2, w_in, b_in, mla_q_norm, mla_kv_norm, mla_w_qb, mla_w_kvb, ssm_lambda_re, ssm_lambda_im, ssm_log_dt, ssm_b_re, ssm_b_im, ssm_c_re, ssm_c_im, ssm_d, ssm_w_glu, ssm_b_glu, w_br, w_out):
    bsz, s, d = x.shape
    assert d == D_MODEL and s % (DIL_PAIRS[2][1] * DIL_BAND) == 0, x.shape
    ada_all = _ada_call(c, w_ada, b_ada).reshape(DEPTH, bsz, 9, d)
    cos, sin = _rope_call(positions)
    pk, pkr = _rope_key_placement()
    ones_row = _mla_ones_row()
    h = x
    for l in range(DEPTH):
        ada = ada_all[l]
        h = _ffn_call(h, ada, 0, *_ffn_weights(ffn_w1[l, 0], ffn_w3[l, 0], ffn_w2[l, 0]),
                      ln_g[l, 0].reshape(1, d), ln_b[l, 0].reshape(1, d))
        (w_main, b_main), (w_g1, b_g1), (w_g2, b_g2) = _in_proj_weights(w_in[l], b_in[l])
        proj_main, proj_g1, proj_g2 = _proj_call(
            h, ada, w_main, b_main, [(DIL_PAIRS[1][1], w_g1, b_g1), (DIL_PAIRS[2][1], w_g2, b_g2)])
        proj_main4 = proj_main.reshape(bsz, 1, s, N_MAIN)
        wq, wqr, wk, wv = _mla_weights(mla_w_qb[l], mla_w_kvb[l])
        q, k, v = _mla_prep_call(proj_main, cos, sin, mla_q_norm[l].reshape(1, -1),
                                 mla_kv_norm[l].reshape(1, -1), wq, wqr, wk, wv, pk, pkr, ones_row)
        y_mla = _mla_attn_call(q, k, v)
        o0, lse0 = _dil_attn_call(proj_main4, COL_Q1 // DIL_GROUP_WIDTH, 0, DIL_PAIRS[0][1], 4096, 1)
        o1, lse1 = _dil_attn_call(proj_g1, 0, 1, DIL_PAIRS[1][1], 1024, 4)
        o2, lse2 = _dil_attn_call(proj_g2, 0, 2, DIL_PAIRS[2][1], 256, 16)
        y_s = _ssm_call(proj_main4, *_ssm_weights(ssm_lambda_re[l], ssm_lambda_im[l], ssm_log_dt[l],
                                                 ssm_b_re[l], ssm_b_im[l], ssm_c_re[l], ssm_c_im[l], ssm_d[l]))
        h = _merge_call(h, ada, y_mla, (o0, o1, o2), (lse0, lse1, lse2), y_s, proj_main,
                        ssm_w_glu[l].astype(_BF16), ssm_b_glu[l].reshape(1, -1), w_br[l].astype(_BF16),
                        w_out[l].astype(_BF16), ln_g[l, 1].reshape(1, d), ln_b[l, 1].reshape(1, d))
        h = _ffn_call(h, ada, 2, *_ffn_weights(ffn_w1[l, 1], ffn_w3[l, 1], ffn_w2[l, 1]),
                      ln_g[l, 2].reshape(1, d), ln_b[l, 2].reshape(1, d))
    return h
```
